```python
import math
import jax, jax.numpy as jnp
from jax import lax
import numpy as np

D_MODEL = 1024
BATCH = 4
SEQ = 4096
DEPTH = 1

CHUNK = 128
SGU_WIDTH = D_MODEL
SGU_GROUPS = 4
SGU_GROUP_DIM = SGU_WIDTH // SGU_GROUPS
RET_HEADS = 4
RET_QK_DIM = 256
RET_V_DIM = 256
RET_WIDTH = RET_HEADS * RET_V_DIM
D_FF = 2816
ROPE_BASE = 10000.0
NORM_EPS = 1e-6
IN_WIDTHS = (SGU_WIDTH, SGU_WIDTH, RET_HEADS * RET_QK_DIM, RET_HEADS * RET_QK_DIM,
             RET_WIDTH, RET_WIDTH, D_MODEL, D_MODEL)
IN_WIDTH = sum(IN_WIDTHS)

kernel_name = "hybrid_sgu_retention_macaron_block"


def rms_norm(x, g):
    xf = x.astype(jnp.float32)
    y = xf * lax.rsqrt(jnp.mean(xf * xf, axis=-1, keepdims=True) + NORM_EPS)
    return (y * g.astype(jnp.float32)).astype(x.dtype)


def swiglu_ffn(h, w_gate, w_up, w_down):
    return (jax.nn.silu(h @ w_gate) * (h @ w_up)) @ w_down


def rotary(t):
    S, D = t.shape[1], t.shape[3]
    theta = ROPE_BASE ** (-jnp.arange(0, D, 2, dtype=jnp.float32) / D)
    ang = jnp.arange(S, dtype=jnp.float32)[:, None] * theta[None, :]
    cos = jnp.cos(ang)[None, :, None, :]
    sin = jnp.sin(ang)[None, :, None, :]
    t1, t2 = jnp.split(t.astype(jnp.float32), 2, axis=-1)
    return jnp.concatenate([t1 * cos - t2 * sin, t2 * cos + t1 * sin], axis=-1)


def spatial_gating(u, v, norm_g, norm_b, w_s, b_s):
    B, S, _ = v.shape
    vf = v.astype(jnp.float32)
    mu = jnp.mean(vf, axis=-1, keepdims=True)
    var = jnp.mean(jnp.square(vf - mu), axis=-1, keepdims=True)
    vn = (vf - mu) * lax.rsqrt(var + NORM_EPS) * norm_g + norm_b
    vc = vn.reshape(B, S // CHUNK, CHUNK, SGU_GROUPS, SGU_GROUP_DIM)
    s = jnp.einsum('gcm,bnmgd->bncgd', w_s.astype(jnp.float32), vc)
    s = s + b_s.astype(jnp.float32).T[None, None, :, :, None]
    return u * s.reshape(B, S, SGU_WIDTH).astype(u.dtype)


def retention_direction(q, k, v, log_gamma, include_diag):
    C = q.shape[3]
    idx = jnp.arange(C, dtype=jnp.float32)
    diff = idx[:, None] - idx[None, :]
    keep = (diff >= 0) if include_diag else (diff > 0)
    lg = log_gamma[:, None, None]
    decay = jnp.where(keep[None], jnp.exp(jnp.maximum(diff, 0.0)[None] * lg), 0.0)
    scores = jnp.einsum('bhncd,bhnmd->bhncm', q, k) * decay[None, :, None]
    intra = jnp.einsum('bhncm,bhnme->bhnce', scores, v)
    q_dec = q * jnp.exp((idx + 1.0)[None, :] * log_gamma[:, None])[None, :, None, :, None]
    k_dec = k * jnp.exp((C - 1.0 - idx)[None, :] * log_gamma[:, None])[None, :, None, :, None]
    chunk_decay = jnp.exp(C * log_gamma)[None, :, None, None]

    def step(state, xs):
        qn, kn, vn = xs
        out = jnp.einsum('bhcd,bhde->bhce', qn, state)
        state = state * chunk_decay + jnp.einsum('bhcd,bhce->bhde', kn, vn)
        return state, out

    B, H = q.shape[0], q.shape[1]
    init = jnp.zeros((B, H, q.shape[-1], v.shape[-1]), jnp.float32)
    xs = (jnp.moveaxis(q_dec, 2, 0), jnp.moveaxis(k_dec, 2, 0), jnp.moveaxis(v, 2, 0))
    _, cross = lax.scan(step, init, xs)
    return intra + jnp.moveaxis(cross, 0, 2)


def bidirectional_retention(q, k, v, decay_logit):
    B, S, H, dv = v.shape
    N = S // CHUNK
    log_gamma = jax.nn.log_sigmoid(decay_logit.astype(jnp.float32))

    def chunk(t):
        return jnp.transpose(t.astype(jnp.float32), (0, 2, 1, 3)).reshape(B, H, N, CHUNK, t.shape[-1])

    def rev(t):
        return jnp.flip(t, axis=1)

    fwd = retention_direction(chunk(q), chunk(k), chunk(v), log_gamma[0], True)
    bwd = retention_direction(chunk(rev(q)), chunk(rev(k)), chunk(rev(v)), log_gamma[1], False)
    out = fwd.reshape(B, H, S, dv) + jnp.flip(bwd.reshape(B, H, S, dv), axis=2)
    return jnp.transpose(out, (0, 2, 1, 3))


def setup_inputs(seed: int = 0) -> dict:
    key = jax.random.key(seed)
    ks = jax.random.split(key, 24)
    L, D = DEPTH, D_MODEL

    def nrm(k, shape, scale):
        return jax.random.normal(k, shape, jnp.float32) * scale

    base_logit = jnp.log(2.0 ** (5.0 + jnp.arange(RET_HEADS, dtype=jnp.float32)) - 1.0)
    return {
        "x": nrm(ks[0], (BATCH, SEQ, D), 1.0),
        "ffn1_norm": 1.0 + nrm(ks[1], (L, D), 0.02),
        "ffn1_w_gate": nrm(ks[2], (L, D, D_FF), D ** -0.5),
        "ffn1_w_up": nrm(ks[3], (L, D, D_FF), D ** -0.5),
        "ffn1_w_down": nrm(ks[4], (L, D_FF, D), D_FF ** -0.5),
        "mix_norm": 1.0 + nrm(ks[5], (L, D), 0.02),
        "w_in": nrm(ks[6], (L, D, IN_WIDTH), D ** -0.5),
        "b_in": nrm(ks[7], (L, IN_WIDTH), 0.02),
        "sgu_norm_g": 1.0 + nrm(ks[8], (L, SGU_WIDTH), 0.02),
        "sgu_norm_b": nrm(ks[9], (L, SGU_WIDTH), 0.02),
        "sgu_w_s": nrm(ks[10], (L, SGU_GROUPS, CHUNK, CHUNK), CHUNK ** -0.5),
        "sgu_b_s": 1.0 + nrm(ks[11], (L, SGU_GROUPS, CHUNK), 0.1),
        "ret_decay_logit": jnp.broadcast_to(base_logit, (L, 2, RET_HEADS)) + nrm(ks[12], (L, 2, RET_HEADS), 0.05),
        "w_branch_a": nrm(ks[13], (L, SGU_WIDTH, D), SGU_WIDTH ** -0.5),
        "w_branch_b": nrm(ks[14], (L, RET_WIDTH, D), RET_WIDTH ** -0.5),
        "w_out": nrm(ks[15], (L, D, D), D ** -0.5),
        "ffn2_norm": 1.0 + nrm(ks[16], (L, D), 0.02),
        "ffn2_w_gate": nrm(ks[17], (L, D, D_FF), D ** -0.5),
        "ffn2_w_up": nrm(ks[18], (L, D, D_FF), D ** -0.5),
        "ffn2_w_down": nrm(ks[19], (L, D_FF, D), D_FF ** -0.5),
        "final_norm": 1.0 + nrm(ks[20], (D,), 0.02),
    }


def reference(x, ffn1_norm, ffn1_w_gate, ffn1_w_up, ffn1_w_down, mix_norm, w_in, b_in,
              sgu_norm_g, sgu_norm_b, sgu_w_s, sgu_b_s, ret_decay_logit,
              w_branch_a, w_branch_b, w_out, ffn2_norm, ffn2_w_gate, ffn2_w_up, ffn2_w_down,
              final_norm):
    B, S, _ = x.shape
    split_at = list(np.cumsum(IN_WIDTHS)[:-1])
    for l in range(DEPTH):
        x = x + 0.5 * swiglu_ffn(rms_norm(x, ffn1_norm[l]), ffn1_w_gate[l], ffn1_w_up[l], ffn1_w_down[l])

        h = rms_norm(x, mix_norm[l])
        proj = h @ w_in[l] + b_in[l]
        u_a, v_a, q_r, k_r, v_r, g_r, gate_a, gate_b = jnp.split(proj, split_at, axis=-1)

        a = spatial_gating(jax.nn.gelu(u_a, approximate=False), jax.nn.gelu(v_a, approximate=False),
                           sgu_norm_g[l], sgu_norm_b[l], sgu_w_s[l], sgu_b_s[l])

        q = rotary(q_r.reshape(B, S, RET_HEADS, RET_QK_DIM))
        k = rotary(k_r.reshape(B, S, RET_HEADS, RET_QK_DIM)) * (RET_QK_DIM ** -0.5)
        v = v_r.reshape(B, S, RET_HEADS, RET_V_DIM)
        r = bidirectional_retention(q, k, v, ret_decay_logit[l])
        r = r * lax.rsqrt(jnp.mean(r * r, axis=-1, keepdims=True) + NORM_EPS)
        r = r.reshape(B, S, RET_WIDTH).astype(x.dtype) * jax.nn.silu(g_r)

        mix = jax.nn.sigmoid(gate_a) * (a @ w_branch_a[l]) + jax.nn.sigmoid(gate_b) * (r @ w_branch_b[l])
        x = x + mix @ w_out[l]

        x = x + 0.5 * swiglu_ffn(rms_norm(x, ffn2_norm[l]), ffn2_w_gate[l], ffn2_w_up[l], ffn2_w_down[l])
    return rms_norm(x, final_norm)
```

```python
import functools

import jax
import jax.numpy as jnp
import numpy as np
from jax import lax
from jax.experimental import pallas as pl
from jax.experimental.pallas import tpu as pltpu

D_MODEL = 1024
D_FF = 2816
CHUNK = 128
SGU_GROUPS = 4
SGU_GROUP_DIM = D_MODEL // SGU_GROUPS
RET_HEADS = 4
RET_DIM = 256
N_PROJ_SEGMENTS = 8
FORWARD, BACKWARD = 0, 1
ROPE_BASE = 10000.0
NORM_EPS = 1e-6

FF_COLS = 256
FFN_ROWS = 512
MIX_ROWS = 256
RET_ROWS = 512
MIB = 1024 * 1024

BF16 = jnp.bfloat16
F32 = jnp.float32


def _rms(x, g):
    return x * lax.rsqrt(jnp.mean(x * x, axis=-1, keepdims=True) + NORM_EPS) * g


def _gelu(x):
    return 0.5 * x * (1.0 + lax.erf(x * np.float32(np.sqrt(0.5))))


def _dot(a, b):
    return jnp.dot(a, b, preferred_element_type=F32)


def _resident(shape):
    return pl.BlockSpec(shape, lambda *_: (0,) * len(shape), pipeline_mode=pl.Buffered(1))


def _ffn_body(*refs, final_norm):
    if final_norm:
        x_ref, nw_ref, wg_ref, wu_ref, wd_ref, fn_ref, o_ref, act_ref = refs
    else:
        x_ref, nw_ref, wg_ref, wu_ref, wd_ref, o_ref, act_ref = refs
    x = x_ref[...]
    h = _rms(x, nw_ref[...]).astype(BF16)
    for c in range(D_FF // FF_COLS):
        cols = slice(c * FF_COLS, (c + 1) * FF_COLS)
        g = _dot(h, wg_ref[:, cols])
        u = _dot(h, wu_ref[:, cols])
        act_ref[:, cols] = (g * jax.nn.sigmoid(g) * u).astype(BF16)
    y = x + 0.5 * _dot(act_ref[...], wd_ref[...])
    if final_norm:
        y = _rms(y, fn_ref[...])
    o_ref[...] = y


def _ffn(x, norm_w, w_gate, w_up, w_down, final_w=None):
    t = x.shape[0]
    rows = pl.BlockSpec((FFN_ROWS, D_MODEL), lambda i: (i, 0))
    in_specs = [rows, _resident((1, D_MODEL)), _resident((D_MODEL, D_FF)), _resident((D_MODEL, D_FF)),
                _resident((D_FF, D_MODEL))]
    args = [x, norm_w, w_gate, w_up, w_down]
    if final_w is not None:
        in_specs.append(_resident((1, D_MODEL)))
        args.append(final_w)
    return pl.pallas_call(
        functools.partial(_ffn_body, final_norm=final_w is not None),
        grid=(t // FFN_ROWS,),
        in_specs=in_specs,
        out_specs=rows,
        out_shape=jax.ShapeDtypeStruct((t, D_MODEL), F32),
        scratch_shapes=[pltpu.VMEM((FFN_ROWS, D_FF), BF16)],
        compiler_params=pltpu.CompilerParams(dimension_semantics=("arbitrary",), vmem_limit_bytes=48 * MIB),
        name="ffn_final" if final_w is not None else "ffn",
    )(*args)


def _mixer_in_body(x_ref, nw_ref, w_ref, b_ref, lng_ref, lnb_ref, ws_ref, bs_ref, wa_ref, cos_ref, sin_ref,
                   mixa_ref, q_ref, k_ref, v_ref, sg_ref, sgb_ref, s_ref):
    h = _rms(x_ref[...], nw_ref[...]).astype(BF16)

    def proj(seg):
        cols = slice(seg * D_MODEL, (seg + 1) * D_MODEL)
        return _dot(h, w_ref[:, cols]) + b_ref[:, cols]

    va = _gelu(proj(1))
    mu = jnp.mean(va, axis=-1, keepdims=True)
    vc = va - mu
    var = jnp.mean(vc * vc, axis=-1, keepdims=True)
    vn = (vc * lax.rsqrt(var + NORM_EPS) * lng_ref[...] + lnb_ref[...]).astype(BF16)
    for j in range(MIX_ROWS // CHUNK):
        rows = slice(j * CHUNK, (j + 1) * CHUNK)
        for g in range(SGU_GROUPS):
            cols = slice(g * SGU_GROUP_DIM, (g + 1) * SGU_GROUP_DIM)
            s_ref[rows, cols] = _dot(ws_ref[g], vn[rows, cols]) + bs_ref[g]
    a = (_gelu(proj(0)) * s_ref[...]).astype(BF16)
    mixa_ref[...] = (jax.nn.sigmoid(proj(6)) * _dot(a, wa_ref[...])).astype(BF16)

    cos = cos_ref[...]
    sin = sin_ref[...]
    half = RET_DIM // 2

    def rotary_store(t, out_ref, scale):
        for hd in range(RET_HEADS):
            lo = slice(hd * RET_DIM, hd * RET_DIM + half)
            hi = slice(hd * RET_DIM + half, (hd + 1) * RET_DIM)
            t1 = t[:, lo]
            t2 = t[:, hi]
            out_ref[:, lo] = ((t1 * cos - t2 * sin) * scale).astype(BF16)
            out_ref[:, hi] = ((t2 * cos + t1 * sin) * scale).astype(BF16)

    rotary_store(proj(2), q_ref, 1.0)
    rotary_store(proj(3), k_ref, RET_DIM ** -0.5)
    v_ref[...] = proj(4).astype(BF16)
    gr = proj(5)
    sg_ref[...] = (gr * jax.nn.sigmoid(gr)).astype(BF16)
    sgb_ref[...] = jax.nn.sigmoid(proj(7)).astype(BF16)


def _mixer_in(x1, norm_w, w_in, b_in, ln_g, ln_b, w_s, b_s, w_a, cos, sin, seq):
    t = x1.shape[0]
    rows = pl.BlockSpec((MIX_ROWS, D_MODEL), lambda i: (i, 0))
    steps_per_seq = seq // MIX_ROWS
    pos = pl.BlockSpec((MIX_ROWS, RET_DIM // 2), lambda i: (i % steps_per_seq, 0))
    width = N_PROJ_SEGMENTS * D_MODEL
    out = jax.ShapeDtypeStruct((t, D_MODEL), BF16)
    return pl.pallas_call(
        _mixer_in_body,
        grid=(t // MIX_ROWS,),
        in_specs=[rows, _resident((1, D_MODEL)), _resident((D_MODEL, width)), _resident((1, width)),
                  _resident((1, D_MODEL)), _resident((1, D_MODEL)), _resident((SGU_GROUPS, CHUNK, CHUNK)),
                  _resident((SGU_GROUPS, CHUNK, 1)), _resident((D_MODEL, D_MODEL)), pos, pos],
        out_specs=[rows] * 6,
        out_shape=[out] * 6,
        scratch_shapes=[pltpu.VMEM((MIX_ROWS, D_MODEL), F32)],
        compiler_params=pltpu.CompilerParams(dimension_semantics=("arbitrary",), vmem_limit_bytes=48 * MIB),
        name="mixer_in",
    )(x1, norm_w, w_in, b_in, ln_g, ln_b, w_s, b_s, w_a, cos, sin)


def _log_gamma(logit, shape):
    return jax.nn.log_sigmoid(jnp.full(shape, logit, F32))


def _state_update(st_ref, hd, k, v, k_scale, chunk_decay):
    kd = (k.astype(F32) * k_scale).astype(BF16)
    kv = lax.dot_general(kd, v, (((0,), (0,)), ((), ())), preferred_element_type=F32)
    st_ref[hd] = st_ref[hd] * chunk_decay + kv


def _ret_bwd_body(dl_ref, q_ref, k_ref, v_ref, o_ref, st_ref):
    @pl.when(pl.program_id(1) == 0)
    def _():
        st_ref[...] = jnp.zeros_like(st_ref)

    idx = lax.broadcasted_iota(jnp.int32, (CHUNK, RET_DIM), 0).astype(F32)
    for hd in range(RET_HEADS):
        cols = slice(hd * RET_DIM, (hd + 1) * RET_DIM)
        lg = _log_gamma(dl_ref[BACKWARD, hd], (CHUNK, RET_DIM))
        q_scale = jnp.exp((CHUNK - idx) * lg)
        k_scale = jnp.exp(idx * lg)
        chunk_decay = jnp.exp(CHUNK * _log_gamma(dl_ref[BACKWARD, hd], (1, RET_DIM)))
        for c in reversed(range(RET_ROWS // CHUNK)):
            rows = slice(c * CHUNK, (c + 1) * CHUNK)
            o_ref[rows, cols] = (_dot(q_ref[rows, cols], st_ref[hd].astype(BF16)) * q_scale).astype(BF16)
            _state_update(st_ref, hd, k_ref[rows, cols], v_ref[rows, cols], k_scale, chunk_decay)


def _ret_fwd_body(dl_ref, q_ref, k_ref, v_ref, cb_ref, sg_ref, sgb_ref, mixa_ref, x1_ref, wb_ref, wo_ref,
                  o_ref, st_ref, r_ref):
    @pl.when(pl.program_id(1) == 0)
    def _():
        st_ref[...] = jnp.zeros_like(st_ref)

    idx = lax.broadcasted_iota(jnp.int32, (CHUNK, RET_DIM), 0).astype(F32)
    diff = (lax.broadcasted_iota(jnp.int32, (CHUNK, CHUNK), 0)
            - lax.broadcasted_iota(jnp.int32, (CHUNK, CHUNK), 1)).astype(F32)
    for hd in range(RET_HEADS):
        cols = slice(hd * RET_DIM, (hd + 1) * RET_DIM)
        lg = _log_gamma(dl_ref[FORWARD, hd], (CHUNK, RET_DIM))
        q_scale = jnp.exp((idx + 1.0) * lg)
        k_scale = jnp.exp((CHUNK - 1.0 - idx) * lg)
        chunk_decay = jnp.exp(CHUNK * _log_gamma(dl_ref[FORWARD, hd], (1, RET_DIM)))
        lg_f = _log_gamma(dl_ref[FORWARD, hd], (CHUNK, CHUNK))
        lg_b = _log_gamma(dl_ref[BACKWARD, hd], (CHUNK, CHUNK))
        decay = jnp.exp(jnp.where(diff >= 0, diff * lg_f, -diff * lg_b))
        for c in range(RET_ROWS // CHUNK):
            rows = slice(c * CHUNK, (c + 1) * CHUNK)
            q = q_ref[rows, cols]
            k = k_ref[rows, cols]
            v = v_ref[rows, cols]
            scores = lax.dot_general(q, k, (((1,), (1,)), ((), ())), preferred_element_type=F32)
            r = (_dot((scores * decay).astype(BF16), v)
                 + _dot(q, st_ref[hd].astype(BF16)) * q_scale
                 + cb_ref[rows, cols].astype(F32))
            r = r * lax.rsqrt(jnp.mean(r * r, axis=-1, keepdims=True) + NORM_EPS)
            r_ref[rows, cols] = (r * sg_ref[rows, cols].astype(F32)).astype(BF16)
            _state_update(st_ref, hd, k, v, k_scale, chunk_decay)
    mix = sgb_ref[...].astype(F32) * _dot(r_ref[...], wb_ref[...]) + mixa_ref[...].astype(F32)
    o_ref[...] = x1_ref[...] + _dot(mix.astype(BF16), wo_ref[...])


def _ret_specs(batch, seq, reverse):
    steps = seq // RET_ROWS
    if reverse:
        index = lambda b, j: (b, steps - 1 - j, 0)
    else:
        index = lambda b, j: (b, j, 0)
    return (batch, steps), pl.BlockSpec((None, RET_ROWS, D_MODEL), index)


def _ret_bwd(decay_logit, q, k, v):
    batch, seq, _ = q.shape
    grid, rows = _ret_specs(batch, seq, reverse=True)
    return pl.pallas_call(
        _ret_bwd_body,
        grid=grid,
        in_specs=[pl.BlockSpec(memory_space=pltpu.SMEM), rows, rows, rows],
        out_specs=rows,
        out_shape=jax.ShapeDtypeStruct(q.shape, BF16),
        scratch_shapes=[pltpu.VMEM((RET_HEADS, RET_DIM, RET_DIM), F32)],
        compiler_params=pltpu.CompilerParams(dimension_semantics=("arbitrary", "arbitrary"),
                                             vmem_limit_bytes=32 * MIB),
        name="ret_bwd",
    )(decay_logit, q, k, v)


def _ret_fwd(decay_logit, q, k, v, cb, sg, sgb, mixa, x1, w_b, w_o):
    batch, seq, _ = q.shape
    grid, rows = _ret_specs(batch, seq, reverse=False)
    return pl.pallas_call(
        _ret_fwd_body,
        grid=grid,
        in_specs=[pl.BlockSpec(memory_space=pltpu.SMEM)] + [rows] * 8
                 + [_resident((D_MODEL, D_MODEL)), _resident((D_MODEL, D_MODEL))],
        out_specs=rows,
        out_shape=jax.ShapeDtypeStruct(q.shape, F32),
        scratch_shapes=[pltpu.VMEM((RET_HEADS, RET_DIM, RET_DIM), F32), pltpu.VMEM((RET_ROWS, D_MODEL), BF16)],
        compiler_params=pltpu.CompilerParams(dimension_semantics=("arbitrary", "arbitrary"),
                                             vmem_limit_bytes=48 * MIB),
        name="ret_fwd",
    )(decay_logit, q, k, v, cb, sg, sgb, mixa, x1, w_b, w_o)


def kernel(x, ffn1_norm, ffn1_w_gate, ffn1_w_up, ffn1_w_down, mix_norm, w_in, b_in, sgu_norm_g, sgu_norm_b, sgu_w_s, sgu_b_s, ret_decay_logit, w_branch_a, w_branch_b, w_out, ffn2_norm, ffn2_w_gate, ffn2_w_up, ffn2_w_down, final_norm):
    batch, seq, _ = x.shape
    depth = ffn1_norm.shape[0]
    assert depth >= 1 and seq % RET_ROWS == 0 and (batch * seq) % FFN_ROWS == 0
    tokens = batch * seq
    bf = lambda w: w.astype(BF16)
    row = lambda p: p.reshape(1, -1)

    theta = ROPE_BASE ** (-jnp.arange(0, RET_DIM, 2, dtype=F32) / RET_DIM)
    ang = jnp.arange(seq, dtype=F32)[:, None] * theta[None, :]
    cos, sin = jnp.cos(ang), jnp.sin(ang)

    xt = x.reshape(tokens, D_MODEL)
    for l in range(depth):
        last = l == depth - 1
        x1 = _ffn(xt, row(ffn1_norm[l]), bf(ffn1_w_gate[l]), bf(ffn1_w_up[l]), bf(ffn1_w_down[l]))
        mixa, q, k, v, sg, sgb = _mixer_in(
            x1, row(mix_norm[l]), bf(w_in[l]), row(b_in[l]), row(sgu_norm_g[l]), row(sgu_norm_b[l]),
            bf(sgu_w_s[l]), sgu_b_s[l].reshape(SGU_GROUPS, CHUNK, 1), bf(w_branch_a[l]), cos, sin, seq)
        seq3 = lambda a: a.reshape(batch, seq, D_MODEL)
        q, k, v = seq3(q), seq3(k), seq3(v)
        cb = _ret_bwd(ret_decay_logit[l], q, k, v)
        x2 = _ret_fwd(ret_decay_logit[l], q, k, v, cb, seq3(sg), seq3(sgb), seq3(mixa), seq3(x1),
                      bf(w_branch_b[l]), bf(w_out[l]))
        xt = _ffn(x2.reshape(tokens, D_MODEL), row(ffn2_norm[l]), bf(ffn2_w_gate[l]), bf(ffn2_w_up[l]),
                  bf(ffn2_w_down[l]), row(final_norm) if last else None)
    return xt.reshape(batch, seq, D_MODEL)
```

```python
import functools

import jax
import jax.numpy as jnp
import numpy as np
from jax import lax
from jax.experimental import pallas as pl
from jax.experimental.pallas import tpu as pltpu

D_MODEL = 1024
D_FF = 2816
CHUNK = 128
SGU_GROUPS = 4
SGU_GROUP_DIM = D_MODEL // SGU_GROUPS
RET_HEADS = 4
RET_DIM = 256
SEG_U, SEG_V, SEG_Q, SEG_K, SEG_VR, SEG_G, SEG_GATE_A, SEG_GATE_B = range(8)
FORWARD, BACKWARD = 0, 1
ROPE_BASE = 10000.0
NORM_EPS = 1e-6

MXU_COLS = 256
FFN_ROWS = 512
MIX_ROWS = 512
RET_ROWS = 512
RET_BLOCK = 256
MIB = 1024 * 1024

BF16 = jnp.bfloat16
F32 = jnp.float32


def _rms(x, g):
    return x * lax.rsqrt(jnp.mean(x * x, axis=-1, keepdims=True) + NORM_EPS) * g


def _gelu(x):
    return 0.5 * x * (1.0 + lax.erf(x * np.float32(np.sqrt(0.5))))


def _silu(x):
    return x * jax.nn.sigmoid(x)


def _dot(a, b):
    return jnp.dot(a, b, preferred_element_type=F32)


def _dot_nt(a, b):
    return lax.dot_general(a, b, (((1,), (1,)), ((), ())), preferred_element_type=F32)


def _col_chunks(width=D_MODEL):
    return [slice(c, c + MXU_COLS) for c in range(0, width, MXU_COLS)]


def _resident(shape):
    return pl.BlockSpec(shape, lambda *_: (0,) * len(shape), pipeline_mode=pl.Buffered(1))


def _ffn_body(*refs, final_norm):
    if final_norm:
        x_ref, nw_ref, wg_ref, wu_ref, wd_ref, fn_ref, o_ref, act_ref = refs
    else:
        x_ref, nw_ref, wg_ref, wu_ref, wd_ref, o_ref, act_ref = refs
    x = x_ref[...]
    h = _rms(x, nw_ref[...]).astype(BF16)
    for cols in _col_chunks(D_FF):
        act_ref[:, cols] = (_silu(_dot(h, wg_ref[:, cols])) * _dot(h, wu_ref[:, cols])).astype(BF16)
    y = x + 0.5 * _dot(act_ref[...], wd_ref[...])
    if final_norm:
        y = _rms(y, fn_ref[...])
    o_ref[...] = y


def _ffn(x, norm_w, w_gate, w_up, w_down, final_w=None):
    t = x.shape[0]
    rows = pl.BlockSpec((FFN_ROWS, D_MODEL), lambda i: (i, 0))
    in_specs = [rows, _resident((1, D_MODEL)), _resident((D_MODEL, D_FF)), _resident((D_MODEL, D_FF)),
                _resident((D_FF, D_MODEL))]
    args = [x, norm_w, w_gate, w_up, w_down]
    if final_w is not None:
        in_specs.append(_resident((1, D_MODEL)))
        args.append(final_w)
    return pl.pallas_call(
        functools.partial(_ffn_body, final_norm=final_w is not None),
        grid=(t // FFN_ROWS,),
        in_specs=in_specs,
        out_specs=rows,
        out_shape=jax.ShapeDtypeStruct((t, D_MODEL), F32),
        scratch_shapes=[pltpu.VMEM((FFN_ROWS, D_FF), BF16)],
        compiler_params=pltpu.CompilerParams(dimension_semantics=("arbitrary",), vmem_limit_bytes=48 * MIB),
        name="ffn_final" if final_w is not None else "ffn",
    )(*args)


def _mixer_in_body(x_ref, nw_ref, w_ref, b_ref, wkt_ref, bkt_ref, lng_ref, lnb_ref, ws_ref, bs_ref, wa_ref,
                   cos_ref, sin_ref, cost_ref, sint_ref,
                   mixa_ref, q_ref, kt_ref, v_ref, sg_ref, sgb_ref, va_ref, vn_ref, a_ref):
    h = _rms(x_ref[...], nw_ref[...]).astype(BF16)
    half = RET_DIM // 2

    def proj(seg, cols):
        wcols = slice(seg * D_MODEL + cols.start, seg * D_MODEL + cols.stop)
        return _dot(h, w_ref[:, wcols]) + b_ref[:, wcols]

    for cols in _col_chunks():
        va_ref[:, cols] = _gelu(proj(SEG_V, cols))
    va = va_ref[...]
    mu = jnp.mean(va, axis=-1, keepdims=True)
    vc = va - mu
    var = jnp.mean(vc * vc, axis=-1, keepdims=True)
    vn_ref[...] = (vc * lax.rsqrt(var + NORM_EPS) * lng_ref[...] + lnb_ref[...]).astype(BF16)
    for g, cols in enumerate(_col_chunks()):
        u = _gelu(proj(SEG_U, cols))
        for j in range(MIX_ROWS // CHUNK):
            rows = slice(j * CHUNK, (j + 1) * CHUNK)
            s = _dot(ws_ref[g], vn_ref[rows, cols]) + bs_ref[g]
            a_ref[rows, cols] = (u[rows] * s).astype(BF16)
    a = a_ref[...]
    for cols in _col_chunks():
        mixa_ref[:, cols] = (jax.nn.sigmoid(proj(SEG_GATE_A, cols)) * _dot(a, wa_ref[:, cols])).astype(BF16)

    cos, sin = cos_ref[...], sin_ref[...]
    cost, sint = cost_ref[...], sint_ref[...]
    k_scale = RET_DIM ** -0.5
    for cols in _col_chunks():
        t = proj(SEG_Q, cols)
        t1, t2 = t[:, :half], t[:, half:]
        q_ref[:, cols.start:cols.start + half] = (t1 * cos - t2 * sin).astype(BF16)
        q_ref[:, cols.start + half:cols.stop] = (t2 * cos + t1 * sin).astype(BF16)
        tt = _dot_nt(wkt_ref[cols, :], h) + bkt_ref[cols, :]
        t1, t2 = tt[:half], tt[half:]
        kt_ref[cols.start:cols.start + half, :] = ((t1 * cost - t2 * sint) * k_scale).astype(BF16)
        kt_ref[cols.start + half:cols.stop, :] = ((t2 * cost + t1 * sint) * k_scale).astype(BF16)
        v_ref[:, cols] = proj(SEG_VR, cols).astype(BF16)
        sg_ref[:, cols] = _silu(proj(SEG_G, cols)).astype(BF16)
        sgb_ref[:, cols] = jax.nn.sigmoid(proj(SEG_GATE_B, cols)).astype(BF16)


def _mixer_in(x1, norm_w, w_in, b_in, w_kt, b_kt, ln_g, ln_b, w_s, b_s, w_a, cos, sin, cos_t, sin_t, seq):
    t = x1.shape[0]
    half = RET_DIM // 2
    steps_per_seq = seq // MIX_ROWS
    rows = pl.BlockSpec((MIX_ROWS, D_MODEL), lambda i: (i, 0))
    lanes = pl.BlockSpec((D_MODEL, MIX_ROWS), lambda i: (0, i))
    pos = pl.BlockSpec((MIX_ROWS, half), lambda i: (i % steps_per_seq, 0))
    pos_t = pl.BlockSpec((half, MIX_ROWS), lambda i: (0, i % steps_per_seq))
    width = w_in.shape[1]
    out = jax.ShapeDtypeStruct((t, D_MODEL), BF16)
    out_t = jax.ShapeDtypeStruct((D_MODEL, t), BF16)
    return pl.pallas_call(
        _mixer_in_body,
        grid=(t // MIX_ROWS,),
        in_specs=[rows, _resident((1, D_MODEL)), _resident((D_MODEL, width)), _resident((1, width)),
                  _resident((D_MODEL, D_MODEL)), _resident((D_MODEL, 1)),
                  _resident((1, D_MODEL)), _resident((1, D_MODEL)), _resident((SGU_GROUPS, CHUNK, CHUNK)),
                  _resident((SGU_GROUPS, CHUNK, 1)), _resident((D_MODEL, D_MODEL)), pos, pos, pos_t, pos_t],
        out_specs=[rows, rows, lanes, rows, rows, rows],
        out_shape=[out, out, out_t, out, out, out],
        scratch_shapes=[pltpu.VMEM((MIX_ROWS, D_MODEL), F32), pltpu.VMEM((MIX_ROWS, D_MODEL), BF16),
                        pltpu.VMEM((MIX_ROWS, D_MODEL), BF16)],
        compiler_params=pltpu.CompilerParams(dimension_semantics=("arbitrary",), vmem_limit_bytes=56 * MIB),
        name="mixer_in",
    )(x1, norm_w, w_in, b_in, w_kt, b_kt, ln_g, ln_b, w_s, b_s, w_a, cos, sin, cos_t, sin_t)


def _log_gamma(logit, shape):
    return jax.nn.log_sigmoid(jnp.full(shape, logit, F32))


def _zero_state_at_sequence_start(st_ref):
    @pl.when(pl.program_id(1) == 0)
    def _():
        st_ref[...] = jnp.zeros_like(st_ref)


def _state_update(st_ref, hd, kt, v, k_scale, block_decay):
    kd = (kt.astype(F32) * k_scale).astype(BF16)
    st_ref[hd] = st_ref[hd] * block_decay + _dot(kd, v)


def _ret_bwd_body(dl_ref, q_ref, kt_ref, v_ref, o_ref, st_ref):
    _zero_state_at_sequence_start(st_ref)
    row_idx = lax.broadcasted_iota(jnp.int32, (RET_BLOCK, RET_DIM), 0).astype(F32)
    lane_idx = lax.broadcasted_iota(jnp.int32, (RET_DIM, RET_BLOCK), 1).astype(F32)
    for hd in range(RET_HEADS):
        cols = slice(hd * RET_DIM, (hd + 1) * RET_DIM)
        logit = dl_ref[BACKWARD, hd]
        q_scale = jnp.exp((RET_BLOCK - row_idx) * _log_gamma(logit, (RET_BLOCK, RET_DIM)))
        k_scale = jnp.exp(lane_idx * _log_gamma(logit, (RET_DIM, RET_BLOCK)))
        block_decay = jnp.exp(RET_BLOCK * _log_gamma(logit, (1, RET_DIM)))
        for blk in reversed(range(RET_ROWS // RET_BLOCK)):
            rows = slice(blk * RET_BLOCK, (blk + 1) * RET_BLOCK)
            o_ref[rows, cols] = (_dot(q_ref[rows, cols], st_ref[hd].astype(BF16)) * q_scale).astype(BF16)
            _state_update(st_ref, hd, kt_ref[cols, rows], v_ref[rows, cols], k_scale, block_decay)


def _ret_fwd_body(dl_ref, q_ref, kt_ref, v_ref, cb_ref, sg_ref, sgb_ref, mixa_ref, x1_ref, wb_ref, wo_ref,
                  o_ref, st_ref, r_ref, mix_ref):
    _zero_state_at_sequence_start(st_ref)
    row_idx = lax.broadcasted_iota(jnp.int32, (RET_BLOCK, RET_DIM), 0).astype(F32)
    lane_idx = lax.broadcasted_iota(jnp.int32, (RET_DIM, RET_BLOCK), 1).astype(F32)
    diff = (lax.broadcasted_iota(jnp.int32, (RET_BLOCK, RET_BLOCK), 0)
            - lax.broadcasted_iota(jnp.int32, (RET_BLOCK, RET_BLOCK), 1)).astype(F32)
    for hd in range(RET_HEADS):
        cols = slice(hd * RET_DIM, (hd + 1) * RET_DIM)
        logit = dl_ref[FORWARD, hd]
        q_scale = jnp.exp((row_idx + 1.0) * _log_gamma(logit, (RET_BLOCK, RET_DIM)))
        k_scale = jnp.exp((RET_BLOCK - 1.0 - lane_idx) * _log_gamma(logit, (RET_DIM, RET_BLOCK)))
        block_decay = jnp.exp(RET_BLOCK * _log_gamma(logit, (1, RET_DIM)))
        lg_f = _log_gamma(logit, (RET_BLOCK, RET_BLOCK))
        lg_b = _log_gamma(dl_ref[BACKWARD, hd], (RET_BLOCK, RET_BLOCK))
        decay = jnp.exp(jnp.where(diff >= 0, diff * lg_f, -diff * lg_b))
        for blk in range(RET_ROWS // RET_BLOCK):
            rows = slice(blk * RET_BLOCK, (blk + 1) * RET_BLOCK)
            q = q_ref[rows, cols]
            kt = kt_ref[cols, rows]
            v = v_ref[rows, cols]
            r = (_dot((_dot(q, kt) * decay).astype(BF16), v)
                 + _dot(q, st_ref[hd].astype(BF16)) * q_scale
                 + cb_ref[rows, cols].astype(F32))
            r = r * lax.rsqrt(jnp.mean(r * r, axis=-1, keepdims=True) + NORM_EPS)
            r_ref[rows, cols] = (r * sg_ref[rows, cols].astype(F32)).astype(BF16)
            _state_update(st_ref, hd, kt, v, k_scale, block_decay)
    r = r_ref[...]
    for cols in _col_chunks():
        mix_ref[:, cols] = (sgb_ref[:, cols].astype(F32) * _dot(r, wb_ref[:, cols])
                            + mixa_ref[:, cols].astype(F32)).astype(BF16)
    mix = mix_ref[...]
    for cols in _col_chunks():
        o_ref[:, cols] = x1_ref[:, cols] + _dot(mix, wo_ref[:, cols])


def _ret_specs(batch, seq, reverse):
    steps = seq // RET_ROWS
    if reverse:
        block = lambda b, j: b * steps + steps - 1 - j
    else:
        block = lambda b, j: b * steps + j
    rows = pl.BlockSpec((RET_ROWS, D_MODEL), lambda b, j: (block(b, j), 0))
    lanes = pl.BlockSpec((D_MODEL, RET_ROWS), lambda b, j: (0, block(b, j)))
    return (batch, steps), rows, lanes


def _ret_bwd(decay_logit, q, kt, v, batch, seq):
    grid, rows, lanes = _ret_specs(batch, seq, reverse=True)
    return pl.pallas_call(
        _ret_bwd_body,
        grid=grid,
        in_specs=[pl.BlockSpec(memory_space=pltpu.SMEM), rows, lanes, rows],
        out_specs=rows,
        out_shape=jax.ShapeDtypeStruct(q.shape, BF16),
        scratch_shapes=[pltpu.VMEM((RET_HEADS, RET_DIM, RET_DIM), F32)],
        compiler_params=pltpu.CompilerParams(dimension_semantics=("arbitrary", "arbitrary"),
                                             vmem_limit_bytes=32 * MIB),
        name="ret_bwd",
    )(decay_logit, q, kt, v)


def _ret_fwd(decay_logit, q, kt, v, cb, sg, sgb, mixa, x1, w_b, w_o, batch, seq):
    grid, rows, lanes = _ret_specs(batch, seq, reverse=False)
    return pl.pallas_call(
        _ret_fwd_body,
        grid=grid,
        in_specs=[pl.BlockSpec(memory_space=pltpu.SMEM), rows, lanes] + [rows] * 6
                 + [_resident((D_MODEL, D_MODEL)), _resident((D_MODEL, D_MODEL))],
        out_specs=rows,
        out_shape=jax.ShapeDtypeStruct(q.shape, F32),
        scratch_shapes=[pltpu.VMEM((RET_HEADS, RET_DIM, RET_DIM), F32), pltpu.VMEM((RET_ROWS, D_MODEL), BF16),
                        pltpu.VMEM((RET_ROWS, D_MODEL), BF16)],
        compiler_params=pltpu.CompilerParams(dimension_semantics=("arbitrary", "arbitrary"),
                                             vmem_limit_bytes=48 * MIB),
        name="ret_fwd",
    )(decay_logit, q, kt, v, cb, sg, sgb, mixa, x1, w_b, w_o)


def kernel(x, ffn1_norm, ffn1_w_gate, ffn1_w_up, ffn1_w_down, mix_norm, w_in, b_in, sgu_norm_g, sgu_norm_b, sgu_w_s, sgu_b_s, ret_decay_logit, w_branch_a, w_branch_b, w_out, ffn2_norm, ffn2_w_gate, ffn2_w_up, ffn2_w_down, final_norm):
    batch, seq, _ = x.shape
    depth = ffn1_norm.shape[0]
    assert depth >= 1 and seq % RET_ROWS == 0 and seq % MIX_ROWS == 0 and (batch * seq) % FFN_ROWS == 0
    tokens = batch * seq
    bf = lambda w: w.astype(BF16)
    row = lambda p: p.reshape(1, -1)

    theta = ROPE_BASE ** (-jnp.arange(0, RET_DIM, 2, dtype=F32) / RET_DIM)
    ang = jnp.arange(seq, dtype=F32)[:, None] * theta[None, :]
    cos, sin = jnp.cos(ang), jnp.sin(ang)

    k_cols = slice(SEG_K * D_MODEL, (SEG_K + 1) * D_MODEL)
    xt = x.reshape(tokens, D_MODEL)
    for l in range(depth):
        last = l == depth - 1
        x1 = _ffn(xt, row(ffn1_norm[l]), bf(ffn1_w_gate[l]), bf(ffn1_w_up[l]), bf(ffn1_w_down[l]))
        mixa, q, kt, v, sg, sgb = _mixer_in(
            x1, row(mix_norm[l]), bf(w_in[l]), row(b_in[l]), bf(w_in[l][:, k_cols].T), b_in[l][k_cols].reshape(-1, 1),
            row(sgu_norm_g[l]), row(sgu_norm_b[l]), bf(sgu_w_s[l]), sgu_b_s[l].reshape(SGU_GROUPS, CHUNK, 1),
            bf(w_branch_a[l]), cos, sin, cos.T, sin.T, seq)
        cb = _ret_bwd(ret_decay_logit[l], q, kt, v, batch, seq)
        x2 = _ret_fwd(ret_decay_logit[l], q, kt, v, cb, sg, sgb, mixa, x1, bf(w_branch_b[l]), bf(w_out[l]),
                      batch, seq)
        xt = _ffn(x2, row(ffn2_norm[l]), bf(ffn2_w_gate[l]), bf(ffn2_w_up[l]), bf(ffn2_w_down[l]),
                  row(final_norm) if last else None)
    return xt.reshape(batch, seq, D_MODEL)
```

```python
import functools

import jax
import jax.numpy as jnp
import numpy as np
from jax import lax
from jax.experimental import pallas as pl
from jax.experimental.pallas import tpu as pltpu

D_MODEL = 1024
D_FF = 2816
CHUNK = 128
SGU_GROUPS = 4
SGU_GROUP_DIM = D_MODEL // SGU_GROUPS
RET_HEADS = 4
RET_DIM = 256
SEG_U, SEG_V, SEG_Q, SEG_K, SEG_VR, SEG_G, SEG_GATE_A, SEG_GATE_B = range(8)
FORWARD, BACKWARD = 0, 1
ROPE_BASE = 10000.0
NORM_EPS = 1e-6

MXU_COLS = 256
FFN_ROWS = 512
MIX_ROWS = 512
RET_ROWS = 512
RET_BLOCK = 256
MIB = 1024 * 1024

BF16 = jnp.bfloat16
F32 = jnp.float32


def _rms(x, g):
    return x * lax.rsqrt(jnp.mean(x * x, axis=-1, keepdims=True) + NORM_EPS) * g


def _gelu(x):
    return 0.5 * x * (1.0 + lax.erf(x * np.float32(np.sqrt(0.5))))


def _silu(x):
    return x * jax.nn.sigmoid(x)


def _dot(a, b):
    return jnp.dot(a, b, preferred_element_type=F32)


def _dot_nt(a, b):
    return lax.dot_general(a, b, (((1,), (1,)), ((), ())), preferred_element_type=F32)


def _col_chunks(width=D_MODEL):
    return [slice(c, c + MXU_COLS) for c in range(0, width, MXU_COLS)]


def _resident(shape):
    return pl.BlockSpec(shape, lambda *_: (0,) * len(shape), pipeline_mode=pl.Buffered(1))


def _ffn_body(*refs, final_norm):
    if final_norm:
        x_ref, nw_ref, wg_ref, wu_ref, wd_ref, fn_ref, o_ref, act_ref = refs
    else:
        x_ref, nw_ref, wg_ref, wu_ref, wd_ref, o_ref, act_ref = refs
    x = x_ref[...]
    h = _rms(x, nw_ref[...]).astype(BF16)
    for cols in _col_chunks(D_FF):
        act_ref[:, cols] = (_silu(_dot(h, wg_ref[:, cols])) * _dot(h, wu_ref[:, cols])).astype(BF16)
    y = x + 0.5 * _dot(act_ref[...], wd_ref[...])
    if final_norm:
        y = _rms(y, fn_ref[...])
    o_ref[...] = y


def _ffn(x, norm_w, w_gate, w_up, w_down, final_w=None):
    t = x.shape[0]
    rows = pl.BlockSpec((FFN_ROWS, D_MODEL), lambda i: (i, 0))
    in_specs = [rows, _resident((1, D_MODEL)), _resident((D_MODEL, D_FF)), _resident((D_MODEL, D_FF)),
                _resident((D_FF, D_MODEL))]
    args = [x, norm_w, w_gate, w_up, w_down]
    if final_w is not None:
        in_specs.append(_resident((1, D_MODEL)))
        args.append(final_w)
    return pl.pallas_call(
        functools.partial(_ffn_body, final_norm=final_w is not None),
        grid=(t // FFN_ROWS,),
        in_specs=in_specs,
        out_specs=rows,
        out_shape=jax.ShapeDtypeStruct((t, D_MODEL), F32),
        scratch_shapes=[pltpu.VMEM((FFN_ROWS, D_FF), BF16)],
        compiler_params=pltpu.CompilerParams(dimension_semantics=("arbitrary",), vmem_limit_bytes=48 * MIB),
        name="ffn_final" if final_w is not None else "ffn",
    )(*args)


def _mixer_in_body(x_ref, nw_ref, w_ref, b_ref, wk_ref, bkt_ref, lng_ref, lnb_ref, ws_ref, bs_ref, wa_ref,
                   cos_ref, sin_ref, cost_ref, sint_ref,
                   mixa_ref, q_ref, kt_ref, v_ref, sg_ref, sgb_ref, wkt_ref, va_ref, vn_ref, a_ref):
    @pl.when(pl.program_id(0) == 0)
    def _():
        for r in _col_chunks():
            for c in _col_chunks():
                wkt_ref[c, r] = wk_ref[r, c].T.astype(BF16)

    h = _rms(x_ref[...], nw_ref[...]).astype(BF16)
    half = RET_DIM // 2

    def proj(seg, cols):
        wcols = slice(seg * D_MODEL + cols.start, seg * D_MODEL + cols.stop)
        return _dot(h, w_ref[:, wcols]) + b_ref[:, wcols]

    for cols in _col_chunks():
        va_ref[:, cols] = _gelu(proj(SEG_V, cols))
    va = va_ref[...]
    mu = jnp.mean(va, axis=-1, keepdims=True)
    vc = va - mu
    var = jnp.mean(vc * vc, axis=-1, keepdims=True)
    vn_ref[...] = (vc * lax.rsqrt(var + NORM_EPS) * lng_ref[...] + lnb_ref[...]).astype(BF16)
    for g, cols in enumerate(_col_chunks()):
        u = _gelu(proj(SEG_U, cols))
        for j in range(MIX_ROWS // CHUNK):
            rows = slice(j * CHUNK, (j + 1) * CHUNK)
            s = _dot(ws_ref[g], vn_ref[rows, cols]) + bs_ref[g]
            a_ref[rows, cols] = (u[rows] * s).astype(BF16)
    a = a_ref[...]
    for cols in _col_chunks():
        mixa_ref[:, cols] = (jax.nn.sigmoid(proj(SEG_GATE_A, cols)) * _dot(a, wa_ref[:, cols])).astype(BF16)

    cos, sin = cos_ref[...], sin_ref[...]
    cost, sint = cost_ref[...], sint_ref[...]
    k_scale = RET_DIM ** -0.5
    for cols in _col_chunks():
        t = proj(SEG_Q, cols)
        t1, t2 = t[:, :half], t[:, half:]
        q_ref[:, cols.start:cols.start + half] = (t1 * cos - t2 * sin).astype(BF16)
        q_ref[:, cols.start + half:cols.stop] = (t2 * cos + t1 * sin).astype(BF16)
        tt = _dot_nt(wkt_ref[cols, :], h) + bkt_ref[cols, :]
        t1, t2 = tt[:half], tt[half:]
        kt_ref[cols.start:cols.start + half, :] = ((t1 * cost - t2 * sint) * k_scale).astype(BF16)
        kt_ref[cols.start + half:cols.stop, :] = ((t2 * cost + t1 * sint) * k_scale).astype(BF16)
        v_ref[:, cols] = proj(SEG_VR, cols).astype(BF16)
        sg_ref[:, cols] = _silu(proj(SEG_G, cols)).astype(BF16)
        sgb_ref[:, cols] = jax.nn.sigmoid(proj(SEG_GATE_B, cols)).astype(BF16)


def _mixer_in(x1, norm_w, w_in, b_in, w_k, b_kt, ln_g, ln_b, w_s, b_s, w_a, cos, sin, cos_t, sin_t, seq):
    t = x1.shape[0]
    half = RET_DIM // 2
    steps_per_seq = seq // MIX_ROWS
    rows = pl.BlockSpec((MIX_ROWS, D_MODEL), lambda i: (i, 0))
    lanes = pl.BlockSpec((D_MODEL, MIX_ROWS), lambda i: (0, i))
    pos = pl.BlockSpec((MIX_ROWS, half), lambda i: (i % steps_per_seq, 0))
    pos_t = pl.BlockSpec((half, MIX_ROWS), lambda i: (0, i % steps_per_seq))
    width = w_in.shape[1]
    out = jax.ShapeDtypeStruct((t, D_MODEL), BF16)
    out_t = jax.ShapeDtypeStruct((D_MODEL, t), BF16)
    return pl.pallas_call(
        _mixer_in_body,
        grid=(t // MIX_ROWS,),
        in_specs=[rows, _resident((1, D_MODEL)), _resident((D_MODEL, width)), _resident((1, width)),
                  _resident((D_MODEL, D_MODEL)), _resident((D_MODEL, 1)),
                  _resident((1, D_MODEL)), _resident((1, D_MODEL)), _resident((SGU_GROUPS, CHUNK, CHUNK)),
                  _resident((SGU_GROUPS, CHUNK, 1)), _resident((D_MODEL, D_MODEL)), pos, pos, pos_t, pos_t],
        out_specs=[rows, rows, lanes, rows, rows, rows],
        out_shape=[out, out, out_t, out, out, out],
        scratch_shapes=[pltpu.VMEM((D_MODEL, D_MODEL), BF16), pltpu.VMEM((MIX_ROWS, D_MODEL), F32),
                        pltpu.VMEM((MIX_ROWS, D_MODEL), BF16), pltpu.VMEM((MIX_ROWS, D_MODEL), BF16)],
        compiler_params=pltpu.CompilerParams(dimension_semantics=("arbitrary",), vmem_limit_bytes=56 * MIB),
        name="mixer_in",
    )(x1, norm_w, w_in, b_in, w_k, b_kt, ln_g, ln_b, w_s, b_s, w_a, cos, sin, cos_t, sin_t)


def _log_gamma(logit, shape):
    return jax.nn.log_sigmoid(jnp.full(shape, logit, F32))


def _zero_state_at_sequence_start(st_ref):
    @pl.when(pl.program_id(1) == 0)
    def _():
        st_ref[...] = jnp.zeros_like(st_ref)


def _state_update(st_ref, hd, kt, v, k_scale, block_decay):
    kd = (kt.astype(F32) * k_scale).astype(BF16)
    st_ref[hd] = st_ref[hd] * block_decay + _dot(kd, v)


def _ret_bwd_body(dl_ref, q_ref, kt_ref, v_ref, o_ref, st_ref):
    _zero_state_at_sequence_start(st_ref)
    row_idx = lax.broadcasted_iota(jnp.int32, (RET_BLOCK, RET_DIM), 0).astype(F32)
    lane_idx = lax.broadcasted_iota(jnp.int32, (RET_DIM, RET_BLOCK), 1).astype(F32)
    for hd in range(RET_HEADS):
        cols = slice(hd * RET_DIM, (hd + 1) * RET_DIM)
        logit = dl_ref[BACKWARD, hd]
        q_scale = jnp.exp((RET_BLOCK - row_idx) * _log_gamma(logit, (RET_BLOCK, RET_DIM)))
        k_scale = jnp.exp(lane_idx * _log_gamma(logit, (RET_DIM, RET_BLOCK)))
        block_decay = jnp.exp(RET_BLOCK * _log_gamma(logit, (1, RET_DIM)))
        for blk in reversed(range(RET_ROWS // RET_BLOCK)):
            rows = slice(blk * RET_BLOCK, (blk + 1) * RET_BLOCK)
            o_ref[rows, cols] = (_dot(q_ref[rows, cols], st_ref[hd].astype(BF16)) * q_scale).astype(BF16)
            _state_update(st_ref, hd, kt_ref[cols, rows], v_ref[rows, cols], k_scale, block_decay)


def _ret_fwd_body(dl_ref, q_ref, kt_ref, v_ref, cb_ref, sg_ref, sgb_ref, mixa_ref, x1_ref, wb_ref, wo_ref,
                  o_ref, st_ref, r_ref, mix_ref):
    _zero_state_at_sequence_start(st_ref)
    row_idx = lax.broadcasted_iota(jnp.int32, (RET_BLOCK, RET_DIM), 0).astype(F32)
    lane_idx = lax.broadcasted_iota(jnp.int32, (RET_DIM, RET_BLOCK), 1).astype(F32)
    diff = (lax.broadcasted_iota(jnp.int32, (RET_BLOCK, RET_BLOCK), 0)
            - lax.broadcasted_iota(jnp.int32, (RET_BLOCK, RET_BLOCK), 1)).astype(F32)
    for hd in range(RET_HEADS):
        cols = slice(hd * RET_DIM, (hd + 1) * RET_DIM)
        logit = dl_ref[FORWARD, hd]
        q_scale = jnp.exp((row_idx + 1.0) * _log_gamma(logit, (RET_BLOCK, RET_DIM)))
        k_scale = jnp.exp((RET_BLOCK - 1.0 - lane_idx) * _log_gamma(logit, (RET_DIM, RET_BLOCK)))
        block_decay = jnp.exp(RET_BLOCK * _log_gamma(logit, (1, RET_DIM)))
        lg_f = _log_gamma(logit, (RET_BLOCK, RET_BLOCK))
        lg_b = _log_gamma(dl_ref[BACKWARD, hd], (RET_BLOCK, RET_BLOCK))
        decay = jnp.exp(jnp.where(diff >= 0, diff * lg_f, -diff * lg_b))
        for blk in range(RET_ROWS // RET_BLOCK):
            rows = slice(blk * RET_BLOCK, (blk + 1) * RET_BLOCK)
            q = q_ref[rows, cols]
            kt = kt_ref[cols, rows]
            v = v_ref[rows, cols]
            r = (_dot((_dot(q, kt) * decay).astype(BF16), v)
                 + _dot(q, st_ref[hd].astype(BF16)) * q_scale
                 + cb_ref[rows, cols].astype(F32))
            r = r * lax.rsqrt(jnp.mean(r * r, axis=-1, keepdims=True) + NORM_EPS)
            r_ref[rows, cols] = (r * sg_ref[rows, cols].astype(F32)).astype(BF16)
            _state_update(st_ref, hd, kt, v, k_scale, block_decay)
    r = r_ref[...]
    for cols in _col_chunks():
        mix_ref[:, cols] = (sgb_ref[:, cols].astype(F32) * _dot(r, wb_ref[:, cols])
                            + mixa_ref[:, cols].astype(F32)).astype(BF16)
    mix = mix_ref[...]
    for cols in _col_chunks():
        o_ref[:, cols] = x1_ref[:, cols] + _dot(mix, wo_ref[:, cols])


def _ret_specs(batch, seq, reverse):
    steps = seq // RET_ROWS
    if reverse:
        block = lambda b, j: b * steps + steps - 1 - j
    else:
        block = lambda b, j: b * steps + j
    rows = pl.BlockSpec((RET_ROWS, D_MODEL), lambda b, j: (block(b, j), 0))
    lanes = pl.BlockSpec((D_MODEL, RET_ROWS), lambda b, j: (0, block(b, j)))
    return (batch, steps), rows, lanes


def _ret_bwd(decay_logit, q, kt, v, batch, seq):
    grid, rows, lanes = _ret_specs(batch, seq, reverse=True)
    return pl.pallas_call(
        _ret_bwd_body,
        grid=grid,
        in_specs=[pl.BlockSpec(memory_space=pltpu.SMEM), rows, lanes, rows],
        out_specs=rows,
        out_shape=jax.ShapeDtypeStruct(q.shape, BF16),
        scratch_shapes=[pltpu.VMEM((RET_HEADS, RET_DIM, RET_DIM), F32)],
        compiler_params=pltpu.CompilerParams(dimension_semantics=("arbitrary", "arbitrary"),
                                             vmem_limit_bytes=32 * MIB),
        name="ret_bwd",
    )(decay_logit, q, kt, v)


def _ret_fwd(decay_logit, q, kt, v, cb, sg, sgb, mixa, x1, w_b, w_o, batch, seq):
    grid, rows, lanes = _ret_specs(batch, seq, reverse=False)
    return pl.pallas_call(
        _ret_fwd_body,
        grid=grid,
        in_specs=[pl.BlockSpec(memory_space=pltpu.SMEM), rows, lanes] + [rows] * 6
                 + [_resident((D_MODEL, D_MODEL)), _resident((D_MODEL, D_MODEL))],
        out_specs=rows,
        out_shape=jax.ShapeDtypeStruct(q.shape, F32),
        scratch_shapes=[pltpu.VMEM((RET_HEADS, RET_DIM, RET_DIM), F32), pltpu.VMEM((RET_ROWS, D_MODEL), BF16),
                        pltpu.VMEM((RET_ROWS, D_MODEL), BF16)],
        compiler_params=pltpu.CompilerParams(dimension_semantics=("arbitrary", "arbitrary"),
                                             vmem_limit_bytes=48 * MIB),
        name="ret_fwd",
    )(decay_logit, q, kt, v, cb, sg, sgb, mixa, x1, w_b, w_o)


def kernel(x, ffn1_norm, ffn1_w_gate, ffn1_w_up, ffn1_w_down, mix_norm, w_in, b_in, sgu_norm_g, sgu_norm_b, sgu_w_s, sgu_b_s, ret_decay_logit, w_branch_a, w_branch_b, w_out, ffn2_norm, ffn2_w_gate, ffn2_w_up, ffn2_w_down, final_norm):
    batch, seq, _ = x.shape
    depth = ffn1_norm.shape[0]
    assert depth >= 1 and seq % RET_ROWS == 0 and seq % MIX_ROWS == 0 and (batch * seq) % FFN_ROWS == 0
    tokens = batch * seq
    bf = lambda w: w.astype(BF16)
    row = lambda p: p.reshape(1, -1)

    theta = ROPE_BASE ** (-jnp.arange(0, RET_DIM, 2, dtype=F32) / RET_DIM)
    ang = jnp.arange(seq, dtype=F32)[:, None] * theta[None, :]
    cos, sin = jnp.cos(ang), jnp.sin(ang)

    k_lo, k_hi = SEG_K * D_MODEL, (SEG_K + 1) * D_MODEL
    xt = x.reshape(tokens, D_MODEL)
    for l in range(depth):
        last = l == depth - 1
        x1 = _ffn(xt, row(ffn1_norm[l]), bf(ffn1_w_gate[l]), bf(ffn1_w_up[l]), bf(ffn1_w_down[l]))
        mixa, q, kt, v, sg, sgb = _mixer_in(
            x1, row(mix_norm[l]), bf(w_in[l]), row(b_in[l]),
            w_in[l][:, k_lo:k_hi], b_in[l][k_lo:k_hi].reshape(-1, 1),
            row(sgu_norm_g[l]), row(sgu_norm_b[l]), bf(sgu_w_s[l]), sgu_b_s[l].reshape(SGU_GROUPS, CHUNK, 1),
            bf(w_branch_a[l]), cos, sin, cos.T, sin.T, seq)
        cb = _ret_bwd(ret_decay_logit[l], q, kt, v, batch, seq)
        x2 = _ret_fwd(ret_decay_logit[l], q, kt, v, cb, sg, sgb, mixa, x1, bf(w_branch_b[l]), bf(w_out[l]),
                      batch, seq)
        xt = _ffn(x2, row(ffn2_norm[l]), bf(ffn2_w_gate[l]), bf(ffn2_w_up[l]), bf(ffn2_w_down[l]),
                  row(final_norm) if last else None)
    return xt.reshape(batch, seq, D_MODEL)
```

```python
import functools

import jax
import jax.numpy as jnp
import numpy as np
from jax import lax
from jax.experimental import pallas as pl
from jax.experimental.pallas import tpu as pltpu

D_MODEL = 1024
D_FF = 2816
CHUNK = 128
SGU_GROUPS = 4
SGU_GROUP_DIM = D_MODEL // SGU_GROUPS
RET_HEADS = 4
RET_DIM = 256
SEG_U, SEG_V, SEG_Q, SEG_K, SEG_VR, SEG_G, SEG_GATE_A, SEG_GATE_B = range(8)
FORWARD, BACKWARD = 0, 1
ROPE_BASE = 10000.0
NORM_EPS = 1e-6

MXU_COLS = 256
FFN_ROWS = 512
MIX_ROWS = 512
RET_ROWS = 512
RET_BLOCK = 256
MIB = 1024 * 1024

BF16 = jnp.bfloat16
F32 = jnp.float32


def _rms(x, g):
    return x * lax.rsqrt(jnp.mean(x * x, axis=-1, keepdims=True) + NORM_EPS) * g


def _gelu(x):
    return 0.5 * x * (1.0 + lax.erf(x * np.float32(np.sqrt(0.5))))


def _silu(x):
    return x * jax.nn.sigmoid(x)


def _dot(a, b):
    return jnp.dot(a, b, preferred_element_type=F32)


def _dot_nt(a, b):
    return lax.dot_general(a, b, (((1,), (1,)), ((), ())), preferred_element_type=F32)


def _col_chunks(width=D_MODEL):
    return [slice(c, c + MXU_COLS) for c in range(0, width, MXU_COLS)]


def _resident(shape):
    return pl.BlockSpec(shape, lambda *_: (0,) * len(shape), pipeline_mode=pl.Buffered(1))


def _swiglu_half_step(x, nw_ref, wg_ref, wu_ref, wd_ref, act_ref):
    h = _rms(x, nw_ref[...]).astype(BF16)
    for cols in _col_chunks(D_FF):
        act_ref[:, cols] = (_silu(_dot(h, wg_ref[:, cols])) * _dot(h, wu_ref[:, cols])).astype(BF16)
    return x + 0.5 * _dot(act_ref[...], wd_ref[...])


def _ffn_body(x_ref, nw_ref, wg_ref, wu_ref, wd_ref, o_ref, act_ref):
    o_ref[...] = _swiglu_half_step(x_ref[...], nw_ref, wg_ref, wu_ref, wd_ref, act_ref)


def _ffn_weight_specs():
    return [_resident((1, D_MODEL)), _resident((D_MODEL, D_FF)), _resident((D_MODEL, D_FF)),
            _resident((D_FF, D_MODEL))]


def _ffn(x, norm_w, w_gate, w_up, w_down):
    t = x.shape[0]
    rows = pl.BlockSpec((FFN_ROWS, D_MODEL), lambda i: (i, 0))
    return pl.pallas_call(
        _ffn_body,
        grid=(t // FFN_ROWS,),
        in_specs=[rows] + _ffn_weight_specs(),
        out_specs=rows,
        out_shape=jax.ShapeDtypeStruct((t, D_MODEL), F32),
        scratch_shapes=[pltpu.VMEM((FFN_ROWS, D_FF), BF16)],
        compiler_params=pltpu.CompilerParams(dimension_semantics=("arbitrary",), vmem_limit_bytes=48 * MIB),
        name="ffn",
    )(x, norm_w, w_gate, w_up, w_down)


def _mixer_in_body(x_ref, nw_ref, w_ref, b_ref, wk_ref, bkt_ref, lng_ref, lnb_ref, ws_ref, bs_ref, wa_ref,
                   cos_ref, sin_ref, cost_ref, sint_ref,
                   mixa_ref, q_ref, kt_ref, v_ref, sg_ref, sgb_ref, wkt_ref, va_ref, vn_ref, a_ref):
    @pl.when(pl.program_id(0) == 0)
    def _():
        for r in _col_chunks():
            for c in _col_chunks():
                wkt_ref[c, r] = wk_ref[r, c].T.astype(BF16)

    h = _rms(x_ref[...], nw_ref[...]).astype(BF16)
    half = RET_DIM // 2

    def proj(seg, cols):
        wcols = slice(seg * D_MODEL + cols.start, seg * D_MODEL + cols.stop)
        return _dot(h, w_ref[:, wcols]) + b_ref[:, wcols]

    for cols in _col_chunks():
        va_ref[:, cols] = _gelu(proj(SEG_V, cols))
    va = va_ref[...]
    mu = jnp.mean(va, axis=-1, keepdims=True)
    vc = va - mu
    var = jnp.mean(vc * vc, axis=-1, keepdims=True)
    vn_ref[...] = (vc * lax.rsqrt(var + NORM_EPS) * lng_ref[...] + lnb_ref[...]).astype(BF16)
    for g, cols in enumerate(_col_chunks()):
        u = _gelu(proj(SEG_U, cols))
        for j in range(MIX_ROWS // CHUNK):
            rows = slice(j * CHUNK, (j + 1) * CHUNK)
            s = _dot(ws_ref[g], vn_ref[rows, cols]) + bs_ref[g]
            a_ref[rows, cols] = (u[rows] * s).astype(BF16)
    a = a_ref[...]
    for cols in _col_chunks():
        mixa_ref[:, cols] = (jax.nn.sigmoid(proj(SEG_GATE_A, cols)) * _dot(a, wa_ref[:, cols])).astype(BF16)

    cos, sin = cos_ref[...], sin_ref[...]
    cost, sint = cost_ref[...], sint_ref[...]
    k_scale = RET_DIM ** -0.5
    for cols in _col_chunks():
        t = proj(SEG_Q, cols)
        t1, t2 = t[:, :half], t[:, half:]
        q_ref[:, cols.start:cols.start + half] = (t1 * cos - t2 * sin).astype(BF16)
        q_ref[:, cols.start + half:cols.stop] = (t2 * cos + t1 * sin).astype(BF16)
        tt = _dot_nt(wkt_ref[cols, :], h) + bkt_ref[cols, :]
        t1, t2 = tt[:half], tt[half:]
        kt_ref[cols.start:cols.start + half, :] = ((t1 * cost - t2 * sint) * k_scale).astype(BF16)
        kt_ref[cols.start + half:cols.stop, :] = ((t2 * cost + t1 * sint) * k_scale).astype(BF16)
        v_ref[:, cols] = proj(SEG_VR, cols).astype(BF16)
        sg_ref[:, cols] = _silu(proj(SEG_G, cols)).astype(BF16)
        sgb_ref[:, cols] = jax.nn.sigmoid(proj(SEG_GATE_B, cols)).astype(BF16)


def _mixer_in(x1, norm_w, w_in, b_in, w_k, b_kt, ln_g, ln_b, w_s, b_s, w_a, cos, sin, cos_t, sin_t, seq):
    t = x1.shape[0]
    half = RET_DIM // 2
    steps_per_seq = seq // MIX_ROWS
    rows = pl.BlockSpec((MIX_ROWS, D_MODEL), lambda i: (i, 0))
    lanes = pl.BlockSpec((D_MODEL, MIX_ROWS), lambda i: (0, i))
    pos = pl.BlockSpec((MIX_ROWS, half), lambda i: (i % steps_per_seq, 0))
    pos_t = pl.BlockSpec((half, MIX_ROWS), lambda i: (0, i % steps_per_seq))
    width = w_in.shape[1]
    out = jax.ShapeDtypeStruct((t, D_MODEL), BF16)
    out_t = jax.ShapeDtypeStruct((D_MODEL, t), BF16)
    return pl.pallas_call(
        _mixer_in_body,
        grid=(t // MIX_ROWS,),
        in_specs=[rows, _resident((1, D_MODEL)), _resident((D_MODEL, width)), _resident((1, width)),
                  _resident((D_MODEL, D_MODEL)), _resident((D_MODEL, 1)),
                  _resident((1, D_MODEL)), _resident((1, D_MODEL)), _resident((SGU_GROUPS, CHUNK, CHUNK)),
                  _resident((SGU_GROUPS, CHUNK, 1)), _resident((D_MODEL, D_MODEL)), pos, pos, pos_t, pos_t],
        out_specs=[rows, rows, lanes, rows, rows, rows],
        out_shape=[out, out, out_t, out, out, out],
        scratch_shapes=[pltpu.VMEM((D_MODEL, D_MODEL), BF16), pltpu.VMEM((MIX_ROWS, D_MODEL), F32),
                        pltpu.VMEM((MIX_ROWS, D_MODEL), BF16), pltpu.VMEM((MIX_ROWS, D_MODEL), BF16)],
        compiler_params=pltpu.CompilerParams(dimension_semantics=("arbitrary",), vmem_limit_bytes=56 * MIB),
        name="mixer_in",
    )(x1, norm_w, w_in, b_in, w_k, b_kt, ln_g, ln_b, w_s, b_s, w_a, cos, sin, cos_t, sin_t)


def _log_gamma(logit, shape):
    return jax.nn.log_sigmoid(jnp.full(shape, logit, F32))


def _zero_state_at_sequence_start(st_ref):
    @pl.when(pl.program_id(1) == 0)
    def _():
        st_ref[...] = jnp.zeros_like(st_ref)


def _state_update(st_ref, hd, kt, v, k_scale, block_decay):
    kd = (kt.astype(F32) * k_scale).astype(BF16)
    st_ref[hd] = st_ref[hd] * block_decay + _dot(kd, v)


def _ret_bwd_body(dl_ref, q_ref, kt_ref, v_ref, o_ref, st_ref):
    _zero_state_at_sequence_start(st_ref)
    row_idx = lax.broadcasted_iota(jnp.int32, (RET_BLOCK, RET_DIM), 0).astype(F32)
    lane_idx = lax.broadcasted_iota(jnp.int32, (RET_DIM, RET_BLOCK), 1).astype(F32)
    for hd in range(RET_HEADS):
        cols = slice(hd * RET_DIM, (hd + 1) * RET_DIM)
        logit = dl_ref[BACKWARD, hd]
        q_scale = jnp.exp((RET_BLOCK - row_idx) * _log_gamma(logit, (RET_BLOCK, RET_DIM)))
        k_scale = jnp.exp(lane_idx * _log_gamma(logit, (RET_DIM, RET_BLOCK)))
        block_decay = jnp.exp(RET_BLOCK * _log_gamma(logit, (1, RET_DIM)))
        for blk in reversed(range(RET_ROWS // RET_BLOCK)):
            rows = slice(blk * RET_BLOCK, (blk + 1) * RET_BLOCK)
            o_ref[rows, cols] = (_dot(q_ref[rows, cols], st_ref[hd].astype(BF16)) * q_scale).astype(BF16)
            _state_update(st_ref, hd, kt_ref[cols, rows], v_ref[rows, cols], k_scale, block_decay)


def _mixer_out_body(*refs, final_norm):
    (dl_ref, q_ref, kt_ref, v_ref, cb_ref, sg_ref, sgb_ref, mixa_ref, x1_ref, wb_ref, wo_ref,
     nw_ref, wg_ref, wu_ref, wd_ref) = refs[:15]
    fn_ref = refs[15] if final_norm else None
    o_ref, st_ref, r_ref, mix_ref, x2_ref, act_ref = refs[-6:]
    _zero_state_at_sequence_start(st_ref)
    row_idx = lax.broadcasted_iota(jnp.int32, (RET_BLOCK, RET_DIM), 0).astype(F32)
    lane_idx = lax.broadcasted_iota(jnp.int32, (RET_DIM, RET_BLOCK), 1).astype(F32)
    diff = (lax.broadcasted_iota(jnp.int32, (RET_BLOCK, RET_BLOCK), 0)
            - lax.broadcasted_iota(jnp.int32, (RET_BLOCK, RET_BLOCK), 1)).astype(F32)
    for hd in range(RET_HEADS):
        cols = slice(hd * RET_DIM, (hd + 1) * RET_DIM)
        logit = dl_ref[FORWARD, hd]
        q_scale = jnp.exp((row_idx + 1.0) * _log_gamma(logit, (RET_BLOCK, RET_DIM)))
        k_scale = jnp.exp((RET_BLOCK - 1.0 - lane_idx) * _log_gamma(logit, (RET_DIM, RET_BLOCK)))
        block_decay = jnp.exp(RET_BLOCK * _log_gamma(logit, (1, RET_DIM)))
        lg_f = _log_gamma(logit, (RET_BLOCK, RET_BLOCK))
        lg_b = _log_gamma(dl_ref[BACKWARD, hd], (RET_BLOCK, RET_BLOCK))
        decay = jnp.exp(jnp.where(diff >= 0, diff * lg_f, -diff * lg_b))
        for blk in range(RET_ROWS // RET_BLOCK):
            rows = slice(blk * RET_BLOCK, (blk + 1) * RET_BLOCK)
            q = q_ref[rows, cols]
            kt = kt_ref[cols, rows]
            v = v_ref[rows, cols]
            r = (_dot((_dot(q, kt) * decay).astype(BF16), v)
                 + _dot(q, st_ref[hd].astype(BF16)) * q_scale
                 + cb_ref[rows, cols].astype(F32))
            r = r * lax.rsqrt(jnp.mean(r * r, axis=-1, keepdims=True) + NORM_EPS)
            r_ref[rows, cols] = (r * sg_ref[rows, cols].astype(F32)).astype(BF16)
            _state_update(st_ref, hd, kt, v, k_scale, block_decay)
    r = r_ref[...]
    for cols in _col_chunks():
        mix_ref[:, cols] = (sgb_ref[:, cols].astype(F32) * _dot(r, wb_ref[:, cols])
                            + mixa_ref[:, cols].astype(F32)).astype(BF16)
    mix = mix_ref[...]
    for cols in _col_chunks():
        x2_ref[:, cols] = x1_ref[:, cols] + _dot(mix, wo_ref[:, cols])
    y = _swiglu_half_step(x2_ref[...], nw_ref, wg_ref, wu_ref, wd_ref, act_ref)
    if final_norm:
        y = _rms(y, fn_ref[...])
    o_ref[...] = y


def _ret_specs(batch, seq, reverse):
    steps = seq // RET_ROWS
    if reverse:
        block = lambda b, j: b * steps + steps - 1 - j
    else:
        block = lambda b, j: b * steps + j
    rows = pl.BlockSpec((RET_ROWS, D_MODEL), lambda b, j: (block(b, j), 0))
    lanes = pl.BlockSpec((D_MODEL, RET_ROWS), lambda b, j: (0, block(b, j)))
    return (batch, steps), rows, lanes


def _ret_bwd(decay_logit, q, kt, v, batch, seq):
    grid, rows, lanes = _ret_specs(batch, seq, reverse=True)
    return pl.pallas_call(
        _ret_bwd_body,
        grid=grid,
        in_specs=[pl.BlockSpec(memory_space=pltpu.SMEM), rows, lanes, rows],
        out_specs=rows,
        out_shape=jax.ShapeDtypeStruct(q.shape, BF16),
        scratch_shapes=[pltpu.VMEM((RET_HEADS, RET_DIM, RET_DIM), F32)],
        compiler_params=pltpu.CompilerParams(dimension_semantics=("arbitrary", "arbitrary"),
                                             vmem_limit_bytes=32 * MIB),
        name="ret_bwd",
    )(decay_logit, q, kt, v)


def _mixer_out(decay_logit, q, kt, v, cb, sg, sgb, mixa, x1, w_b, w_o, norm_w, w_gate, w_up, w_down, final_w,
               batch, seq):
    grid, rows, lanes = _ret_specs(batch, seq, reverse=False)
    in_specs = ([pl.BlockSpec(memory_space=pltpu.SMEM), rows, lanes] + [rows] * 6
                + [_resident((D_MODEL, D_MODEL)), _resident((D_MODEL, D_MODEL))] + _ffn_weight_specs())
    args = [decay_logit, q, kt, v, cb, sg, sgb, mixa, x1, w_b, w_o, norm_w, w_gate, w_up, w_down]
    if final_w is not None:
        in_specs.append(_resident((1, D_MODEL)))
        args.append(final_w)
    return pl.pallas_call(
        functools.partial(_mixer_out_body, final_norm=final_w is not None),
        grid=grid,
        in_specs=in_specs,
        out_specs=rows,
        out_shape=jax.ShapeDtypeStruct(q.shape, F32),
        scratch_shapes=[pltpu.VMEM((RET_HEADS, RET_DIM, RET_DIM), F32), pltpu.VMEM((RET_ROWS, D_MODEL), BF16),
                        pltpu.VMEM((RET_ROWS, D_MODEL), BF16), pltpu.VMEM((RET_ROWS, D_MODEL), F32),
                        pltpu.VMEM((RET_ROWS, D_FF), BF16)],
        compiler_params=pltpu.CompilerParams(dimension_semantics=("arbitrary", "arbitrary"),
                                             vmem_limit_bytes=60 * MIB),
        name="mixer_out",
    )(*args)


def kernel(x, ffn1_norm, ffn1_w_gate, ffn1_w_up, ffn1_w_down, mix_norm, w_in, b_in, sgu_norm_g, sgu_norm_b, sgu_w_s, sgu_b_s, ret_decay_logit, w_branch_a, w_branch_b, w_out, ffn2_norm, ffn2_w_gate, ffn2_w_up, ffn2_w_down, final_norm):
    batch, seq, _ = x.shape
    depth = ffn1_norm.shape[0]
    assert depth >= 1 and seq % RET_ROWS == 0 and seq % MIX_ROWS == 0 and (batch * seq) % FFN_ROWS == 0
    tokens = batch * seq
    bf = lambda w: w.astype(BF16)
    row = lambda p: p.reshape(1, -1)

    theta = ROPE_BASE ** (-jnp.arange(0, RET_DIM, 2, dtype=F32) / RET_DIM)
    ang = jnp.arange(seq, dtype=F32)[:, None] * theta[None, :]
    cos, sin = jnp.cos(ang), jnp.sin(ang)

    k_lo, k_hi = SEG_K * D_MODEL, (SEG_K + 1) * D_MODEL
    xt = x.reshape(tokens, D_MODEL)
    for l in range(depth):
        last = l == depth - 1
        x1 = _ffn(xt, row(ffn1_norm[l]), bf(ffn1_w_gate[l]), bf(ffn1_w_up[l]), bf(ffn1_w_down[l]))
        mixa, q, kt, v, sg, sgb = _mixer_in(
            x1, row(mix_norm[l]), bf(w_in[l]), row(b_in[l]),
            w_in[l][:, k_lo:k_hi], b_in[l][k_lo:k_hi].reshape(-1, 1),
            row(sgu_norm_g[l]), row(sgu_norm_b[l]), bf(sgu_w_s[l]), sgu_b_s[l].reshape(SGU_GROUPS, CHUNK, 1),
            bf(w_branch_a[l]), cos, sin, cos.T, sin.T, seq)
        cb = _ret_bwd(ret_decay_logit[l], q, kt, v, batch, seq)
        xt = _mixer_out(ret_decay_logit[l], q, kt, v, cb, sg, sgb, mixa, x1, bf(w_branch_b[l]), bf(w_out[l]),
                        row(ffn2_norm[l]), bf(ffn2_w_gate[l]), bf(ffn2_w_up[l]), bf(ffn2_w_down[l]),
                        row(final_norm) if last else None, batch, seq)
    return xt.reshape(batch, seq, D_MODEL)
```

```python
import functools

import jax
import jax.numpy as jnp
import numpy as np
from jax import lax
from jax.experimental import pallas as pl
from jax.experimental.pallas import tpu as pltpu

D_MODEL = 1024
D_FF = 2816
CHUNK = 128
SGU_GROUPS = 4
SGU_GROUP_DIM = D_MODEL // SGU_GROUPS
RET_HEADS = 4
RET_DIM = 256
SEG_U, SEG_V, SEG_Q, SEG_K, SEG_VR, SEG_G, SEG_GATE_A, SEG_GATE_B = range(8)
FORWARD, BACKWARD = 0, 1
ROPE_BASE = 10000.0
NORM_EPS = 1e-6

MXU_COLS = 256
FFN_ROWS = 512
MIX_ROWS = 512
RET_ROWS = 512
RET_BLOCK = 256
MIB = 1024 * 1024

BF16 = jnp.bfloat16
F32 = jnp.float32


def _rms(x, g):
    return x * lax.rsqrt(jnp.mean(x * x, axis=-1, keepdims=True) + NORM_EPS) * g


def _gelu(x):
    return 0.5 * x * (1.0 + lax.erf(x * np.float32(np.sqrt(0.5))))


def _silu(x):
    return x * jax.nn.sigmoid(x)


def _dot(a, b):
    return jnp.dot(a, b, preferred_element_type=F32)


def _dot_nt(a, b):
    return lax.dot_general(a, b, (((1,), (1,)), ((), ())), preferred_element_type=F32)


def _col_chunks(width=D_MODEL):
    return [slice(c, c + MXU_COLS) for c in range(0, width, MXU_COLS)]


def _resident(shape):
    return pl.BlockSpec(shape, lambda *_: (0,) * len(shape), pipeline_mode=pl.Buffered(1))


def _swiglu_half_step(x, nw_ref, wg_ref, wu_ref, wd_ref, act_ref):
    h = _rms(x, nw_ref[...]).astype(BF16)
    for cols in _col_chunks(D_FF):
        g = _dot(h, wg_ref[:, cols].astype(BF16))
        u = _dot(h, wu_ref[:, cols].astype(BF16))
        act_ref[:, cols] = (_silu(g) * u).astype(BF16)
    act = act_ref[...]
    down = jnp.concatenate([_dot(act, wd_ref[:, cols].astype(BF16)) for cols in _col_chunks()], axis=-1)
    return x + 0.5 * down


def _ffn_body(x_ref, nw_ref, wg_ref, wu_ref, wd_ref, o_ref, act_ref):
    o_ref[...] = _swiglu_half_step(x_ref[...], nw_ref, wg_ref, wu_ref, wd_ref, act_ref)


def _ffn_weight_specs():
    return [_resident((1, D_MODEL)), _resident((D_MODEL, D_FF)), _resident((D_MODEL, D_FF)),
            _resident((D_FF, D_MODEL))]


def _ffn(x, norm_w, w_gate, w_up, w_down):
    t = x.shape[0]
    rows = pl.BlockSpec((FFN_ROWS, D_MODEL), lambda i: (i, 0))
    return pl.pallas_call(
        _ffn_body,
        grid=(t // FFN_ROWS,),
        in_specs=[rows] + _ffn_weight_specs(),
        out_specs=rows,
        out_shape=jax.ShapeDtypeStruct((t, D_MODEL), F32),
        scratch_shapes=[pltpu.VMEM((FFN_ROWS, D_FF), BF16)],
        compiler_params=pltpu.CompilerParams(dimension_semantics=("arbitrary",), vmem_limit_bytes=56 * MIB),
        name="ffn",
    )(x, norm_w, w_gate, w_up, w_down)


def _mixer_in_body(x_ref, nw_ref, w_ref, b_ref, wk_ref, bkt_ref, lng_ref, lnb_ref, ws_ref, bs_ref, wa_ref,
                   cos_ref, sin_ref, cost_ref, sint_ref,
                   mixa_ref, q_ref, kt_ref, v_ref, sg_ref, sgb_ref, wkt_ref, va_ref, vn_ref, a_ref):
    @pl.when(pl.program_id(0) == 0)
    def _():
        for r in _col_chunks():
            for c in _col_chunks():
                wkt_ref[c, r] = wk_ref[r, c].T.astype(BF16)

    h = _rms(x_ref[...], nw_ref[...]).astype(BF16)
    half = RET_DIM // 2

    def proj(seg, cols):
        wcols = slice(seg * D_MODEL + cols.start, seg * D_MODEL + cols.stop)
        return _dot(h, w_ref[:, wcols]) + b_ref[:, wcols]

    for cols in _col_chunks():
        va_ref[:, cols] = _gelu(proj(SEG_V, cols))
    va = va_ref[...]
    mu = jnp.mean(va, axis=-1, keepdims=True)
    vc = va - mu
    var = jnp.mean(vc * vc, axis=-1, keepdims=True)
    vn_ref[...] = (vc * lax.rsqrt(var + NORM_EPS) * lng_ref[...] + lnb_ref[...]).astype(BF16)
    for g, cols in enumerate(_col_chunks()):
        u = _gelu(proj(SEG_U, cols))
        for j in range(MIX_ROWS // CHUNK):
            rows = slice(j * CHUNK, (j + 1) * CHUNK)
            s = _dot(ws_ref[g], vn_ref[rows, cols]) + bs_ref[g]
            a_ref[rows, cols] = (u[rows] * s).astype(BF16)
    a = a_ref[...]
    for cols in _col_chunks():
        mixa_ref[:, cols] = (jax.nn.sigmoid(proj(SEG_GATE_A, cols)) * _dot(a, wa_ref[:, cols])).astype(BF16)

    cos, sin = cos_ref[...], sin_ref[...]
    cost, sint = cost_ref[...], sint_ref[...]
    k_scale = RET_DIM ** -0.5
    for cols in _col_chunks():
        t = proj(SEG_Q, cols)
        t1, t2 = t[:, :half], t[:, half:]
        q_ref[:, cols.start:cols.start + half] = (t1 * cos - t2 * sin).astype(BF16)
        q_ref[:, cols.start + half:cols.stop] = (t2 * cos + t1 * sin).astype(BF16)
        tt = _dot_nt(wkt_ref[cols, :], h) + bkt_ref[cols, :]
        t1, t2 = tt[:half], tt[half:]
        kt_ref[cols.start:cols.start + half, :] = ((t1 * cost - t2 * sint) * k_scale).astype(BF16)
        kt_ref[cols.start + half:cols.stop, :] = ((t2 * cost + t1 * sint) * k_scale).astype(BF16)
        v_ref[:, cols] = proj(SEG_VR, cols).astype(BF16)
        sg_ref[:, cols] = _silu(proj(SEG_G, cols)).astype(BF16)
        sgb_ref[:, cols] = jax.nn.sigmoid(proj(SEG_GATE_B, cols)).astype(BF16)


def _mixer_in(x1, norm_w, w_in, b_in, w_k, b_kt, ln_g, ln_b, w_s, b_s, w_a, cos, sin, cos_t, sin_t, seq):
    t = x1.shape[0]
    half = RET_DIM // 2
    steps_per_seq = seq // MIX_ROWS
    rows = pl.BlockSpec((MIX_ROWS, D_MODEL), lambda i: (i, 0))
    lanes = pl.BlockSpec((D_MODEL, MIX_ROWS), lambda i: (0, i))
    pos = pl.BlockSpec((MIX_ROWS, half), lambda i: (i % steps_per_seq, 0))
    pos_t = pl.BlockSpec((half, MIX_ROWS), lambda i: (0, i % steps_per_seq))
    width = w_in.shape[1]
    out = jax.ShapeDtypeStruct((t, D_MODEL), BF16)
    out_t = jax.ShapeDtypeStruct((D_MODEL, t), BF16)
    return pl.pallas_call(
        _mixer_in_body,
        grid=(t // MIX_ROWS,),
        in_specs=[rows, _resident((1, D_MODEL)), _resident((D_MODEL, width)), _resident((1, width)),
                  _resident((D_MODEL, D_MODEL)), _resident((D_MODEL, 1)),
                  _resident((1, D_MODEL)), _resident((1, D_MODEL)), _resident((SGU_GROUPS, CHUNK, CHUNK)),
                  _resident((SGU_GROUPS, CHUNK, 1)), _resident((D_MODEL, D_MODEL)), pos, pos, pos_t, pos_t],
        out_specs=[rows, rows, lanes, rows, rows, rows],
        out_shape=[out, out, out_t, out, out, out],
        scratch_shapes=[pltpu.VMEM((D_MODEL, D_MODEL), BF16), pltpu.VMEM((MIX_ROWS, D_MODEL), F32),
                        pltpu.VMEM((MIX_ROWS, D_MODEL), BF16), pltpu.VMEM((MIX_ROWS, D_MODEL), BF16)],
        compiler_params=pltpu.CompilerParams(dimension_semantics=("arbitrary",), vmem_limit_bytes=56 * MIB),
        name="mixer_in",
    )(x1, norm_w, w_in, b_in, w_k, b_kt, ln_g, ln_b, w_s, b_s, w_a, cos, sin, cos_t, sin_t)


def _log_gamma(logit, shape):
    return jax.nn.log_sigmoid(jnp.full(shape, logit, F32))


def _zero_state_at_sequence_start(st_ref):
    @pl.when(pl.program_id(1) == 0)
    def _():
        st_ref[...] = jnp.zeros_like(st_ref)


def _state_update(st_ref, hd, kt, v, k_scale, block_decay):
    kd = (kt.astype(F32) * k_scale).astype(BF16)
    st_ref[hd] = st_ref[hd] * block_decay + _dot(kd, v)


def _ret_bwd_body(dl_ref, q_ref, kt_ref, v_ref, o_ref, st_ref):
    _zero_state_at_sequence_start(st_ref)
    row_idx = lax.broadcasted_iota(jnp.int32, (RET_BLOCK, RET_DIM), 0).astype(F32)
    lane_idx = lax.broadcasted_iota(jnp.int32, (RET_DIM, RET_BLOCK), 1).astype(F32)
    for hd in range(RET_HEADS):
        cols = slice(hd * RET_DIM, (hd + 1) * RET_DIM)
        logit = dl_ref[BACKWARD, hd]
        q_scale = jnp.exp((RET_BLOCK - row_idx) * _log_gamma(logit, (RET_BLOCK, RET_DIM)))
        k_scale = jnp.exp(lane_idx * _log_gamma(logit, (RET_DIM, RET_BLOCK)))
        block_decay = jnp.exp(RET_BLOCK * _log_gamma(logit, (1, RET_DIM)))
        for blk in reversed(range(RET_ROWS // RET_BLOCK)):
            rows = slice(blk * RET_BLOCK, (blk + 1) * RET_BLOCK)
            o_ref[rows, cols] = (_dot(q_ref[rows, cols], st_ref[hd].astype(BF16)) * q_scale).astype(BF16)
            _state_update(st_ref, hd, kt_ref[cols, rows], v_ref[rows, cols], k_scale, block_decay)


def _mixer_out_body(*refs, final_norm):
    (dl_ref, q_ref, kt_ref, v_ref, cb_ref, sg_ref, sgb_ref, mixa_ref, x1_ref, wb_ref, wo_ref,
     nw_ref, wg_ref, wu_ref, wd_ref) = refs[:15]
    fn_ref = refs[15] if final_norm else None
    o_ref, st_ref, r_ref, mix_ref, x2_ref, act_ref = refs[-6:]
    _zero_state_at_sequence_start(st_ref)
    row_idx = lax.broadcasted_iota(jnp.int32, (RET_BLOCK, RET_DIM), 0).astype(F32)
    lane_idx = lax.broadcasted_iota(jnp.int32, (RET_DIM, RET_BLOCK), 1).astype(F32)
    diff = (lax.broadcasted_iota(jnp.int32, (RET_BLOCK, RET_BLOCK), 0)
            - lax.broadcasted_iota(jnp.int32, (RET_BLOCK, RET_BLOCK), 1)).astype(F32)
    for hd in range(RET_HEADS):
        cols = slice(hd * RET_DIM, (hd + 1) * RET_DIM)
        logit = dl_ref[FORWARD, hd]
        q_scale = jnp.exp((row_idx + 1.0) * _log_gamma(logit, (RET_BLOCK, RET_DIM)))
        k_scale = jnp.exp((RET_BLOCK - 1.0 - lane_idx) * _log_gamma(logit, (RET_DIM, RET_BLOCK)))
        block_decay = jnp.exp(RET_BLOCK * _log_gamma(logit, (1, RET_DIM)))
        lg_f = _log_gamma(logit, (RET_BLOCK, RET_BLOCK))
        lg_b = _log_gamma(dl_ref[BACKWARD, hd], (RET_BLOCK, RET_BLOCK))
        decay = jnp.exp(jnp.where(diff >= 0, diff * lg_f, -diff * lg_b))
        for blk in range(RET_ROWS // RET_BLOCK):
            rows = slice(blk * RET_BLOCK, (blk + 1) * RET_BLOCK)
            q = q_ref[rows, cols]
            kt = kt_ref[cols, rows]
            v = v_ref[rows, cols]
            r = (_dot((_dot(q, kt) * decay).astype(BF16), v)
                 + _dot(q, st_ref[hd].astype(BF16)) * q_scale
                 + cb_ref[rows, cols].astype(F32))
            r = r * lax.rsqrt(jnp.mean(r * r, axis=-1, keepdims=True) + NORM_EPS)
            r_ref[rows, cols] = (r * sg_ref[rows, cols].astype(F32)).astype(BF16)
            _state_update(st_ref, hd, kt, v, k_scale, block_decay)
    r = r_ref[...]
    for cols in _col_chunks():
        mix_ref[:, cols] = (sgb_ref[:, cols].astype(F32) * _dot(r, wb_ref[:, cols])
                            + mixa_ref[:, cols].astype(F32)).astype(BF16)
    mix = mix_ref[...]
    for cols in _col_chunks():
        x2_ref[:, cols] = x1_ref[:, cols] + _dot(mix, wo_ref[:, cols])
    y = _swiglu_half_step(x2_ref[...], nw_ref, wg_ref, wu_ref, wd_ref, act_ref)
    if final_norm:
        y = _rms(y, fn_ref[...])
    o_ref[...] = y


def _ret_specs(batch, seq, reverse):
    steps = seq // RET_ROWS
    if reverse:
        block = lambda b, j: b * steps + steps - 1 - j
    else:
        block = lambda b, j: b * steps + j
    rows = pl.BlockSpec((RET_ROWS, D_MODEL), lambda b, j: (block(b, j), 0))
    lanes = pl.BlockSpec((D_MODEL, RET_ROWS), lambda b, j: (0, block(b, j)))
    return (batch, steps), rows, lanes


def _ret_bwd(decay_logit, q, kt, v, batch, seq):
    grid, rows, lanes = _ret_specs(batch, seq, reverse=True)
    return pl.pallas_call(
        _ret_bwd_body,
        grid=grid,
        in_specs=[pl.BlockSpec(memory_space=pltpu.SMEM), rows, lanes, rows],
        out_specs=rows,
        out_shape=jax.ShapeDtypeStruct(q.shape, BF16),
        scratch_shapes=[pltpu.VMEM((RET_HEADS, RET_DIM, RET_DIM), F32)],
        compiler_params=pltpu.CompilerParams(dimension_semantics=("arbitrary", "arbitrary"),
                                             vmem_limit_bytes=32 * MIB),
        name="ret_bwd",
    )(decay_logit, q, kt, v)


def _mixer_out(decay_logit, q, kt, v, cb, sg, sgb, mixa, x1, w_b, w_o, norm_w, w_gate, w_up, w_down, final_w,
               batch, seq):
    grid, rows, lanes = _ret_specs(batch, seq, reverse=False)
    in_specs = ([pl.BlockSpec(memory_space=pltpu.SMEM), rows, lanes] + [rows] * 6
                + [_resident((D_MODEL, D_MODEL)), _resident((D_MODEL, D_MODEL))] + _ffn_weight_specs())
    args = [decay_logit, q, kt, v, cb, sg, sgb, mixa, x1, w_b, w_o, norm_w, w_gate, w_up, w_down]
    if final_w is not None:
        in_specs.append(_resident((1, D_MODEL)))
        args.append(final_w)
    return pl.pallas_call(
        functools.partial(_mixer_out_body, final_norm=final_w is not None),
        grid=grid,
        in_specs=in_specs,
        out_specs=rows,
        out_shape=jax.ShapeDtypeStruct(q.shape, F32),
        scratch_shapes=[pltpu.VMEM((RET_HEADS, RET_DIM, RET_DIM), F32), pltpu.VMEM((RET_ROWS, D_MODEL), BF16),
                        pltpu.VMEM((RET_ROWS, D_MODEL), BF16), pltpu.VMEM((RET_ROWS, D_MODEL), F32),
                        pltpu.VMEM((RET_ROWS, D_FF), BF16)],
        compiler_params=pltpu.CompilerParams(dimension_semantics=("arbitrary", "arbitrary"),
                                             vmem_limit_bytes=60 * MIB),
        name="mixer_out",
    )(*args)


def kernel(x, ffn1_norm, ffn1_w_gate, ffn1_w_up, ffn1_w_down, mix_norm, w_in, b_in, sgu_norm_g, sgu_norm_b, sgu_w_s, sgu_b_s, ret_decay_logit, w_branch_a, w_branch_b, w_out, ffn2_norm, ffn2_w_gate, ffn2_w_up, ffn2_w_down, final_norm):
    batch, seq, _ = x.shape
    depth = ffn1_norm.shape[0]
    assert depth >= 1 and seq % RET_ROWS == 0 and seq % MIX_ROWS == 0 and (batch * seq) % FFN_ROWS == 0
    tokens = batch * seq
    bf = lambda w: w.astype(BF16)
    row = lambda p: p.reshape(1, -1)

    theta = ROPE_BASE ** (-jnp.arange(0, RET_DIM, 2, dtype=F32) / RET_DIM)
    ang = jnp.arange(seq, dtype=F32)[:, None] * theta[None, :]
    cos, sin = jnp.cos(ang), jnp.sin(ang)

    k_lo, k_hi = SEG_K * D_MODEL, (SEG_K + 1) * D_MODEL
    xt = x.reshape(tokens, D_MODEL)
    for l in range(depth):
        last = l == depth - 1
        x1 = _ffn(xt, row(ffn1_norm[l]), ffn1_w_gate[l], ffn1_w_up[l], ffn1_w_down[l])
        mixa, q, kt, v, sg, sgb = _mixer_in(
            x1, row(mix_norm[l]), bf(w_in[l]), row(b_in[l]),
            w_in[l][:, k_lo:k_hi], b_in[l][k_lo:k_hi].reshape(-1, 1),
            row(sgu_norm_g[l]), row(sgu_norm_b[l]), bf(sgu_w_s[l]), sgu_b_s[l].reshape(SGU_GROUPS, CHUNK, 1),
            bf(w_branch_a[l]), cos, sin, cos.T, sin.T, seq)
        cb = _ret_bwd(ret_decay_logit[l], q, kt, v, batch, seq)
        xt = _mixer_out(ret_decay_logit[l], q, kt, v, cb, sg, sgb, mixa, x1, bf(w_branch_b[l]), bf(w_out[l]),
                        row(ffn2_norm[l]), bf(ffn2_w_gate[l]), bf(ffn2_w_up[l]), bf(ffn2_w_down[l]),
                        row(final_norm) if last else None, batch, seq)
    return xt.reshape(batch, seq, D_MODEL)
```

```python
import functools

import jax
import jax.numpy as jnp
import numpy as np
from jax import lax
from jax.experimental import pallas as pl
from jax.experimental.pallas import tpu as pltpu

D_MODEL = 1024
D_FF = 2816
CHUNK = 128
SGU_GROUPS = 4
SGU_GROUP_DIM = D_MODEL // SGU_GROUPS
RET_HEADS = 4
RET_DIM = 256
SEG_U, SEG_V, SEG_Q, SEG_K, SEG_VR, SEG_G, SEG_GATE_A, SEG_GATE_B = range(8)
FORWARD, BACKWARD = 0, 1
ROPE_BASE = 10000.0
NORM_EPS = 1e-6

MXU_COLS = 256
BF16_SUBLANES = 16
FFN_ROWS = 512
MIX_ROWS = 512
RET_ROWS = 512
RET_BLOCK = 256
MIB = 1024 * 1024

BF16 = jnp.bfloat16
F32 = jnp.float32


def _rms(x, g):
    return x * lax.rsqrt(jnp.mean(x * x, axis=-1, keepdims=True) + NORM_EPS) * g


def _gelu(x):
    return 0.5 * x * (1.0 + lax.erf(x * np.float32(np.sqrt(0.5))))


def _silu(x):
    return x * jax.nn.sigmoid(x)


def _dot(a, b):
    return jnp.dot(a, b, preferred_element_type=F32)


def _dot_nt(a, b):
    return lax.dot_general(a, b, (((1,), (1,)), ((), ())), preferred_element_type=F32)


def _col_chunks(width=D_MODEL):
    return [slice(c, c + MXU_COLS) for c in range(0, width, MXU_COLS)]


def _resident(shape):
    return pl.BlockSpec(shape, lambda *_: (0,) * len(shape), pipeline_mode=pl.Buffered(1))


def _swiglu_half_step(x, nw_ref, wg_ref, wu_ref, wd_ref, act_ref):
    h = _rms(x, nw_ref[...]).astype(BF16)
    for cols in _col_chunks(D_FF):
        g = _dot(h, wg_ref[:, cols].astype(BF16))
        u = _dot(h, wu_ref[:, cols].astype(BF16))
        act_ref[:, cols] = (_silu(g) * u).astype(BF16)
    act = act_ref[...]
    down = jnp.concatenate([_dot(act, wd_ref[:, cols].astype(BF16)) for cols in _col_chunks()], axis=-1)
    return x + 0.5 * down


def _ffn_body(*refs, n_cast, pos_rows):
    n = n_cast
    x_ref, nw_ref, wg_ref, wu_ref, wd_ref, theta_ref, theta_t_ref = refs[:7]
    srcs = refs[7:7 + n]
    o_ref = refs[7 + n]
    dsts = refs[8 + n:8 + 2 * n]
    cos_ref, sin_ref, cost_ref, sint_ref, act_ref = refs[8 + 2 * n:]
    step = pl.program_id(0)

    for src, dst in zip(srcs, dsts):
        dst[...] = src[...].astype(BF16)

    first = step * pos_rows
    pos = (first + lax.broadcasted_iota(jnp.int32, (pos_rows, RET_DIM // 2), 0)).astype(F32)
    ang = pos * theta_ref[...]
    cos_ref[...] = jnp.cos(ang)
    sin_ref[...] = jnp.sin(ang)
    pos_t = (first + lax.broadcasted_iota(jnp.int32, (RET_DIM // 2, pos_rows), 1)).astype(F32)
    ang_t = pos_t * theta_t_ref[...]
    cost_ref[...] = jnp.cos(ang_t)
    sint_ref[...] = jnp.sin(ang_t)

    o_ref[...] = _swiglu_half_step(x_ref[...], nw_ref, wg_ref, wu_ref, wd_ref, act_ref)


def _ffn_weight_specs():
    return [_resident((1, D_MODEL)), _resident((D_MODEL, D_FF)), _resident((D_MODEL, D_FF)),
            _resident((D_FF, D_MODEL))]


def _cast_rows(n_rows, n_steps):
    for rows in range(BF16_SUBLANES, n_rows + 1, BF16_SUBLANES):
        if n_rows % rows == 0 and n_rows // rows <= n_steps:
            return rows
    raise ValueError(f"cannot split {n_rows} rows over {n_steps} steps")


def _ffn(x, norm_w, w_gate, w_up, w_down, theta, later_weights, seq):
    t = x.shape[0]
    n_steps = t // FFN_ROWS
    pos_rows = seq // n_steps
    half = RET_DIM // 2
    assert seq % n_steps == 0 and pos_rows % 128 == 0
    rows = pl.BlockSpec((FFN_ROWS, D_MODEL), lambda i: (i, 0))

    cast_specs, cast_shapes = [], []
    for w in later_weights:
        r = _cast_rows(w.shape[0], n_steps)
        steps = w.shape[0] // r
        cast_specs.append(pl.BlockSpec((r, w.shape[1]), lambda i, last=steps - 1: (jnp.minimum(i, last), 0)))
        cast_shapes.append(jax.ShapeDtypeStruct(w.shape, BF16))
    table = pl.BlockSpec((pos_rows, half), lambda i: (i, 0))
    table_t = pl.BlockSpec((half, pos_rows), lambda i: (0, i))

    outs = pl.pallas_call(
        functools.partial(_ffn_body, n_cast=len(later_weights), pos_rows=pos_rows),
        grid=(n_steps,),
        in_specs=[rows] + _ffn_weight_specs() + [_resident((1, half)), _resident((half, 1))] + cast_specs,
        out_specs=[rows] + cast_specs + [table, table, table_t, table_t],
        out_shape=[jax.ShapeDtypeStruct((t, D_MODEL), F32)] + cast_shapes
                  + [jax.ShapeDtypeStruct((seq, half), F32)] * 2 + [jax.ShapeDtypeStruct((half, seq), F32)] * 2,
        scratch_shapes=[pltpu.VMEM((FFN_ROWS, D_FF), BF16)],
        compiler_params=pltpu.CompilerParams(dimension_semantics=("arbitrary",), vmem_limit_bytes=60 * MIB),
        name="ffn",
    )(x, norm_w, w_gate, w_up, w_down, theta.reshape(1, half), theta.reshape(half, 1), *later_weights)
    n = len(later_weights)
    return outs[0], outs[1:1 + n], outs[1 + n:]


def _mixer_in_body(x_ref, nw_ref, w_ref, b_ref, wk_ref, bkt_ref, lng_ref, lnb_ref, ws_ref, bs_ref, wa_ref,
                   cos_ref, sin_ref, cost_ref, sint_ref,
                   mixa_ref, q_ref, kt_ref, v_ref, sg_ref, sgb_ref, wkt_ref, va_ref, vn_ref, a_ref):
    @pl.when(pl.program_id(0) == 0)
    def _():
        for r in _col_chunks():
            for c in _col_chunks():
                wkt_ref[c, r] = wk_ref[r, c].T.astype(BF16)

    h = _rms(x_ref[...], nw_ref[...]).astype(BF16)
    half = RET_DIM // 2

    def proj(seg, cols):
        wcols = slice(seg * D_MODEL + cols.start, seg * D_MODEL + cols.stop)
        return _dot(h, w_ref[:, wcols]) + b_ref[:, wcols]

    for cols in _col_chunks():
        va_ref[:, cols] = _gelu(proj(SEG_V, cols))
    va = va_ref[...]
    mu = jnp.mean(va, axis=-1, keepdims=True)
    vc = va - mu
    var = jnp.mean(vc * vc, axis=-1, keepdims=True)
    vn_ref[...] = (vc * lax.rsqrt(var + NORM_EPS) * lng_ref[...] + lnb_ref[...]).astype(BF16)
    for g, cols in enumerate(_col_chunks()):
        u = _gelu(proj(SEG_U, cols))
        for j in range(MIX_ROWS // CHUNK):
            rows = slice(j * CHUNK, (j + 1) * CHUNK)
            s = _dot(ws_ref[g], vn_ref[rows, cols]) + bs_ref[g]
            a_ref[rows, cols] = (u[rows] * s).astype(BF16)
    a = a_ref[...]
    for cols in _col_chunks():
        mixa_ref[:, cols] = (jax.nn.sigmoid(proj(SEG_GATE_A, cols)) * _dot(a, wa_ref[:, cols])).astype(BF16)

    cos, sin = cos_ref[...], sin_ref[...]
    cost, sint = cost_ref[...], sint_ref[...]
    k_scale = RET_DIM ** -0.5
    for cols in _col_chunks():
        t = proj(SEG_Q, cols)
        t1, t2 = t[:, :half], t[:, half:]
        q_ref[:, cols.start:cols.start + half] = (t1 * cos - t2 * sin).astype(BF16)
        q_ref[:, cols.start + half:cols.stop] = (t2 * cos + t1 * sin).astype(BF16)
        tt = _dot_nt(wkt_ref[cols, :], h) + bkt_ref[cols, :]
        t1, t2 = tt[:half], tt[half:]
        kt_ref[cols.start:cols.start + half, :] = ((t1 * cost - t2 * sint) * k_scale).astype(BF16)
        kt_ref[cols.start + half:cols.stop, :] = ((t2 * cost + t1 * sint) * k_scale).astype(BF16)
        v_ref[:, cols] = proj(SEG_VR, cols).astype(BF16)
        sg_ref[:, cols] = _silu(proj(SEG_G, cols)).astype(BF16)
        sgb_ref[:, cols] = jax.nn.sigmoid(proj(SEG_GATE_B, cols)).astype(BF16)


def _mixer_in(x1, norm_w, w_in, b_in, w_k, b_kt, ln_g, ln_b, w_s, b_s, w_a, cos, sin, cos_t, sin_t, seq):
    t = x1.shape[0]
    half = RET_DIM // 2
    steps_per_seq = seq // MIX_ROWS
    rows = pl.BlockSpec((MIX_ROWS, D_MODEL), lambda i: (i, 0))
    lanes = pl.BlockSpec((D_MODEL, MIX_ROWS), lambda i: (0, i))
    pos = pl.BlockSpec((MIX_ROWS, half), lambda i: (i % steps_per_seq, 0))
    pos_t = pl.BlockSpec((half, MIX_ROWS), lambda i: (0, i % steps_per_seq))
    width = w_in.shape[1]
    out = jax.ShapeDtypeStruct((t, D_MODEL), BF16)
    out_t = jax.ShapeDtypeStruct((D_MODEL, t), BF16)
    return pl.pallas_call(
        _mixer_in_body,
        grid=(t // MIX_ROWS,),
        in_specs=[rows, _resident((1, D_MODEL)), _resident((D_MODEL, width)), _resident((1, width)),
                  _resident((D_MODEL, D_MODEL)), _resident((D_MODEL, 1)),
                  _resident((1, D_MODEL)), _resident((1, D_MODEL)), _resident((SGU_GROUPS, CHUNK, CHUNK)),
                  _resident((SGU_GROUPS, CHUNK, 1)), _resident((D_MODEL, D_MODEL)), pos, pos, pos_t, pos_t],
        out_specs=[rows, rows, lanes, rows, rows, rows],
        out_shape=[out, out, out_t, out, out, out],
        scratch_shapes=[pltpu.VMEM((D_MODEL, D_MODEL), BF16), pltpu.VMEM((MIX_ROWS, D_MODEL), F32),
                        pltpu.VMEM((MIX_ROWS, D_MODEL), BF16), pltpu.VMEM((MIX_ROWS, D_MODEL), BF16)],
        compiler_params=pltpu.CompilerParams(dimension_semantics=("arbitrary",), vmem_limit_bytes=56 * MIB),
        name="mixer_in",
    )(x1, norm_w, w_in, b_in, w_k, b_kt, ln_g, ln_b, w_s, b_s, w_a, cos, sin, cos_t, sin_t)


def _log_gamma(logit, shape):
    return jax.nn.log_sigmoid(jnp.full(shape, logit, F32))


def _zero_state_at_sequence_start(st_ref):
    @pl.when(pl.program_id(1) == 0)
    def _():
        st_ref[...] = jnp.zeros_like(st_ref)


def _state_update(st_ref, hd, kt, v, k_scale, block_decay):
    kd = (kt.astype(F32) * k_scale).astype(BF16)
    st_ref[hd] = st_ref[hd] * block_decay + _dot(kd, v)


def _ret_bwd_body(dl_ref, q_ref, kt_ref, v_ref, o_ref, st_ref):
    _zero_state_at_sequence_start(st_ref)
    row_idx = lax.broadcasted_iota(jnp.int32, (RET_BLOCK, RET_DIM), 0).astype(F32)
    lane_idx = lax.broadcasted_iota(jnp.int32, (RET_DIM, RET_BLOCK), 1).astype(F32)
    for hd in range(RET_HEADS):
        cols = slice(hd * RET_DIM, (hd + 1) * RET_DIM)
        logit = dl_ref[BACKWARD, hd]
        q_scale = jnp.exp((RET_BLOCK - row_idx) * _log_gamma(logit, (RET_BLOCK, RET_DIM)))
        k_scale = jnp.exp(lane_idx * _log_gamma(logit, (RET_DIM, RET_BLOCK)))
        block_decay = jnp.exp(RET_BLOCK * _log_gamma(logit, (1, RET_DIM)))
        for blk in reversed(range(RET_ROWS // RET_BLOCK)):
            rows = slice(blk * RET_BLOCK, (blk + 1) * RET_BLOCK)
            o_ref[rows, cols] = (_dot(q_ref[rows, cols], st_ref[hd].astype(BF16)) * q_scale).astype(BF16)
            _state_update(st_ref, hd, kt_ref[cols, rows], v_ref[rows, cols], k_scale, block_decay)


def _mixer_out_body(*refs, final_norm):
    (dl_ref, q_ref, kt_ref, v_ref, cb_ref, sg_ref, sgb_ref, mixa_ref, x1_ref, wb_ref, wo_ref,
     nw_ref, wg_ref, wu_ref, wd_ref) = refs[:15]
    fn_ref = refs[15] if final_norm else None
    o_ref, st_ref, r_ref, mix_ref, x2_ref, act_ref = refs[-6:]
    _zero_state_at_sequence_start(st_ref)
    row_idx = lax.broadcasted_iota(jnp.int32, (RET_BLOCK, RET_DIM), 0).astype(F32)
    lane_idx = lax.broadcasted_iota(jnp.int32, (RET_DIM, RET_BLOCK), 1).astype(F32)
    diff = (lax.broadcasted_iota(jnp.int32, (RET_BLOCK, RET_BLOCK), 0)
            - lax.broadcasted_iota(jnp.int32, (RET_BLOCK, RET_BLOCK), 1)).astype(F32)
    for hd in range(RET_HEADS):
        cols = slice(hd * RET_DIM, (hd + 1) * RET_DIM)
        logit = dl_ref[FORWARD, hd]
        q_scale = jnp.exp((row_idx + 1.0) * _log_gamma(logit, (RET_BLOCK, RET_DIM)))
        k_scale = jnp.exp((RET_BLOCK - 1.0 - lane_idx) * _log_gamma(logit, (RET_DIM, RET_BLOCK)))
        block_decay = jnp.exp(RET_BLOCK * _log_gamma(logit, (1, RET_DIM)))
        lg_f = _log_gamma(logit, (RET_BLOCK, RET_BLOCK))
        lg_b = _log_gamma(dl_ref[BACKWARD, hd], (RET_BLOCK, RET_BLOCK))
        decay = jnp.exp(jnp.where(diff >= 0, diff * lg_f, -diff * lg_b))
        for blk in range(RET_ROWS // RET_BLOCK):
            rows = slice(blk * RET_BLOCK, (blk + 1) * RET_BLOCK)
            q = q_ref[rows, cols]
            kt = kt_ref[cols, rows]
            v = v_ref[rows, cols]
            r = (_dot((_dot(q, kt) * decay).astype(BF16), v)
                 + _dot(q, st_ref[hd].astype(BF16)) * q_scale
                 + cb_ref[rows, cols].astype(F32))
            r = r * lax.rsqrt(jnp.mean(r * r, axis=-1, keepdims=True) + NORM_EPS)
            r_ref[rows, cols] = (r * sg_ref[rows, cols].astype(F32)).astype(BF16)
            _state_update(st_ref, hd, kt, v, k_scale, block_decay)
    r = r_ref[...]
    for cols in _col_chunks():
        mix_ref[:, cols] = (sgb_ref[:, cols].astype(F32) * _dot(r, wb_ref[:, cols])
                            + mixa_ref[:, cols].astype(F32)).astype(BF16)
    mix = mix_ref[...]
    for cols in _col_chunks():
        x2_ref[:, cols] = x1_ref[:, cols] + _dot(mix, wo_ref[:, cols])
    y = _swiglu_half_step(x2_ref[...], nw_ref, wg_ref, wu_ref, wd_ref, act_ref)
    if final_norm:
        y = _rms(y, fn_ref[...])
    o_ref[...] = y


def _ret_specs(batch, seq, reverse):
    steps = seq // RET_ROWS
    if reverse:
        block = lambda b, j: b * steps + steps - 1 - j
    else:
        block = lambda b, j: b * steps + j
    rows = pl.BlockSpec((RET_ROWS, D_MODEL), lambda b, j: (block(b, j), 0))
    lanes = pl.BlockSpec((D_MODEL, RET_ROWS), lambda b, j: (0, block(b, j)))
    return (batch, steps), rows, lanes


def _ret_bwd(decay_logit, q, kt, v, batch, seq):
    grid, rows, lanes = _ret_specs(batch, seq, reverse=True)
    return pl.pallas_call(
        _ret_bwd_body,
        grid=grid,
        in_specs=[pl.BlockSpec(memory_space=pltpu.SMEM), rows, lanes, rows],
        out_specs=rows,
        out_shape=jax.ShapeDtypeStruct(q.shape, BF16),
        scratch_shapes=[pltpu.VMEM((RET_HEADS, RET_DIM, RET_DIM), F32)],
        compiler_params=pltpu.CompilerParams(dimension_semantics=("arbitrary", "arbitrary"),
                                             vmem_limit_bytes=32 * MIB),
        name="ret_bwd",
    )(decay_logit, q, kt, v)


def _mixer_out(decay_logit, q, kt, v, cb, sg, sgb, mixa, x1, w_b, w_o, norm_w, w_gate, w_up, w_down, final_w,
               batch, seq):
    grid, rows, lanes = _ret_specs(batch, seq, reverse=False)
    in_specs = ([pl.BlockSpec(memory_space=pltpu.SMEM), rows, lanes] + [rows] * 6
                + [_resident((D_MODEL, D_MODEL)), _resident((D_MODEL, D_MODEL))] + _ffn_weight_specs())
    args = [decay_logit, q, kt, v, cb, sg, sgb, mixa, x1, w_b, w_o, norm_w, w_gate, w_up, w_down]
    if final_w is not None:
        in_specs.append(_resident((1, D_MODEL)))
        args.append(final_w)
    return pl.pallas_call(
        functools.partial(_mixer_out_body, final_norm=final_w is not None),
        grid=grid,
        in_specs=in_specs,
        out_specs=rows,
        out_shape=jax.ShapeDtypeStruct(q.shape, F32),
        scratch_shapes=[pltpu.VMEM((RET_HEADS, RET_DIM, RET_DIM), F32), pltpu.VMEM((RET_ROWS, D_MODEL), BF16),
                        pltpu.VMEM((RET_ROWS, D_MODEL), BF16), pltpu.VMEM((RET_ROWS, D_MODEL), F32),
                        pltpu.VMEM((RET_ROWS, D_FF), BF16)],
        compiler_params=pltpu.CompilerParams(dimension_semantics=("arbitrary", "arbitrary"),
                                             vmem_limit_bytes=60 * MIB),
        name="mixer_out",
    )(*args)


def kernel(x, ffn1_norm, ffn1_w_gate, ffn1_w_up, ffn1_w_down, mix_norm, w_in, b_in, sgu_norm_g, sgu_norm_b, sgu_w_s, sgu_b_s, ret_decay_logit, w_branch_a, w_branch_b, w_out, ffn2_norm, ffn2_w_gate, ffn2_w_up, ffn2_w_down, final_norm):
    batch, seq, _ = x.shape
    depth = ffn1_norm.shape[0]
    assert depth >= 1 and seq % RET_ROWS == 0 and seq % MIX_ROWS == 0 and (batch * seq) % FFN_ROWS == 0
    tokens = batch * seq
    row = lambda p: p.reshape(1, -1)
    theta = ROPE_BASE ** (-jnp.arange(0, RET_DIM, 2, dtype=F32) / RET_DIM)

    k_lo, k_hi = SEG_K * D_MODEL, (SEG_K + 1) * D_MODEL
    xt = x.reshape(tokens, D_MODEL)
    for l in range(depth):
        last = l == depth - 1
        x1, (w_in_bf, w_a, w_b, w_o, w_gate2, w_up2, w_down2), (cos, sin, cos_t, sin_t) = _ffn(
            xt, row(ffn1_norm[l]), ffn1_w_gate[l], ffn1_w_up[l], ffn1_w_down[l], theta,
            [w_in[l], w_branch_a[l], w_branch_b[l], w_out[l], ffn2_w_gate[l], ffn2_w_up[l], ffn2_w_down[l]], seq)
        mixa, q, kt, v, sg, sgb = _mixer_in(
            x1, row(mix_norm[l]), w_in_bf, row(b_in[l]),
            w_in[l][:, k_lo:k_hi], b_in[l][k_lo:k_hi].reshape(-1, 1),
            row(sgu_norm_g[l]), row(sgu_norm_b[l]), sgu_w_s[l].astype(BF16),
            sgu_b_s[l].reshape(SGU_GROUPS, CHUNK, 1), w_a, cos, sin, cos_t, sin_t, seq)
        cb = _ret_bwd(ret_decay_logit[l], q, kt, v, batch, seq)
        xt = _mixer_out(ret_decay_logit[l], q, kt, v, cb, sg, sgb, mixa, x1, w_b, w_o,
                        row(ffn2_norm[l]), w_gate2, w_up2, w_down2,
                        row(final_norm) if last else None, batch, seq)
    return xt.reshape(batch, seq, D_MODEL)
```

```python
import functools

import jax
import jax.numpy as jnp
import numpy as np
from jax import lax
from jax.experimental import pallas as pl
from jax.experimental.pallas import tpu as pltpu

D_MODEL = 1024
D_FF = 2816
CHUNK = 128
SGU_GROUPS = 4
SGU_GROUP_DIM = D_MODEL // SGU_GROUPS
RET_HEADS = 4
RET_DIM = 256
SEG_U, SEG_V, SEG_Q, SEG_K, SEG_VR, SEG_G, SEG_GATE_A, SEG_GATE_B = range(8)
FORWARD, BACKWARD = 0, 1
ROPE_BASE = 10000.0
NORM_EPS = 1e-6

MXU_COLS = 256
BF16_SUBLANES = 16
FFN_ROWS = 512
MIX_ROWS = 512
RET_ROWS = 512
RET_BLOCK = 256
MIB = 1024 * 1024

BF16 = jnp.bfloat16
F32 = jnp.float32


def _rms(x, g):
    return x * lax.rsqrt(jnp.mean(x * x, axis=-1, keepdims=True) + NORM_EPS) * g


def _gelu(x):
    return 0.5 * x * (1.0 + lax.erf(x * np.float32(np.sqrt(0.5))))


def _silu(x):
    return x * jax.nn.sigmoid(x)


def _dot(a, b):
    return jnp.dot(a, b, preferred_element_type=F32)


def _dot_nt(a, b):
    return lax.dot_general(a, b, (((1,), (1,)), ((), ())), preferred_element_type=F32)


def _col_chunks(width=D_MODEL):
    return [slice(c, c + MXU_COLS) for c in range(0, width, MXU_COLS)]


def _resident(shape):
    return pl.BlockSpec(shape, lambda *_: (0,) * len(shape), pipeline_mode=pl.Buffered(1))


def _swiglu_half_step(x, nw_ref, wg_ref, wu_ref, wd_ref, act_ref):
    h = _rms(x, nw_ref[...]).astype(BF16)
    for cols in _col_chunks(D_FF):
        g = _dot(h, wg_ref[:, cols].astype(BF16))
        u = _dot(h, wu_ref[:, cols].astype(BF16))
        act_ref[:, cols] = (_silu(g) * u).astype(BF16)
    act = act_ref[...]
    down = jnp.concatenate([_dot(act, wd_ref[:, cols].astype(BF16)) for cols in _col_chunks()], axis=-1)
    return x + 0.5 * down


def _ffn_body(*refs, n_cast, pos_rows):
    n = n_cast
    x_ref, nw_ref, wg_ref, wu_ref, wd_ref, theta_ref, theta_t_ref = refs[:7]
    srcs = refs[7:7 + n]
    o_ref = refs[7 + n]
    dsts = refs[8 + n:8 + 2 * n]
    cos_ref, sin_ref, cost_ref, sint_ref, act_ref = refs[8 + 2 * n:]
    step = pl.program_id(0)

    for src, dst in zip(srcs, dsts):
        dst[...] = src[...].astype(BF16)

    first = step * pos_rows
    pos = (first + lax.broadcasted_iota(jnp.int32, (pos_rows, RET_DIM // 2), 0)).astype(F32)
    ang = pos * theta_ref[...]
    cos_ref[...] = jnp.cos(ang)
    sin_ref[...] = jnp.sin(ang)
    pos_t = (first + lax.broadcasted_iota(jnp.int32, (RET_DIM // 2, pos_rows), 1)).astype(F32)
    ang_t = pos_t * theta_t_ref[...]
    cost_ref[...] = jnp.cos(ang_t)
    sint_ref[...] = jnp.sin(ang_t)

    o_ref[...] = _swiglu_half_step(x_ref[...], nw_ref, wg_ref, wu_ref, wd_ref, act_ref)


def _ffn_weight_specs():
    return [_resident((1, D_MODEL)), _resident((D_MODEL, D_FF)), _resident((D_MODEL, D_FF)),
            _resident((D_FF, D_MODEL))]


def _cast_rows(n_rows, n_steps):
    for rows in range(BF16_SUBLANES, n_rows + 1, BF16_SUBLANES):
        if n_rows % rows == 0 and n_rows // rows <= n_steps:
            return rows
    raise ValueError(f"cannot split {n_rows} rows over {n_steps} steps")


def _ffn(x, norm_w, w_gate, w_up, w_down, theta, later_weights, seq):
    t = x.shape[0]
    n_steps = t // FFN_ROWS
    pos_rows = seq // n_steps
    half = RET_DIM // 2
    assert seq % n_steps == 0 and pos_rows % 128 == 0
    rows = pl.BlockSpec((FFN_ROWS, D_MODEL), lambda i: (i, 0))

    cast_specs, cast_shapes = [], []
    for w in later_weights:
        r = _cast_rows(w.shape[0], n_steps)
        steps = w.shape[0] // r
        cast_specs.append(pl.BlockSpec((r, w.shape[1]), lambda i, last=steps - 1: (jnp.minimum(i, last), 0)))
        cast_shapes.append(jax.ShapeDtypeStruct(w.shape, BF16))
    table = pl.BlockSpec((pos_rows, half), lambda i: (i, 0))
    table_t = pl.BlockSpec((half, pos_rows), lambda i: (0, i))

    outs = pl.pallas_call(
        functools.partial(_ffn_body, n_cast=len(later_weights), pos_rows=pos_rows),
        grid=(n_steps,),
        in_specs=[rows] + _ffn_weight_specs() + [_resident((1, half)), _resident((half, 1))] + cast_specs,
        out_specs=[rows] + cast_specs + [table, table, table_t, table_t],
        out_shape=[jax.ShapeDtypeStruct((t, D_MODEL), F32)] + cast_shapes
                  + [jax.ShapeDtypeStruct((seq, half), F32)] * 2 + [jax.ShapeDtypeStruct((half, seq), F32)] * 2,
        scratch_shapes=[pltpu.VMEM((FFN_ROWS, D_FF), BF16)],
        compiler_params=pltpu.CompilerParams(dimension_semantics=("arbitrary",), vmem_limit_bytes=60 * MIB),
        name="ffn",
    )(x, norm_w, w_gate, w_up, w_down, theta.reshape(1, half), theta.reshape(half, 1), *later_weights)
    n = len(later_weights)
    return outs[0], outs[1:1 + n], outs[1 + n:]


def _log_gamma(logit, shape):
    return jax.nn.log_sigmoid(jnp.full(shape, logit, F32))


def _zero_state_at_sequence_start(st_ref):
    @pl.when(pl.program_id(1) == 0)
    def _():
        st_ref[...] = jnp.zeros_like(st_ref)


def _state_update(st_ref, hd, kt, v, k_scale, block_decay):
    kd = (kt.astype(F32) * k_scale).astype(BF16)
    st_ref[hd] = st_ref[hd] * block_decay + _dot(kd, v)


def _retention_backward_head(dl_ref, hd, q_ref, kt_ref, v_ref, o_ref, st_ref):
    cols = slice(hd * RET_DIM, (hd + 1) * RET_DIM)
    row_idx = lax.broadcasted_iota(jnp.int32, (RET_BLOCK, RET_DIM), 0).astype(F32)
    lane_idx = lax.broadcasted_iota(jnp.int32, (RET_DIM, RET_BLOCK), 1).astype(F32)
    logit = dl_ref[BACKWARD, hd]
    q_scale = jnp.exp((RET_BLOCK - row_idx) * _log_gamma(logit, (RET_BLOCK, RET_DIM)))
    k_scale = jnp.exp(lane_idx * _log_gamma(logit, (RET_DIM, RET_BLOCK)))
    block_decay = jnp.exp(RET_BLOCK * _log_gamma(logit, (1, RET_DIM)))
    for blk in reversed(range(q_ref.shape[0] // RET_BLOCK)):
        rows = slice(blk * RET_BLOCK, (blk + 1) * RET_BLOCK)
        o_ref[rows, cols] = (_dot(q_ref[rows, cols], st_ref[hd].astype(BF16)) * q_scale).astype(BF16)
        _state_update(st_ref, hd, kt_ref[cols, rows], v_ref[rows, cols], k_scale, block_decay)


def _mixer_in_body(dl_ref, x_ref, nw_ref, w_ref, b_ref, wk_ref, bkt_ref, lng_ref, lnb_ref, ws_ref, bs_ref, wa_ref,
                   cos_ref, sin_ref, cost_ref, sint_ref,
                   mixa_ref, q_ref, kt_ref, v_ref, sg_ref, sgb_ref, cb_ref, wkt_ref, va_ref, vn_ref, a_ref, st_ref):
    _zero_state_at_sequence_start(st_ref)

    @pl.when((pl.program_id(0) == 0) & (pl.program_id(1) == 0))
    def _():
        for r in _col_chunks():
            for c in _col_chunks():
                wkt_ref[c, r] = wk_ref[r, c].T.astype(BF16)

    h = _rms(x_ref[...], nw_ref[...]).astype(BF16)
    half = RET_DIM // 2

    def proj(seg, cols):
        wcols = slice(seg * D_MODEL + cols.start, seg * D_MODEL + cols.stop)
        return _dot(h, w_ref[:, wcols]) + b_ref[:, wcols]

    for cols in _col_chunks():
        va_ref[:, cols] = _gelu(proj(SEG_V, cols))
    va = va_ref[...]
    mu = jnp.mean(va, axis=-1, keepdims=True)
    vc = va - mu
    var = jnp.mean(vc * vc, axis=-1, keepdims=True)
    vn_ref[...] = (vc * lax.rsqrt(var + NORM_EPS) * lng_ref[...] + lnb_ref[...]).astype(BF16)
    for g, cols in enumerate(_col_chunks()):
        u = _gelu(proj(SEG_U, cols))
        for j in range(MIX_ROWS // CHUNK):
            rows = slice(j * CHUNK, (j + 1) * CHUNK)
            s = _dot(ws_ref[g], vn_ref[rows, cols]) + bs_ref[g]
            a_ref[rows, cols] = (u[rows] * s).astype(BF16)
    a = a_ref[...]
    for cols in _col_chunks():
        mixa_ref[:, cols] = (jax.nn.sigmoid(proj(SEG_GATE_A, cols)) * _dot(a, wa_ref[:, cols])).astype(BF16)

    cos, sin = cos_ref[...], sin_ref[...]
    cost, sint = cost_ref[...], sint_ref[...]
    k_scale = RET_DIM ** -0.5
    for hd, cols in enumerate(_col_chunks()):
        t = proj(SEG_Q, cols)
        t1, t2 = t[:, :half], t[:, half:]
        q_ref[:, cols.start:cols.start + half] = (t1 * cos - t2 * sin).astype(BF16)
        q_ref[:, cols.start + half:cols.stop] = (t2 * cos + t1 * sin).astype(BF16)
        tt = _dot_nt(wkt_ref[cols, :], h) + bkt_ref[cols, :]
        t1, t2 = tt[:half], tt[half:]
        kt_ref[cols.start:cols.start + half, :] = ((t1 * cost - t2 * sint) * k_scale).astype(BF16)
        kt_ref[cols.start + half:cols.stop, :] = ((t2 * cost + t1 * sint) * k_scale).astype(BF16)
        v_ref[:, cols] = proj(SEG_VR, cols).astype(BF16)
        sg_ref[:, cols] = _silu(proj(SEG_G, cols)).astype(BF16)
        sgb_ref[:, cols] = jax.nn.sigmoid(proj(SEG_GATE_B, cols)).astype(BF16)
        _retention_backward_head(dl_ref, hd, q_ref, kt_ref, v_ref, cb_ref, st_ref)


def _mixer_in(decay_logit, x1, norm_w, w_in, b_in, w_k, b_kt, ln_g, ln_b, w_s, b_s, w_a, cos, sin, cos_t, sin_t,
              batch, seq):
    t = x1.shape[0]
    half = RET_DIM // 2
    steps = seq // MIX_ROWS
    tile = lambda b, j: b * steps + steps - 1 - j
    rows = pl.BlockSpec((MIX_ROWS, D_MODEL), lambda b, j: (tile(b, j), 0))
    lanes = pl.BlockSpec((D_MODEL, MIX_ROWS), lambda b, j: (0, tile(b, j)))
    pos = pl.BlockSpec((MIX_ROWS, half), lambda b, j: (steps - 1 - j, 0))
    pos_t = pl.BlockSpec((half, MIX_ROWS), lambda b, j: (0, steps - 1 - j))
    width = w_in.shape[1]
    out = jax.ShapeDtypeStruct((t, D_MODEL), BF16)
    out_t = jax.ShapeDtypeStruct((D_MODEL, t), BF16)
    return pl.pallas_call(
        _mixer_in_body,
        grid=(batch, steps),
        in_specs=[pl.BlockSpec(memory_space=pltpu.SMEM),
                  rows, _resident((1, D_MODEL)), _resident((D_MODEL, width)), _resident((1, width)),
                  _resident((D_MODEL, D_MODEL)), _resident((D_MODEL, 1)),
                  _resident((1, D_MODEL)), _resident((1, D_MODEL)), _resident((SGU_GROUPS, CHUNK, CHUNK)),
                  _resident((SGU_GROUPS, CHUNK, 1)), _resident((D_MODEL, D_MODEL)), pos, pos, pos_t, pos_t],
        out_specs=[rows, rows, lanes, rows, rows, rows, rows],
        out_shape=[out, out, out_t, out, out, out, out],
        scratch_shapes=[pltpu.VMEM((D_MODEL, D_MODEL), BF16), pltpu.VMEM((MIX_ROWS, D_MODEL), F32),
                        pltpu.VMEM((MIX_ROWS, D_MODEL), BF16), pltpu.VMEM((MIX_ROWS, D_MODEL), BF16),
                        pltpu.VMEM((RET_HEADS, RET_DIM, RET_DIM), F32)],
        compiler_params=pltpu.CompilerParams(dimension_semantics=("arbitrary", "arbitrary"),
                                             vmem_limit_bytes=60 * MIB),
        name="mixer_in",
    )(decay_logit, x1, norm_w, w_in, b_in, w_k, b_kt, ln_g, ln_b, w_s, b_s, w_a, cos, sin, cos_t, sin_t)


def _mixer_out_body(*refs, final_norm):
    (dl_ref, q_ref, kt_ref, v_ref, cb_ref, sg_ref, sgb_ref, mixa_ref, x1_ref, wb_ref, wo_ref,
     nw_ref, wg_ref, wu_ref, wd_ref) = refs[:15]
    fn_ref = refs[15] if final_norm else None
    o_ref, st_ref, r_ref, mix_ref, x2_ref, act_ref = refs[-6:]
    _zero_state_at_sequence_start(st_ref)
    row_idx = lax.broadcasted_iota(jnp.int32, (RET_BLOCK, RET_DIM), 0).astype(F32)
    lane_idx = lax.broadcasted_iota(jnp.int32, (RET_DIM, RET_BLOCK), 1).astype(F32)
    diff = (lax.broadcasted_iota(jnp.int32, (RET_BLOCK, RET_BLOCK), 0)
            - lax.broadcasted_iota(jnp.int32, (RET_BLOCK, RET_BLOCK), 1)).astype(F32)
    for hd in range(RET_HEADS):
        cols = slice(hd * RET_DIM, (hd + 1) * RET_DIM)
        logit = dl_ref[FORWARD, hd]
        q_scale = jnp.exp((row_idx + 1.0) * _log_gamma(logit, (RET_BLOCK, RET_DIM)))
        k_scale = jnp.exp((RET_BLOCK - 1.0 - lane_idx) * _log_gamma(logit, (RET_DIM, RET_BLOCK)))
        block_decay = jnp.exp(RET_BLOCK * _log_gamma(logit, (1, RET_DIM)))
        lg_f = _log_gamma(logit, (RET_BLOCK, RET_BLOCK))
        lg_b = _log_gamma(dl_ref[BACKWARD, hd], (RET_BLOCK, RET_BLOCK))
        decay = jnp.exp(jnp.where(diff >= 0, diff * lg_f, -diff * lg_b))
        for blk in range(RET_ROWS // RET_BLOCK):
            rows = slice(blk * RET_BLOCK, (blk + 1) * RET_BLOCK)
            q = q_ref[rows, cols]
            kt = kt_ref[cols, rows]
            v = v_ref[rows, cols]
            r = (_dot((_dot(q, kt) * decay).astype(BF16), v)
                 + _dot(q, st_ref[hd].astype(BF16)) * q_scale
                 + cb_ref[rows, cols].astype(F32))
            r = r * lax.rsqrt(jnp.mean(r * r, axis=-1, keepdims=True) + NORM_EPS)
            r_ref[rows, cols] = (r * sg_ref[rows, cols].astype(F32)).astype(BF16)
            _state_update(st_ref, hd, kt, v, k_scale, block_decay)
    r = r_ref[...]
    for cols in _col_chunks():
        mix_ref[:, cols] = (sgb_ref[:, cols].astype(F32) * _dot(r, wb_ref[:, cols])
                            + mixa_ref[:, cols].astype(F32)).astype(BF16)
    mix = mix_ref[...]
    for cols in _col_chunks():
        x2_ref[:, cols] = x1_ref[:, cols] + _dot(mix, wo_ref[:, cols])
    y = _swiglu_half_step(x2_ref[...], nw_ref, wg_ref, wu_ref, wd_ref, act_ref)
    if final_norm:
        y = _rms(y, fn_ref[...])
    o_ref[...] = y


def _ret_specs(batch, seq):
    steps = seq // RET_ROWS
    rows = pl.BlockSpec((RET_ROWS, D_MODEL), lambda b, j: (b * steps + j, 0))
    lanes = pl.BlockSpec((D_MODEL, RET_ROWS), lambda b, j: (0, b * steps + j))
    return (batch, steps), rows, lanes


def _mixer_out(decay_logit, q, kt, v, cb, sg, sgb, mixa, x1, w_b, w_o, norm_w, w_gate, w_up, w_down, final_w,
               batch, seq):
    grid, rows, lanes = _ret_specs(batch, seq)
    in_specs = ([pl.BlockSpec(memory_space=pltpu.SMEM), rows, lanes] + [rows] * 6
                + [_resident((D_MODEL, D_MODEL)), _resident((D_MODEL, D_MODEL))] + _ffn_weight_specs())
    args = [decay_logit, q, kt, v, cb, sg, sgb, mixa, x1, w_b, w_o, norm_w, w_gate, w_up, w_down]
    if final_w is not None:
        in_specs.append(_resident((1, D_MODEL)))
        args.append(final_w)
    return pl.pallas_call(
        functools.partial(_mixer_out_body, final_norm=final_w is not None),
        grid=grid,
        in_specs=in_specs,
        out_specs=rows,
        out_shape=jax.ShapeDtypeStruct(q.shape, F32),
        scratch_shapes=[pltpu.VMEM((RET_HEADS, RET_DIM, RET_DIM), F32), pltpu.VMEM((RET_ROWS, D_MODEL), BF16),
                        pltpu.VMEM((RET_ROWS, D_MODEL), BF16), pltpu.VMEM((RET_ROWS, D_MODEL), F32),
                        pltpu.VMEM((RET_ROWS, D_FF), BF16)],
        compiler_params=pltpu.CompilerParams(dimension_semantics=("arbitrary", "arbitrary"),
                                             vmem_limit_bytes=60 * MIB),
        name="mixer_out",
    )(*args)


def kernel(x, ffn1_norm, ffn1_w_gate, ffn1_w_up, ffn1_w_down, mix_norm, w_in, b_in, sgu_norm_g, sgu_norm_b, sgu_w_s, sgu_b_s, ret_decay_logit, w_branch_a, w_branch_b, w_out, ffn2_norm, ffn2_w_gate, ffn2_w_up, ffn2_w_down, final_norm):
    batch, seq, _ = x.shape
    depth = ffn1_norm.shape[0]
    assert depth >= 1 and seq % RET_ROWS == 0 and seq % MIX_ROWS == 0 and (batch * seq) % FFN_ROWS == 0
    tokens = batch * seq
    row = lambda p: p.reshape(1, -1)
    theta = ROPE_BASE ** (-jnp.arange(0, RET_DIM, 2, dtype=F32) / RET_DIM)

    k_lo, k_hi = SEG_K * D_MODEL, (SEG_K + 1) * D_MODEL
    xt = x.reshape(tokens, D_MODEL)
    for l in range(depth):
        last = l == depth - 1
        x1, (w_in_bf, w_a, w_b, w_o, w_gate2, w_up2, w_down2), (cos, sin, cos_t, sin_t) = _ffn(
            xt, row(ffn1_norm[l]), ffn1_w_gate[l], ffn1_w_up[l], ffn1_w_down[l], theta,
            [w_in[l], w_branch_a[l], w_branch_b[l], w_out[l], ffn2_w_gate[l], ffn2_w_up[l], ffn2_w_down[l]], seq)
        mixa, q, kt, v, sg, sgb, cb = _mixer_in(
            ret_decay_logit[l], x1, row(mix_norm[l]), w_in_bf, row(b_in[l]),
            w_in[l][:, k_lo:k_hi], b_in[l][k_lo:k_hi].reshape(-1, 1),
            row(sgu_norm_g[l]), row(sgu_norm_b[l]), sgu_w_s[l].astype(BF16),
            sgu_b_s[l].reshape(SGU_GROUPS, CHUNK, 1), w_a, cos, sin, cos_t, sin_t, batch, seq)
        xt = _mixer_out(ret_decay_logit[l], q, kt, v, cb, sg, sgb, mixa, x1, w_b, w_o,
                        row(ffn2_norm[l]), w_gate2, w_up2, w_down2,
                        row(final_norm) if last else None, batch, seq)
    return xt.reshape(batch, seq, D_MODEL)
```

```python
import functools

import jax
import jax.numpy as jnp
import numpy as np
from jax import lax
from jax.experimental import pallas as pl
from jax.experimental.pallas import tpu as pltpu

D_MODEL = 1024
D_FF = 2816
CHUNK = 128
SGU_GROUPS = 4
SGU_GROUP_DIM = D_MODEL // SGU_GROUPS
RET_HEADS = 4
RET_DIM = 256
SEG_U, SEG_V, SEG_Q, SEG_K, SEG_VR, SEG_G, SEG_GATE_A, SEG_GATE_B = range(8)
FORWARD, BACKWARD = 0, 1
ROPE_BASE = 10000.0
NORM_EPS = 1e-6

MXU_COLS = 256
BF16_SUBLANES = 16
FFN_ROWS = 512
MIX_ROWS = 512
RET_ROWS = 512
RET_BLOCK = 256
MIB = 1024 * 1024

BF16 = jnp.bfloat16
F32 = jnp.float32


def _rms(x, g):
    return x * lax.rsqrt(jnp.mean(x * x, axis=-1, keepdims=True) + NORM_EPS) * g


def _gelu(x):
    return 0.5 * x * (1.0 + lax.erf(x * np.float32(np.sqrt(0.5))))


def _silu(x):
    return x * jax.nn.sigmoid(x)


def _dot(a, b):
    return jnp.dot(a, b, preferred_element_type=F32)


def _dot_nt(a, b):
    return lax.dot_general(a, b, (((1,), (1,)), ((), ())), preferred_element_type=F32)


def _col_chunks(width=D_MODEL):
    return [slice(c, c + MXU_COLS) for c in range(0, width, MXU_COLS)]


def _resident(shape):
    return pl.BlockSpec(shape, lambda *_: (0,) * len(shape), pipeline_mode=pl.Buffered(1))


def _swiglu_half_step(x, nw_ref, wg_ref, wu_ref, wd_ref, act_ref):
    h = _rms(x, nw_ref[...]).astype(BF16)
    for cols in _col_chunks(D_FF):
        g = _dot(h, wg_ref[:, cols].astype(BF16))
        u = _dot(h, wu_ref[:, cols].astype(BF16))
        act_ref[:, cols] = (_silu(g) * u).astype(BF16)
    act = act_ref[...]
    down = jnp.concatenate([_dot(act, wd_ref[:, cols].astype(BF16)) for cols in _col_chunks()], axis=-1)
    return x + 0.5 * down


def _ffn_body(*refs, n_cast, pos_rows):
    n = n_cast
    x_ref, nw_ref, wg_ref, wu_ref, wd_ref, theta_ref = refs[:6]
    srcs = refs[6:6 + n]
    o_ref = refs[6 + n]
    dsts = refs[7 + n:7 + 2 * n]
    cos_ref, sin_ref, cost_ref, sint_ref, act_ref, cos0_ref, sin0_ref = refs[7 + 2 * n:]
    step = pl.program_id(0)
    theta = theta_ref[...]

    @pl.when(step == 0)
    def _():
        ang0 = lax.broadcasted_iota(jnp.int32, (pos_rows, RET_DIM // 2), 0).astype(F32) * theta
        cos0_ref[...] = jnp.cos(ang0)
        sin0_ref[...] = jnp.sin(ang0)

    for src, dst in zip(srcs, dsts):
        dst[...] = src[...].astype(BF16)

    ang1 = (step * pos_rows).astype(F32) * theta
    cos1, sin1 = jnp.cos(ang1), jnp.sin(ang1)
    cos0, sin0 = cos0_ref[...], sin0_ref[...]
    cos = cos1 * cos0 - sin1 * sin0
    sin = sin1 * cos0 + cos1 * sin0
    cos_ref[...] = cos
    sin_ref[...] = sin
    cost_ref[...] = cos.T
    sint_ref[...] = sin.T

    o_ref[...] = _swiglu_half_step(x_ref[...], nw_ref, wg_ref, wu_ref, wd_ref, act_ref)


def _ffn_weight_specs():
    return [_resident((1, D_MODEL)), _resident((D_MODEL, D_FF)), _resident((D_MODEL, D_FF)),
            _resident((D_FF, D_MODEL))]


def _cast_rows(n_rows, n_steps):
    for rows in range(BF16_SUBLANES, n_rows + 1, BF16_SUBLANES):
        if n_rows % rows == 0 and n_rows // rows <= n_steps:
            return rows
    raise ValueError(f"cannot split {n_rows} rows over {n_steps} steps")


def _ffn(x, norm_w, w_gate, w_up, w_down, theta, later_weights, seq):
    t = x.shape[0]
    n_steps = t // FFN_ROWS
    pos_rows = seq // n_steps
    half = RET_DIM // 2
    assert seq % n_steps == 0 and pos_rows % 128 == 0
    rows = pl.BlockSpec((FFN_ROWS, D_MODEL), lambda i: (i, 0))

    cast_specs, cast_shapes = [], []
    for w in later_weights:
        r = _cast_rows(w.shape[0], n_steps)
        steps = w.shape[0] // r
        cast_specs.append(pl.BlockSpec((r, w.shape[1]), lambda i, last=steps - 1: (jnp.minimum(i, last), 0)))
        cast_shapes.append(jax.ShapeDtypeStruct(w.shape, BF16))
    table = pl.BlockSpec((pos_rows, half), lambda i: (i, 0))
    table_t = pl.BlockSpec((half, pos_rows), lambda i: (0, i))

    outs = pl.pallas_call(
        functools.partial(_ffn_body, n_cast=len(later_weights), pos_rows=pos_rows),
        grid=(n_steps,),
        in_specs=[rows] + _ffn_weight_specs() + [_resident((1, half))] + cast_specs,
        out_specs=[rows] + cast_specs + [table, table, table_t, table_t],
        out_shape=[jax.ShapeDtypeStruct((t, D_MODEL), F32)] + cast_shapes
                  + [jax.ShapeDtypeStruct((seq, half), F32)] * 2 + [jax.ShapeDtypeStruct((half, seq), F32)] * 2,
        scratch_shapes=[pltpu.VMEM((FFN_ROWS, D_FF), BF16), pltpu.VMEM((pos_rows, half), F32),
                        pltpu.VMEM((pos_rows, half), F32)],
        compiler_params=pltpu.CompilerParams(dimension_semantics=("arbitrary",), vmem_limit_bytes=60 * MIB),
        name="ffn",
    )(x, norm_w, w_gate, w_up, w_down, theta.reshape(1, half), *later_weights)
    n = len(later_weights)
    return outs[0], outs[1:1 + n], outs[1 + n:]


def _log_gamma(logit, shape):
    return jax.nn.log_sigmoid(jnp.full(shape, logit, F32))


def _zero_state_at_sequence_start(st_ref):
    @pl.when(pl.program_id(1) == 0)
    def _():
        st_ref[...] = jnp.zeros_like(st_ref)


def _state_update(st_ref, hd, kt, v, k_scale, block_decay):
    kd = (kt.astype(F32) * k_scale).astype(BF16)
    st_ref[hd] = st_ref[hd] * block_decay + _dot(kd, v)


def _retention_backward_head(dl_ref, hd, q_ref, kt_ref, v_ref, o_ref, st_ref):
    cols = slice(hd * RET_DIM, (hd + 1) * RET_DIM)
    row_idx = lax.broadcasted_iota(jnp.int32, (RET_BLOCK, RET_DIM), 0).astype(F32)
    lane_idx = lax.broadcasted_iota(jnp.int32, (RET_DIM, RET_BLOCK), 1).astype(F32)
    logit = dl_ref[BACKWARD, hd]
    q_scale = jnp.exp((RET_BLOCK - row_idx) * _log_gamma(logit, (RET_BLOCK, RET_DIM)))
    k_scale = jnp.exp(lane_idx * _log_gamma(logit, (RET_DIM, RET_BLOCK)))
    block_decay = jnp.exp(RET_BLOCK * _log_gamma(logit, (1, RET_DIM)))
    for blk in reversed(range(q_ref.shape[0] // RET_BLOCK)):
        rows = slice(blk * RET_BLOCK, (blk + 1) * RET_BLOCK)
        o_ref[rows, cols] = (_dot(q_ref[rows, cols], st_ref[hd].astype(BF16)) * q_scale).astype(BF16)
        _state_update(st_ref, hd, kt_ref[cols, rows], v_ref[rows, cols], k_scale, block_decay)


def _mixer_in_body(dl_ref, x_ref, nw_ref, w_ref, b_ref, wk_ref, bkt_ref, lng_ref, lnb_ref, ws_ref, bs_ref, wa_ref,
                   cos_ref, sin_ref, cost_ref, sint_ref,
                   mixa_ref, q_ref, kt_ref, v_ref, sg_ref, sgb_ref, cb_ref, wkt_ref, va_ref, vn_ref, a_ref, st_ref):
    _zero_state_at_sequence_start(st_ref)

    @pl.when((pl.program_id(0) == 0) & (pl.program_id(1) == 0))
    def _():
        for r in _col_chunks():
            for c in _col_chunks():
                wkt_ref[c, r] = wk_ref[r, c].T.astype(BF16)

    h = _rms(x_ref[...], nw_ref[...]).astype(BF16)
    half = RET_DIM // 2

    def proj(seg, cols):
        wcols = slice(seg * D_MODEL + cols.start, seg * D_MODEL + cols.stop)
        return _dot(h, w_ref[:, wcols]) + b_ref[:, wcols]

    for cols in _col_chunks():
        va_ref[:, cols] = _gelu(proj(SEG_V, cols))
    va = va_ref[...]
    mu = jnp.mean(va, axis=-1, keepdims=True)
    vc = va - mu
    var = jnp.mean(vc * vc, axis=-1, keepdims=True)
    vn_ref[...] = (vc * lax.rsqrt(var + NORM_EPS) * lng_ref[...] + lnb_ref[...]).astype(BF16)
    for g, cols in enumerate(_col_chunks()):
        u = _gelu(proj(SEG_U, cols))
        for j in range(MIX_ROWS // CHUNK):
            rows = slice(j * CHUNK, (j + 1) * CHUNK)
            s = _dot(ws_ref[g], vn_ref[rows, cols]) + bs_ref[g]
            a_ref[rows, cols] = (u[rows] * s).astype(BF16)
    a = a_ref[...]
    for cols in _col_chunks():
        mixa_ref[:, cols] = (jax.nn.sigmoid(proj(SEG_GATE_A, cols)) * _dot(a, wa_ref[:, cols])).astype(BF16)

    cos, sin = cos_ref[...], sin_ref[...]
    cost, sint = cost_ref[...], sint_ref[...]
    k_scale = RET_DIM ** -0.5
    for hd, cols in enumerate(_col_chunks()):
        t = proj(SEG_Q, cols)
        t1, t2 = t[:, :half], t[:, half:]
        q_ref[:, cols.start:cols.start + half] = (t1 * cos - t2 * sin).astype(BF16)
        q_ref[:, cols.start + half:cols.stop] = (t2 * cos + t1 * sin).astype(BF16)
        tt = _dot_nt(wkt_ref[cols, :], h) + bkt_ref[cols, :]
        t1, t2 = tt[:half], tt[half:]
        kt_ref[cols.start:cols.start + half, :] = ((t1 * cost - t2 * sint) * k_scale).astype(BF16)
        kt_ref[cols.start + half:cols.stop, :] = ((t2 * cost + t1 * sint) * k_scale).astype(BF16)
        v_ref[:, cols] = proj(SEG_VR, cols).astype(BF16)
        sg_ref[:, cols] = _silu(proj(SEG_G, cols)).astype(BF16)
        sgb_ref[:, cols] = jax.nn.sigmoid(proj(SEG_GATE_B, cols)).astype(BF16)
        _retention_backward_head(dl_ref, hd, q_ref, kt_ref, v_ref, cb_ref, st_ref)


def _mixer_in(decay_logit, x1, norm_w, w_in, b_in, w_k, b_kt, ln_g, ln_b, w_s, b_s, w_a, cos, sin, cos_t, sin_t,
              batch, seq):
    t = x1.shape[0]
    half = RET_DIM // 2
    steps = seq // MIX_ROWS
    tile = lambda b, j: b * steps + steps - 1 - j
    rows = pl.BlockSpec((MIX_ROWS, D_MODEL), lambda b, j: (tile(b, j), 0))
    lanes = pl.BlockSpec((D_MODEL, MIX_ROWS), lambda b, j: (0, tile(b, j)))
    pos = pl.BlockSpec((MIX_ROWS, half), lambda b, j: (steps - 1 - j, 0))
    pos_t = pl.BlockSpec((half, MIX_ROWS), lambda b, j: (0, steps - 1 - j))
    width = w_in.shape[1]
    out = jax.ShapeDtypeStruct((t, D_MODEL), BF16)
    out_t = jax.ShapeDtypeStruct((D_MODEL, t), BF16)
    return pl.pallas_call(
        _mixer_in_body,
        grid=(batch, steps),
        in_specs=[pl.BlockSpec(memory_space=pltpu.SMEM),
                  rows, _resident((1, D_MODEL)), _resident((D_MODEL, width)), _resident((1, width)),
                  pl.BlockSpec((D_MODEL, D_MODEL), lambda *_: (0, SEG_K), pipeline_mode=pl.Buffered(1)),
                  _resident((D_MODEL, 1)),
                  _resident((1, D_MODEL)), _resident((1, D_MODEL)), _resident((SGU_GROUPS, CHUNK, CHUNK)),
                  _resident((SGU_GROUPS, CHUNK, 1)), _resident((D_MODEL, D_MODEL)), pos, pos, pos_t, pos_t],
        out_specs=[rows, rows, lanes, rows, rows, rows, rows],
        out_shape=[out, out, out_t, out, out, out, out],
        scratch_shapes=[pltpu.VMEM((D_MODEL, D_MODEL), BF16), pltpu.VMEM((MIX_ROWS, D_MODEL), F32),
                        pltpu.VMEM((MIX_ROWS, D_MODEL), BF16), pltpu.VMEM((MIX_ROWS, D_MODEL), BF16),
                        pltpu.VMEM((RET_HEADS, RET_DIM, RET_DIM), F32)],
        compiler_params=pltpu.CompilerParams(dimension_semantics=("arbitrary", "arbitrary"),
                                             vmem_limit_bytes=60 * MIB),
        name="mixer_in",
    )(decay_logit, x1, norm_w, w_in, b_in, w_k, b_kt, ln_g, ln_b, w_s, b_s, w_a, cos, sin, cos_t, sin_t)


def _mixer_out_body(*refs, final_norm):
    (dl_ref, q_ref, kt_ref, v_ref, cb_ref, sg_ref, sgb_ref, mixa_ref, x1_ref, wb_ref, wo_ref,
     nw_ref, wg_ref, wu_ref, wd_ref) = refs[:15]
    fn_ref = refs[15] if final_norm else None
    o_ref, st_ref, r_ref, mix_ref, x2_ref, act_ref = refs[-6:]
    _zero_state_at_sequence_start(st_ref)
    row_idx = lax.broadcasted_iota(jnp.int32, (RET_BLOCK, RET_DIM), 0).astype(F32)
    lane_idx = lax.broadcasted_iota(jnp.int32, (RET_DIM, RET_BLOCK), 1).astype(F32)
    diff = (lax.broadcasted_iota(jnp.int32, (RET_BLOCK, RET_BLOCK), 0)
            - lax.broadcasted_iota(jnp.int32, (RET_BLOCK, RET_BLOCK), 1)).astype(F32)
    for hd in range(RET_HEADS):
        cols = slice(hd * RET_DIM, (hd + 1) * RET_DIM)
        logit = dl_ref[FORWARD, hd]
        q_scale = jnp.exp((row_idx + 1.0) * _log_gamma(logit, (RET_BLOCK, RET_DIM)))
        k_scale = jnp.exp((RET_BLOCK - 1.0 - lane_idx) * _log_gamma(logit, (RET_DIM, RET_BLOCK)))
        block_decay = jnp.exp(RET_BLOCK * _log_gamma(logit, (1, RET_DIM)))
        lg_f = _log_gamma(logit, (RET_BLOCK, RET_BLOCK))
        lg_b = _log_gamma(dl_ref[BACKWARD, hd], (RET_BLOCK, RET_BLOCK))
        decay = jnp.exp(jnp.where(diff >= 0, diff * lg_f, -diff * lg_b))
        for blk in range(RET_ROWS // RET_BLOCK):
            rows = slice(blk * RET_BLOCK, (blk + 1) * RET_BLOCK)
            q = q_ref[rows, cols]
            kt = kt_ref[cols, rows]
            v = v_ref[rows, cols]
            r = (_dot((_dot(q, kt) * decay).astype(BF16), v)
                 + _dot(q, st_ref[hd].astype(BF16)) * q_scale
                 + cb_ref[rows, cols].astype(F32))
            r = r * lax.rsqrt(jnp.mean(r * r, axis=-1, keepdims=True) + NORM_EPS)
            r_ref[rows, cols] = (r * sg_ref[rows, cols].astype(F32)).astype(BF16)
            _state_update(st_ref, hd, kt, v, k_scale, block_decay)
    r = r_ref[...]
    for cols in _col_chunks():
        mix_ref[:, cols] = (sgb_ref[:, cols].astype(F32) * _dot(r, wb_ref[:, cols])
                            + mixa_ref[:, cols].astype(F32)).astype(BF16)
    mix = mix_ref[...]
    for cols in _col_chunks():
        x2_ref[:, cols] = x1_ref[:, cols] + _dot(mix, wo_ref[:, cols])
    y = _swiglu_half_step(x2_ref[...], nw_ref, wg_ref, wu_ref, wd_ref, act_ref)
    if final_norm:
        y = _rms(y, fn_ref[...])
    o_ref[...] = y


def _ret_specs(batch, seq):
    steps = seq // RET_ROWS
    rows = pl.BlockSpec((RET_ROWS, D_MODEL), lambda b, j: (b * steps + j, 0))
    lanes = pl.BlockSpec((D_MODEL, RET_ROWS), lambda b, j: (0, b * steps + j))
    return (batch, steps), rows, lanes


def _mixer_out(decay_logit, q, kt, v, cb, sg, sgb, mixa, x1, w_b, w_o, norm_w, w_gate, w_up, w_down, final_w,
               batch, seq):
    grid, rows, lanes = _ret_specs(batch, seq)
    in_specs = ([pl.BlockSpec(memory_space=pltpu.SMEM), rows, lanes] + [rows] * 6
                + [_resident((D_MODEL, D_MODEL)), _resident((D_MODEL, D_MODEL))] + _ffn_weight_specs())
    args = [decay_logit, q, kt, v, cb, sg, sgb, mixa, x1, w_b, w_o, norm_w, w_gate, w_up, w_down]
    if final_w is not None:
        in_specs.append(_resident((1, D_MODEL)))
        args.append(final_w)
    return pl.pallas_call(
        functools.partial(_mixer_out_body, final_norm=final_w is not None),
        grid=grid,
        in_specs=in_specs,
        out_specs=rows,
        out_shape=jax.ShapeDtypeStruct(q.shape, F32),
        scratch_shapes=[pltpu.VMEM((RET_HEADS, RET_DIM, RET_DIM), F32), pltpu.VMEM((RET_ROWS, D_MODEL), BF16),
                        pltpu.VMEM((RET_ROWS, D_MODEL), BF16), pltpu.VMEM((RET_ROWS, D_MODEL), F32),
                        pltpu.VMEM((RET_ROWS, D_FF), BF16)],
        compiler_params=pltpu.CompilerParams(dimension_semantics=("arbitrary", "arbitrary"),
                                             vmem_limit_bytes=60 * MIB),
        name="mixer_out",
    )(*args)


def kernel(x, ffn1_norm, ffn1_w_gate, ffn1_w_up, ffn1_w_down, mix_norm, w_in, b_in, sgu_norm_g, sgu_norm_b, sgu_w_s, sgu_b_s, ret_decay_logit, w_branch_a, w_branch_b, w_out, ffn2_norm, ffn2_w_gate, ffn2_w_up, ffn2_w_down, final_norm):
    batch, seq, _ = x.shape
    depth = ffn1_norm.shape[0]
    assert depth >= 1 and seq % RET_ROWS == 0 and seq % MIX_ROWS == 0 and (batch * seq) % FFN_ROWS == 0
    tokens = batch * seq
    row = lambda p: p.reshape(1, -1)
    theta = ROPE_BASE ** (-jnp.arange(0, RET_DIM, 2, dtype=F32) / RET_DIM)

    k_lo, k_hi = SEG_K * D_MODEL, (SEG_K + 1) * D_MODEL
    xt = x.reshape(tokens, D_MODEL)
    for l in range(depth):
        last = l == depth - 1
        x1, (w_in_bf, w_a, w_b, w_o, w_gate2, w_up2, w_down2), (cos, sin, cos_t, sin_t) = _ffn(
            xt, row(ffn1_norm[l]), ffn1_w_gate[l], ffn1_w_up[l], ffn1_w_down[l], theta,
            [w_in[l], w_branch_a[l], w_branch_b[l], w_out[l], ffn2_w_gate[l], ffn2_w_up[l], ffn2_w_down[l]], seq)
        mixa, q, kt, v, sg, sgb, cb = _mixer_in(
            ret_decay_logit[l], x1, row(mix_norm[l]), w_in_bf, row(b_in[l]),
            w_in[l], b_in[l][k_lo:k_hi].reshape(-1, 1),
            row(sgu_norm_g[l]), row(sgu_norm_b[l]), sgu_w_s[l].astype(BF16),
            sgu_b_s[l].reshape(SGU_GROUPS, CHUNK, 1), w_a, cos, sin, cos_t, sin_t, batch, seq)
        xt = _mixer_out(ret_decay_logit[l], q, kt, v, cb, sg, sgb, mixa, x1, w_b, w_o,
                        row(ffn2_norm[l]), w_gate2, w_up2, w_down2,
                        row(final_norm) if last else None, batch, seq)
    return xt.reshape(batch, seq, D_MODEL)
```

```python
import functools

import jax
import jax.numpy as jnp
import numpy as np
from jax import lax
from jax.experimental import pallas as pl
from jax.experimental.pallas import tpu as pltpu

D_MODEL = 1024
D_FF = 2816
CHUNK = 128
SPATIAL_PAIR = 2
SGU_GROUPS = 4
SGU_GROUP_DIM = D_MODEL // SGU_GROUPS
RET_HEADS = 4
RET_DIM = 256
SEG_U, SEG_V, SEG_Q, SEG_K, SEG_VR, SEG_G, SEG_GATE_A, SEG_GATE_B = range(8)
FORWARD, BACKWARD = 0, 1
ROPE_BASE = 10000.0
NORM_EPS = 1e-6

MXU_COLS = 256
BF16_SUBLANES = 16
FFN_ROWS = 512
MIX_ROWS = 512
RET_ROWS = 512
RET_BLOCK = 256
MIB = 1024 * 1024

BF16 = jnp.bfloat16
F32 = jnp.float32


def _rms(x, g):
    return x * lax.rsqrt(jnp.mean(x * x, axis=-1, keepdims=True) + NORM_EPS) * g


def _gelu(x):
    return 0.5 * x * (1.0 + lax.erf(x * np.float32(np.sqrt(0.5))))


def _silu(x):
    return x * jax.nn.sigmoid(x)


def _dot(a, b):
    return jnp.dot(a, b, preferred_element_type=F32)


def _col_chunks(width=D_MODEL):
    return [slice(c, c + MXU_COLS) for c in range(0, width, MXU_COLS)]


def _resident(shape):
    return pl.BlockSpec(shape, lambda *_: (0,) * len(shape), pipeline_mode=pl.Buffered(1))


def _swiglu_half_step(x, nw_ref, wg_ref, wu_ref, wd_ref, act_ref):
    h = _rms(x, nw_ref[...]).astype(BF16)
    for cols in _col_chunks(D_FF):
        g = _dot(h, wg_ref[:, cols].astype(BF16))
        u = _dot(h, wu_ref[:, cols].astype(BF16))
        act_ref[:, cols] = (_silu(g) * u).astype(BF16)
    act = act_ref[...]
    down = jnp.concatenate([_dot(act, wd_ref[:, cols].astype(BF16)) for cols in _col_chunks()], axis=-1)
    return x + 0.5 * down


def _ffn_body(*refs, n_cast, pos_rows):
    n = n_cast
    x_ref, nw_ref, wg_ref, wu_ref, wd_ref, theta_ref = refs[:6]
    srcs = refs[6:6 + n]
    o_ref = refs[6 + n]
    dsts = refs[7 + n:7 + 2 * n]
    cos_ref, sin_ref, cost_ref, sint_ref, act_ref, cos0_ref, sin0_ref = refs[7 + 2 * n:]
    step = pl.program_id(0)
    theta = theta_ref[...]

    @pl.when(step == 0)
    def _():
        ang0 = lax.broadcasted_iota(jnp.int32, (pos_rows, RET_DIM // 2), 0).astype(F32) * theta
        cos0_ref[...] = jnp.cos(ang0)
        sin0_ref[...] = jnp.sin(ang0)

    for src, dst in zip(srcs, dsts):
        dst[...] = src[...].astype(BF16)

    ang1 = (step * pos_rows).astype(F32) * theta
    cos1, sin1 = jnp.cos(ang1), jnp.sin(ang1)
    cos0, sin0 = cos0_ref[...], sin0_ref[...]
    cos = cos1 * cos0 - sin1 * sin0
    sin = sin1 * cos0 + cos1 * sin0
    cos_ref[...] = cos
    sin_ref[...] = sin
    cost_ref[...] = cos.T
    sint_ref[...] = sin.T

    o_ref[...] = _swiglu_half_step(x_ref[...], nw_ref, wg_ref, wu_ref, wd_ref, act_ref)


def _ffn_weight_specs():
    return [_resident((1, D_MODEL)), _resident((D_MODEL, D_FF)), _resident((D_MODEL, D_FF)),
            _resident((D_FF, D_MODEL))]


def _cast_rows(n_rows, n_steps):
    for rows in range(BF16_SUBLANES, n_rows + 1, BF16_SUBLANES):
        if n_rows % rows == 0 and n_rows // rows <= n_steps:
            return rows
    raise ValueError(f"cannot split {n_rows} rows over {n_steps} steps")


def _ffn(x, norm_w, w_gate, w_up, w_down, theta, later_weights, seq):
    t = x.shape[0]
    n_steps = t // FFN_ROWS
    pos_rows = seq // n_steps
    half = RET_DIM // 2
    assert seq % n_steps == 0 and pos_rows % 128 == 0
    rows = pl.BlockSpec((FFN_ROWS, D_MODEL), lambda i: (i, 0))

    cast_specs, cast_shapes = [], []
    for w in later_weights:
        r = _cast_rows(w.shape[0], n_steps)
        steps = w.shape[0] // r
        cast_specs.append(pl.BlockSpec((r, w.shape[1]), lambda i, last=steps - 1: (jnp.minimum(i, last), 0)))
        cast_shapes.append(jax.ShapeDtypeStruct(w.shape, BF16))
    table = pl.BlockSpec((pos_rows, half), lambda i: (i, 0))
    table_t = pl.BlockSpec((half, pos_rows), lambda i: (0, i))

    outs = pl.pallas_call(
        functools.partial(_ffn_body, n_cast=len(later_weights), pos_rows=pos_rows),
        grid=(n_steps,),
        in_specs=[rows] + _ffn_weight_specs() + [_resident((1, half))] + cast_specs,
        out_specs=[rows] + cast_specs + [table, table, table_t, table_t],
        out_shape=[jax.ShapeDtypeStruct((t, D_MODEL), F32)] + cast_shapes
                  + [jax.ShapeDtypeStruct((seq, half), F32)] * 2 + [jax.ShapeDtypeStruct((half, seq), F32)] * 2,
        scratch_shapes=[pltpu.VMEM((FFN_ROWS, D_FF), BF16), pltpu.VMEM((pos_rows, half), F32),
                        pltpu.VMEM((pos_rows, half), F32)],
        compiler_params=pltpu.CompilerParams(dimension_semantics=("arbitrary",), vmem_limit_bytes=60 * MIB),
        name="ffn",
    )(x, norm_w, w_gate, w_up, w_down, theta.reshape(1, half), *later_weights)
    n = len(later_weights)
    return outs[0], outs[1:1 + n], outs[1 + n:]


def _log_gamma(logit, shape):
    return jax.nn.log_sigmoid(jnp.full(shape, logit, F32))


def _zero_state_at_sequence_start(st_ref):
    @pl.when(pl.program_id(1) == 0)
    def _():
        st_ref[...] = jnp.zeros_like(st_ref)


def _state_update(st_ref, hd, kt, v, k_scale, block_decay):
    kd = (kt.astype(F32) * k_scale).astype(BF16)
    st_ref[hd] = st_ref[hd] * block_decay + _dot(kd, v)


def _retention_backward_head(dl_ref, hd, q_ref, kt_ref, v_ref, o_ref, st_ref):
    cols = slice(hd * RET_DIM, (hd + 1) * RET_DIM)
    row_idx = lax.broadcasted_iota(jnp.int32, (RET_BLOCK, RET_DIM), 0).astype(F32)
    lane_idx = lax.broadcasted_iota(jnp.int32, (RET_DIM, RET_BLOCK), 1).astype(F32)
    logit = dl_ref[BACKWARD, hd]
    q_scale = jnp.exp((RET_BLOCK - row_idx) * _log_gamma(logit, (RET_BLOCK, RET_DIM)))
    k_scale = jnp.exp(lane_idx * _log_gamma(logit, (RET_DIM, RET_BLOCK)))
    block_decay = jnp.exp(RET_BLOCK * _log_gamma(logit, (1, RET_DIM)))
    for blk in reversed(range(q_ref.shape[0] // RET_BLOCK)):
        rows = slice(blk * RET_BLOCK, (blk + 1) * RET_BLOCK)
        o_ref[rows, cols] = (_dot(q_ref[rows, cols], st_ref[hd].astype(BF16)) * q_scale).astype(BF16)
        _state_update(st_ref, hd, kt_ref[cols, rows], v_ref[rows, cols], k_scale, block_decay)


def _mixer_in_body(dl_ref, x_ref, nw_ref, w_ref, b_ref, lng_ref, lnb_ref, ws_ref, bs_ref, wa_ref,
                   cos_ref, sin_ref, cost_ref, sint_ref,
                   mixa_ref, q_ref, kt_ref, v_ref, sg_ref, sgb_ref, cb_ref, va_ref, vn_ref, a_ref, st_ref):
    _zero_state_at_sequence_start(st_ref)
    h = _rms(x_ref[...], nw_ref[...]).astype(BF16)
    half = RET_DIM // 2

    def proj(seg, cols):
        wcols = slice(seg * D_MODEL + cols.start, seg * D_MODEL + cols.stop)
        return _dot(h, w_ref[:, wcols]) + b_ref[:, wcols]

    cos, sin = cos_ref[...], sin_ref[...]
    cost, sint = cost_ref[...], sint_ref[...]
    k_scale = RET_DIM ** -0.5

    def retention_head(hd):
        cols = _col_chunks()[hd]
        t = proj(SEG_Q, cols)
        t1, t2 = t[:, :half], t[:, half:]
        q_ref[:, cols.start:cols.start + half] = (t1 * cos - t2 * sin).astype(BF16)
        q_ref[:, cols.start + half:cols.stop] = (t2 * cos + t1 * sin).astype(BF16)
        tt = proj(SEG_K, cols).T
        t1, t2 = tt[:half], tt[half:]
        kt_ref[cols.start:cols.start + half, :] = ((t1 * cost - t2 * sint) * k_scale).astype(BF16)
        kt_ref[cols.start + half:cols.stop, :] = ((t2 * cost + t1 * sint) * k_scale).astype(BF16)
        v_ref[:, cols] = proj(SEG_VR, cols).astype(BF16)
        sg_ref[:, cols] = _silu(proj(SEG_G, cols)).astype(BF16)
        sgb_ref[:, cols] = jax.nn.sigmoid(proj(SEG_GATE_B, cols)).astype(BF16)
        _retention_backward_head(dl_ref, hd, q_ref, kt_ref, v_ref, cb_ref, st_ref)

    for cols in _col_chunks():
        va_ref[:, cols] = _gelu(proj(SEG_V, cols))
    for hd in range(RET_HEADS // 2):
        retention_head(hd)
    va = va_ref[...]
    mu = jnp.mean(va, axis=-1, keepdims=True)
    vc = va - mu
    var = jnp.mean(vc * vc, axis=-1, keepdims=True)
    vn_ref[...] = (vc * lax.rsqrt(var + NORM_EPS) * lng_ref[...] + lnb_ref[...]).astype(BF16)
    for g, cols in enumerate(_col_chunks()):
        u = _gelu(proj(SEG_U, cols))
        for j in range(MIX_ROWS // (SPATIAL_PAIR * CHUNK)):
            rows = slice(j * SPATIAL_PAIR * CHUNK, (j + 1) * SPATIAL_PAIR * CHUNK)
            s = _dot(ws_ref[g], vn_ref[rows, cols]) + bs_ref[g]
            a_ref[rows, cols] = (u[rows] * s).astype(BF16)
    for hd in range(RET_HEADS // 2, RET_HEADS):
        retention_head(hd)
    a = a_ref[...]
    for cols in _col_chunks():
        mixa_ref[:, cols] = (jax.nn.sigmoid(proj(SEG_GATE_A, cols)) * _dot(a, wa_ref[:, cols])).astype(BF16)


def _paired_spatial_weight(w_s):
    eye = jnp.eye(SPATIAL_PAIR, dtype=w_s.dtype)
    paired = jnp.einsum("pq,gcm->gpcqm", eye, w_s)
    return paired.reshape(w_s.shape[0], SPATIAL_PAIR * CHUNK, SPATIAL_PAIR * CHUNK).astype(BF16)


def _mixer_in(decay_logit, x1, norm_w, w_in, b_in, ln_g, ln_b, w_s, b_s, w_a, cos, sin, cos_t, sin_t,
              batch, seq):
    t = x1.shape[0]
    half = RET_DIM // 2
    steps = seq // MIX_ROWS
    tile = lambda b, j: b * steps + steps - 1 - j
    rows = pl.BlockSpec((MIX_ROWS, D_MODEL), lambda b, j: (tile(b, j), 0))
    lanes = pl.BlockSpec((D_MODEL, MIX_ROWS), lambda b, j: (0, tile(b, j)))
    pos = pl.BlockSpec((MIX_ROWS, half), lambda b, j: (steps - 1 - j, 0))
    pos_t = pl.BlockSpec((half, MIX_ROWS), lambda b, j: (0, steps - 1 - j))
    width = w_in.shape[1]
    out = jax.ShapeDtypeStruct((t, D_MODEL), BF16)
    out_t = jax.ShapeDtypeStruct((D_MODEL, t), BF16)
    return pl.pallas_call(
        _mixer_in_body,
        grid=(batch, steps),
        in_specs=[pl.BlockSpec(memory_space=pltpu.SMEM),
                  rows, _resident((1, D_MODEL)), _resident((D_MODEL, width)), _resident((1, width)),
                  _resident((1, D_MODEL)), _resident((1, D_MODEL)), _resident(w_s.shape), _resident(b_s.shape),
                  _resident((D_MODEL, D_MODEL)), pos, pos, pos_t, pos_t],
        out_specs=[rows, rows, lanes, rows, rows, rows, rows],
        out_shape=[out, out, out_t, out, out, out, out],
        scratch_shapes=[pltpu.VMEM((MIX_ROWS, D_MODEL), F32), pltpu.VMEM((MIX_ROWS, D_MODEL), BF16),
                        pltpu.VMEM((MIX_ROWS, D_MODEL), BF16), pltpu.VMEM((RET_HEADS, RET_DIM, RET_DIM), F32)],
        compiler_params=pltpu.CompilerParams(dimension_semantics=("arbitrary", "arbitrary"),
                                             vmem_limit_bytes=56 * MIB),
        name="mixer_in",
    )(decay_logit, x1, norm_w, w_in, b_in, ln_g, ln_b, w_s, b_s, w_a, cos, sin, cos_t, sin_t)


def _mixer_out_body(*refs, final_norm):
    (dl_ref, q_ref, kt_ref, v_ref, cb_ref, sg_ref, sgb_ref, mixa_ref, x1_ref, wb_ref, wo_ref,
     nw_ref, wg_ref, wu_ref, wd_ref) = refs[:15]
    fn_ref = refs[15] if final_norm else None
    o_ref, st_ref, r_ref, mix_ref, x2_ref, act_ref = refs[-6:]
    _zero_state_at_sequence_start(st_ref)
    row_idx = lax.broadcasted_iota(jnp.int32, (RET_BLOCK, RET_DIM), 0).astype(F32)
    lane_idx = lax.broadcasted_iota(jnp.int32, (RET_DIM, RET_BLOCK), 1).astype(F32)
    diff = (lax.broadcasted_iota(jnp.int32, (RET_BLOCK, RET_BLOCK), 0)
            - lax.broadcasted_iota(jnp.int32, (RET_BLOCK, RET_BLOCK), 1)).astype(F32)
    for hd in range(RET_HEADS):
        cols = slice(hd * RET_DIM, (hd + 1) * RET_DIM)
        logit = dl_ref[FORWARD, hd]
        q_scale = jnp.exp((row_idx + 1.0) * _log_gamma(logit, (RET_BLOCK, RET_DIM)))
        k_scale = jnp.exp((RET_BLOCK - 1.0 - lane_idx) * _log_gamma(logit, (RET_DIM, RET_BLOCK)))
        block_decay = jnp.exp(RET_BLOCK * _log_gamma(logit, (1, RET_DIM)))
        lg_f = _log_gamma(logit, (RET_BLOCK, RET_BLOCK))
        lg_b = _log_gamma(dl_ref[BACKWARD, hd], (RET_BLOCK, RET_BLOCK))
        decay = jnp.exp(jnp.where(diff >= 0, diff * lg_f, -diff * lg_b))
        for blk in range(RET_ROWS // RET_BLOCK):
            rows = slice(blk * RET_BLOCK, (blk + 1) * RET_BLOCK)
            q = q_ref[rows, cols]
            kt = kt_ref[cols, rows]
            v = v_ref[rows, cols]
            r = (_dot((_dot(q, kt) * decay).astype(BF16), v)
                 + _dot(q, st_ref[hd].astype(BF16)) * q_scale
                 + cb_ref[rows, cols].astype(F32))
            r = r * lax.rsqrt(jnp.mean(r * r, axis=-1, keepdims=True) + NORM_EPS)
            r_ref[rows, cols] = (r * sg_ref[rows, cols].astype(F32)).astype(BF16)
            _state_update(st_ref, hd, kt, v, k_scale, block_decay)
    r = r_ref[...]
    for cols in _col_chunks():
        mix_ref[:, cols] = (sgb_ref[:, cols].astype(F32) * _dot(r, wb_ref[:, cols])
                            + mixa_ref[:, cols].astype(F32)).astype(BF16)
    mix = mix_ref[...]
    for cols in _col_chunks():
        x2_ref[:, cols] = x1_ref[:, cols] + _dot(mix, wo_ref[:, cols])
    y = _swiglu_half_step(x2_ref[...], nw_ref, wg_ref, wu_ref, wd_ref, act_ref)
    if final_norm:
        y = _rms(y, fn_ref[...])
    o_ref[...] = y


def _ret_specs(batch, seq):
    steps = seq // RET_ROWS
    rows = pl.BlockSpec((RET_ROWS, D_MODEL), lambda b, j: (b * steps + j, 0))
    lanes = pl.BlockSpec((D_MODEL, RET_ROWS), lambda b, j: (0, b * steps + j))
    return (batch, steps), rows, lanes


def _mixer_out(decay_logit, q, kt, v, cb, sg, sgb, mixa, x1, w_b, w_o, norm_w, w_gate, w_up, w_down, final_w,
               batch, seq):
    grid, rows, lanes = _ret_specs(batch, seq)
    in_specs = ([pl.BlockSpec(memory_space=pltpu.SMEM), rows, lanes] + [rows] * 6
                + [_resident((D_MODEL, D_MODEL)), _resident((D_MODEL, D_MODEL))] + _ffn_weight_specs())
    args = [decay_logit, q, kt, v, cb, sg, sgb, mixa, x1, w_b, w_o, norm_w, w_gate, w_up, w_down]
    if final_w is not None:
        in_specs.append(_resident((1, D_MODEL)))
        args.append(final_w)
    return pl.pallas_call(
        functools.partial(_mixer_out_body, final_norm=final_w is not None),
        grid=grid,
        in_specs=in_specs,
        out_specs=rows,
        out_shape=jax.ShapeDtypeStruct(q.shape, F32),
        scratch_shapes=[pltpu.VMEM((RET_HEADS, RET_DIM, RET_DIM), F32), pltpu.VMEM((RET_ROWS, D_MODEL), BF16),
                        pltpu.VMEM((RET_ROWS, D_MODEL), BF16), pltpu.VMEM((RET_ROWS, D_MODEL), F32),
                        pltpu.VMEM((RET_ROWS, D_FF), BF16)],
        compiler_params=pltpu.CompilerParams(dimension_semantics=("arbitrary", "arbitrary"),
                                             vmem_limit_bytes=60 * MIB),
        name="mixer_out",
    )(*args)


def kernel(x, ffn1_norm, ffn1_w_gate, ffn1_w_up, ffn1_w_down, mix_norm, w_in, b_in, sgu_norm_g, sgu_norm_b, sgu_w_s, sgu_b_s, ret_decay_logit, w_branch_a, w_branch_b, w_out, ffn2_norm, ffn2_w_gate, ffn2_w_up, ffn2_w_down, final_norm):
    batch, seq, _ = x.shape
    depth = ffn1_norm.shape[0]
    assert depth >= 1 and seq % RET_ROWS == 0 and seq % MIX_ROWS == 0 and (batch * seq) % FFN_ROWS == 0
    tokens = batch * seq
    row = lambda p: p.reshape(1, -1)
    theta = ROPE_BASE ** (-jnp.arange(0, RET_DIM, 2, dtype=F32) / RET_DIM)

    xt = x.reshape(tokens, D_MODEL)
    for l in range(depth):
        last = l == depth - 1
        x1, (w_in_bf, w_a, w_b, w_o, w_gate2, w_up2, w_down2), (cos, sin, cos_t, sin_t) = _ffn(
            xt, row(ffn1_norm[l]), ffn1_w_gate[l], ffn1_w_up[l], ffn1_w_down[l], theta,
            [w_in[l], w_branch_a[l], w_branch_b[l], w_out[l], ffn2_w_gate[l], ffn2_w_up[l], ffn2_w_down[l]], seq)
        mixa, q, kt, v, sg, sgb, cb = _mixer_in(
            ret_decay_logit[l], x1, row(mix_norm[l]), w_in_bf, row(b_in[l]),
            row(sgu_norm_g[l]), row(sgu_norm_b[l]), _paired_spatial_weight(sgu_w_s[l]),
            jnp.tile(sgu_b_s[l], (1, SPATIAL_PAIR))[..., None], w_a, cos, sin, cos_t, sin_t, batch, seq)
        xt = _mixer_out(ret_decay_logit[l], q, kt, v, cb, sg, sgb, mixa, x1, w_b, w_o,
                        row(ffn2_norm[l]), w_gate2, w_up2, w_down2,
                        row(final_norm) if last else None, batch, seq)
    return xt.reshape(batch, seq, D_MODEL)
```

```python
import functools

import jax
import jax.numpy as jnp
import numpy as np
from jax import lax
from jax.experimental import pallas as pl
from jax.experimental.pallas import tpu as pltpu

D_MODEL = 1024
D_FF = 2816
CHUNK = 128
SPATIAL_PAIR = 2
SGU_GROUPS = 4
SGU_GROUP_DIM = D_MODEL // SGU_GROUPS
RET_HEADS = 4
RET_DIM = 256
SEG_U, SEG_V, SEG_Q, SEG_K, SEG_VR, SEG_G, SEG_GATE_A, SEG_GATE_B = range(8)
FORWARD, BACKWARD = 0, 1
PACK_Q, PACK_V, PACK_SG, PACK_SGB, PACK_MIXA, PACK_CB = range(6)
N_PACK = 6
ROPE_BASE = 10000.0
NORM_EPS = 1e-6

MXU_COLS = 256
BF16_SUBLANES = 16
FFN_ROWS = 512
MIX_ROWS = 512
RET_ROWS = 512
RET_BLOCK = 256
MIB = 1024 * 1024

BF16 = jnp.bfloat16
F32 = jnp.float32


def _rms(x, g):
    return x * lax.rsqrt(jnp.mean(x * x, axis=-1, keepdims=True) + NORM_EPS) * g


def _gelu(x):
    return 0.5 * x * (1.0 + lax.erf(x * np.float32(np.sqrt(0.5))))


def _silu(x):
    return x * jax.nn.sigmoid(x)


def _dot(a, b):
    return jnp.dot(a, b, preferred_element_type=F32)


def _col_chunks(width=D_MODEL):
    return [slice(c, c + MXU_COLS) for c in range(0, width, MXU_COLS)]


def _packed_views(pk_ref):
    return [pk_ref.at[:, i * D_MODEL:(i + 1) * D_MODEL] for i in range(N_PACK)]


def _resident(shape):
    return pl.BlockSpec(shape, lambda *_: (0,) * len(shape), pipeline_mode=pl.Buffered(1))


def _swiglu_half_step(x, nw_ref, wg_ref, wu_ref, wd_ref, act_ref):
    h = _rms(x, nw_ref[...]).astype(BF16)
    for cols in _col_chunks(D_FF):
        g = _dot(h, wg_ref[:, cols].astype(BF16))
        u = _dot(h, wu_ref[:, cols].astype(BF16))
        act_ref[:, cols] = (_silu(g) * u).astype(BF16)
    act = act_ref[...]
    down = jnp.concatenate([_dot(act, wd_ref[:, cols].astype(BF16)) for cols in _col_chunks()], axis=-1)
    return x + 0.5 * down


def _ffn_body(*refs, n_cast, pos_rows):
    n = n_cast
    x_ref, nw_ref, wg_ref, wu_ref, wd_ref, theta_ref = refs[:6]
    srcs = refs[6:6 + n]
    o_ref = refs[6 + n]
    dsts = refs[7 + n:7 + 2 * n]
    cos_ref, sin_ref, cost_ref, sint_ref, act_ref, cos0_ref, sin0_ref = refs[7 + 2 * n:]
    step = pl.program_id(0)
    theta = theta_ref[...]

    @pl.when(step == 0)
    def _():
        ang0 = lax.broadcasted_iota(jnp.int32, (pos_rows, RET_DIM // 2), 0).astype(F32) * theta
        cos0_ref[...] = jnp.cos(ang0)
        sin0_ref[...] = jnp.sin(ang0)

    for src, dst in zip(srcs, dsts):
        dst[...] = src[...].astype(BF16)

    ang1 = (step * pos_rows).astype(F32) * theta
    cos1, sin1 = jnp.cos(ang1), jnp.sin(ang1)
    cos0, sin0 = cos0_ref[...], sin0_ref[...]
    cos = cos1 * cos0 - sin1 * sin0
    sin = sin1 * cos0 + cos1 * sin0
    cos_ref[...] = cos
    sin_ref[...] = sin
    cost_ref[...] = cos.T
    sint_ref[...] = sin.T

    o_ref[...] = _swiglu_half_step(x_ref[...], nw_ref, wg_ref, wu_ref, wd_ref, act_ref)


def _ffn_weight_specs():
    return [_resident((1, D_MODEL)), _resident((D_MODEL, D_FF)), _resident((D_MODEL, D_FF)),
            _resident((D_FF, D_MODEL))]


def _cast_rows(n_rows, n_steps):
    for rows in range(BF16_SUBLANES, n_rows + 1, BF16_SUBLANES):
        if n_rows % rows == 0 and n_rows // rows <= n_steps:
            return rows
    raise ValueError(f"cannot split {n_rows} rows over {n_steps} steps")


def _ffn(x, norm_w, w_gate, w_up, w_down, theta, later_weights, seq):
    t = x.shape[0]
    n_steps = t // FFN_ROWS
    pos_rows = seq // n_steps
    half = RET_DIM // 2
    assert seq % n_steps == 0 and pos_rows % 128 == 0
    rows = pl.BlockSpec((FFN_ROWS, D_MODEL), lambda i: (i, 0))

    cast_specs, cast_shapes = [], []
    for w in later_weights:
        r = _cast_rows(w.shape[0], n_steps)
        steps = w.shape[0] // r
        cast_specs.append(pl.BlockSpec((r, w.shape[1]), lambda i, last=steps - 1: (jnp.minimum(i, last), 0)))
        cast_shapes.append(jax.ShapeDtypeStruct(w.shape, BF16))
    table = pl.BlockSpec((pos_rows, half), lambda i: (i, 0))
    table_t = pl.BlockSpec((half, pos_rows), lambda i: (0, i))

    outs = pl.pallas_call(
        functools.partial(_ffn_body, n_cast=len(later_weights), pos_rows=pos_rows),
        grid=(n_steps,),
        in_specs=[rows] + _ffn_weight_specs() + [_resident((1, half))] + cast_specs,
        out_specs=[rows] + cast_specs + [table, table, table_t, table_t],
        out_shape=[jax.ShapeDtypeStruct((t, D_MODEL), F32)] + cast_shapes
                  + [jax.ShapeDtypeStruct((seq, half), F32)] * 2 + [jax.ShapeDtypeStruct((half, seq), F32)] * 2,
        scratch_shapes=[pltpu.VMEM((FFN_ROWS, D_FF), BF16), pltpu.VMEM((pos_rows, half), F32),
                        pltpu.VMEM((pos_rows, half), F32)],
        compiler_params=pltpu.CompilerParams(dimension_semantics=("arbitrary",), vmem_limit_bytes=60 * MIB),
        name="ffn",
    )(x, norm_w, w_gate, w_up, w_down, theta.reshape(1, half), *later_weights)
    n = len(later_weights)
    return outs[0], outs[1:1 + n], outs[1 + n:]


def _log_gamma(logit, shape):
    return jax.nn.log_sigmoid(jnp.full(shape, logit, F32))


def _zero_state_at_sequence_start(st_ref):
    @pl.when(pl.program_id(1) == 0)
    def _():
        st_ref[...] = jnp.zeros_like(st_ref)


def _state_update(st_ref, hd, kt, v, k_scale, block_decay):
    kd = (kt.astype(F32) * k_scale).astype(BF16)
    st_ref[hd] = st_ref[hd] * block_decay + _dot(kd, v)


def _retention_backward_head(dl_ref, hd, q_ref, kt_ref, v_ref, o_ref, st_ref):
    cols = slice(hd * RET_DIM, (hd + 1) * RET_DIM)
    row_idx = lax.broadcasted_iota(jnp.int32, (RET_BLOCK, RET_DIM), 0).astype(F32)
    lane_idx = lax.broadcasted_iota(jnp.int32, (RET_DIM, RET_BLOCK), 1).astype(F32)
    logit = dl_ref[BACKWARD, hd]
    q_scale = jnp.exp((RET_BLOCK - row_idx) * _log_gamma(logit, (RET_BLOCK, RET_DIM)))
    k_scale = jnp.exp(lane_idx * _log_gamma(logit, (RET_DIM, RET_BLOCK)))
    block_decay = jnp.exp(RET_BLOCK * _log_gamma(logit, (1, RET_DIM)))
    for blk in reversed(range(q_ref.shape[0] // RET_BLOCK)):
        rows = slice(blk * RET_BLOCK, (blk + 1) * RET_BLOCK)
        o_ref[rows, cols] = (_dot(q_ref[rows, cols], st_ref[hd].astype(BF16)) * q_scale).astype(BF16)
        _state_update(st_ref, hd, kt_ref[cols, rows], v_ref[rows, cols], k_scale, block_decay)


def _mixer_in_body(dl_ref, x_ref, nw_ref, w_ref, b_ref, lng_ref, lnb_ref, ws_ref, bs_ref, wa_ref,
                   cos_ref, sin_ref, cost_ref, sint_ref,
                   pk_ref, kt_ref, va_ref, vn_ref, a_ref, st_ref):
    q_ref, v_ref, sg_ref, sgb_ref, mixa_ref, cb_ref = _packed_views(pk_ref)
    _zero_state_at_sequence_start(st_ref)
    h = _rms(x_ref[...], nw_ref[...]).astype(BF16)
    half = RET_DIM // 2

    def proj(seg, cols):
        wcols = slice(seg * D_MODEL + cols.start, seg * D_MODEL + cols.stop)
        return _dot(h, w_ref[:, wcols]) + b_ref[:, wcols]

    cos, sin = cos_ref[...], sin_ref[...]
    cost, sint = cost_ref[...], sint_ref[...]
    k_scale = RET_DIM ** -0.5

    def retention_head(hd):
        cols = _col_chunks()[hd]
        t = proj(SEG_Q, cols)
        t1, t2 = t[:, :half], t[:, half:]
        q_ref[:, cols.start:cols.start + half] = (t1 * cos - t2 * sin).astype(BF16)
        q_ref[:, cols.start + half:cols.stop] = (t2 * cos + t1 * sin).astype(BF16)
        tt = proj(SEG_K, cols).T
        t1, t2 = tt[:half], tt[half:]
        kt_ref[cols.start:cols.start + half, :] = ((t1 * cost - t2 * sint) * k_scale).astype(BF16)
        kt_ref[cols.start + half:cols.stop, :] = ((t2 * cost + t1 * sint) * k_scale).astype(BF16)
        v_ref[:, cols] = proj(SEG_VR, cols).astype(BF16)
        sg_ref[:, cols] = _silu(proj(SEG_G, cols)).astype(BF16)
        sgb_ref[:, cols] = jax.nn.sigmoid(proj(SEG_GATE_B, cols)).astype(BF16)
        _retention_backward_head(dl_ref, hd, q_ref, kt_ref, v_ref, cb_ref, st_ref)

    for cols in _col_chunks():
        va_ref[:, cols] = _gelu(proj(SEG_V, cols))
    for hd in range(RET_HEADS // 2):
        retention_head(hd)
    va = va_ref[...]
    mu = jnp.mean(va, axis=-1, keepdims=True)
    vc = va - mu
    var = jnp.mean(vc * vc, axis=-1, keepdims=True)
    vn_ref[...] = (vc * lax.rsqrt(var + NORM_EPS) * lng_ref[...] + lnb_ref[...]).astype(BF16)
    for g, cols in enumerate(_col_chunks()):
        u = _gelu(proj(SEG_U, cols))
        for j in range(MIX_ROWS // (SPATIAL_PAIR * CHUNK)):
            rows = slice(j * SPATIAL_PAIR * CHUNK, (j + 1) * SPATIAL_PAIR * CHUNK)
            s = _dot(ws_ref[g], vn_ref[rows, cols]) + bs_ref[g]
            a_ref[rows, cols] = (u[rows] * s).astype(BF16)
    for hd in range(RET_HEADS // 2, RET_HEADS):
        retention_head(hd)
    a = a_ref[...]
    for cols in _col_chunks():
        mixa_ref[:, cols] = (jax.nn.sigmoid(proj(SEG_GATE_A, cols)) * _dot(a, wa_ref[:, cols])).astype(BF16)


def _paired_spatial_weight(w_s):
    eye = jnp.eye(SPATIAL_PAIR, dtype=w_s.dtype)
    paired = jnp.einsum("pq,gcm->gpcqm", eye, w_s)
    return paired.reshape(w_s.shape[0], SPATIAL_PAIR * CHUNK, SPATIAL_PAIR * CHUNK).astype(BF16)


def _mixer_in(decay_logit, x1, norm_w, w_in, b_in, ln_g, ln_b, w_s, b_s, w_a, cos, sin, cos_t, sin_t,
              batch, seq):
    t = x1.shape[0]
    half = RET_DIM // 2
    steps = seq // MIX_ROWS
    tile = lambda b, j: b * steps + steps - 1 - j
    rows = pl.BlockSpec((MIX_ROWS, D_MODEL), lambda b, j: (tile(b, j), 0))
    lanes = pl.BlockSpec((D_MODEL, MIX_ROWS), lambda b, j: (0, tile(b, j)))
    pos = pl.BlockSpec((MIX_ROWS, half), lambda b, j: (steps - 1 - j, 0))
    pos_t = pl.BlockSpec((half, MIX_ROWS), lambda b, j: (0, steps - 1 - j))
    packed = pl.BlockSpec((MIX_ROWS, N_PACK * D_MODEL), lambda b, j: (tile(b, j), 0))
    width = w_in.shape[1]
    out_t = jax.ShapeDtypeStruct((D_MODEL, t), BF16)
    return pl.pallas_call(
        _mixer_in_body,
        grid=(batch, steps),
        in_specs=[pl.BlockSpec(memory_space=pltpu.SMEM),
                  rows, _resident((1, D_MODEL)), _resident((D_MODEL, width)), _resident((1, width)),
                  _resident((1, D_MODEL)), _resident((1, D_MODEL)), _resident(w_s.shape), _resident(b_s.shape),
                  _resident((D_MODEL, D_MODEL)), pos, pos, pos_t, pos_t],
        out_specs=[packed, lanes],
        out_shape=[jax.ShapeDtypeStruct((t, N_PACK * D_MODEL), BF16), out_t],
        scratch_shapes=[pltpu.VMEM((MIX_ROWS, D_MODEL), F32), pltpu.VMEM((MIX_ROWS, D_MODEL), BF16),
                        pltpu.VMEM((MIX_ROWS, D_MODEL), BF16), pltpu.VMEM((RET_HEADS, RET_DIM, RET_DIM), F32)],
        compiler_params=pltpu.CompilerParams(dimension_semantics=("arbitrary", "arbitrary"),
                                             vmem_limit_bytes=56 * MIB),
        name="mixer_in",
    )(decay_logit, x1, norm_w, w_in, b_in, ln_g, ln_b, w_s, b_s, w_a, cos, sin, cos_t, sin_t)


def _mixer_out_body(*refs, final_norm):
    dl_ref, pk_ref, kt_ref, x1_ref, wb_ref, wo_ref, nw_ref, wg_ref, wu_ref, wd_ref = refs[:10]
    fn_ref = refs[10] if final_norm else None
    q_ref, v_ref, sg_ref, sgb_ref, mixa_ref, cb_ref = _packed_views(pk_ref)
    o_ref, st_ref, p_ref, r_ref, mix_ref, x2_ref, act_ref = refs[-7:]
    _zero_state_at_sequence_start(st_ref)
    row_idx = lax.broadcasted_iota(jnp.int32, (RET_BLOCK, RET_DIM), 0).astype(F32)
    lane_idx = lax.broadcasted_iota(jnp.int32, (RET_DIM, RET_BLOCK), 1).astype(F32)
    diff = (lax.broadcasted_iota(jnp.int32, (RET_BLOCK, RET_BLOCK), 0)
            - lax.broadcasted_iota(jnp.int32, (RET_BLOCK, RET_BLOCK), 1)).astype(F32)
    n_blocks = RET_ROWS // RET_BLOCK
    head_cols = lambda hd: slice(hd * RET_DIM, (hd + 1) * RET_DIM)
    block_rows = lambda blk: slice(blk * RET_BLOCK, (blk + 1) * RET_BLOCK)

    for hd in range(RET_HEADS):
        lg_f = _log_gamma(dl_ref[FORWARD, hd], (RET_BLOCK, RET_BLOCK))
        lg_b = _log_gamma(dl_ref[BACKWARD, hd], (RET_BLOCK, RET_BLOCK))
        decay = jnp.exp(jnp.where(diff >= 0, diff * lg_f, -diff * lg_b))
        for blk in range(n_blocks):
            scores = _dot(q_ref[block_rows(blk), head_cols(hd)], kt_ref[head_cols(hd), block_rows(blk)])
            p_ref[hd * n_blocks + blk] = (scores * decay).astype(BF16)

    for hd in range(RET_HEADS):
        cols = head_cols(hd)
        logit = dl_ref[FORWARD, hd]
        q_scale = jnp.exp((row_idx + 1.0) * _log_gamma(logit, (RET_BLOCK, RET_DIM)))
        k_scale = jnp.exp((RET_BLOCK - 1.0 - lane_idx) * _log_gamma(logit, (RET_DIM, RET_BLOCK)))
        block_decay = jnp.exp(RET_BLOCK * _log_gamma(logit, (1, RET_DIM)))
        for blk in range(n_blocks):
            rows = block_rows(blk)
            v = v_ref[rows, cols]
            r = (_dot(p_ref[hd * n_blocks + blk], v)
                 + _dot(q_ref[rows, cols], st_ref[hd].astype(BF16)) * q_scale
                 + cb_ref[rows, cols].astype(F32))
            r = r * lax.rsqrt(jnp.mean(r * r, axis=-1, keepdims=True) + NORM_EPS)
            r_ref[rows, cols] = (r * sg_ref[rows, cols].astype(F32)).astype(BF16)
            _state_update(st_ref, hd, kt_ref[cols, rows], v, k_scale, block_decay)
    r = r_ref[...]
    for cols in _col_chunks():
        mix_ref[:, cols] = (sgb_ref[:, cols].astype(F32) * _dot(r, wb_ref[:, cols])
                            + mixa_ref[:, cols].astype(F32)).astype(BF16)
    mix = mix_ref[...]
    for cols in _col_chunks():
        x2_ref[:, cols] = x1_ref[:, cols] + _dot(mix, wo_ref[:, cols])
    y = _swiglu_half_step(x2_ref[...], nw_ref, wg_ref, wu_ref, wd_ref, act_ref)
    if final_norm:
        y = _rms(y, fn_ref[...])
    o_ref[...] = y


def _mixer_out(decay_logit, packed, kt, x1, w_b, w_o, norm_w, w_gate, w_up, w_down, final_w, batch, seq):
    steps = seq // RET_ROWS
    rows = pl.BlockSpec((RET_ROWS, D_MODEL), lambda b, j: (b * steps + j, 0))
    packed_rows = pl.BlockSpec((RET_ROWS, N_PACK * D_MODEL), lambda b, j: (b * steps + j, 0))
    lanes = pl.BlockSpec((D_MODEL, RET_ROWS), lambda b, j: (0, b * steps + j))
    in_specs = ([pl.BlockSpec(memory_space=pltpu.SMEM), packed_rows, lanes, rows]
                + [_resident((D_MODEL, D_MODEL)), _resident((D_MODEL, D_MODEL))] + _ffn_weight_specs())
    args = [decay_logit, packed, kt, x1, w_b, w_o, norm_w, w_gate, w_up, w_down]
    if final_w is not None:
        in_specs.append(_resident((1, D_MODEL)))
        args.append(final_w)
    return pl.pallas_call(
        functools.partial(_mixer_out_body, final_norm=final_w is not None),
        grid=(batch, steps),
        in_specs=in_specs,
        out_specs=rows,
        out_shape=jax.ShapeDtypeStruct(x1.shape, F32),
        scratch_shapes=[pltpu.VMEM((RET_HEADS, RET_DIM, RET_DIM), F32),
                        pltpu.VMEM((RET_HEADS * (RET_ROWS // RET_BLOCK), RET_BLOCK, RET_BLOCK), BF16),
                        pltpu.VMEM((RET_ROWS, D_MODEL), BF16), pltpu.VMEM((RET_ROWS, D_MODEL), BF16),
                        pltpu.VMEM((RET_ROWS, D_MODEL), F32), pltpu.VMEM((RET_ROWS, D_FF), BF16)],
        compiler_params=pltpu.CompilerParams(dimension_semantics=("arbitrary", "arbitrary"),
                                             vmem_limit_bytes=60 * MIB),
        name="mixer_out",
    )(*args)


def kernel(x, ffn1_norm, ffn1_w_gate, ffn1_w_up, ffn1_w_down, mix_norm, w_in, b_in, sgu_norm_g, sgu_norm_b, sgu_w_s, sgu_b_s, ret_decay_logit, w_branch_a, w_branch_b, w_out, ffn2_norm, ffn2_w_gate, ffn2_w_up, ffn2_w_down, final_norm):
    batch, seq, _ = x.shape
    depth = ffn1_norm.shape[0]
    assert depth >= 1 and seq % RET_ROWS == 0 and seq % MIX_ROWS == 0 and (batch * seq) % FFN_ROWS == 0
    tokens = batch * seq
    row = lambda p: p.reshape(1, -1)
    theta = ROPE_BASE ** (-jnp.arange(0, RET_DIM, 2, dtype=F32) / RET_DIM)

    xt = x.reshape(tokens, D_MODEL)
    for l in range(depth):
        last = l == depth - 1
        x1, (w_in_bf, w_a, w_b, w_o, w_gate2, w_up2, w_down2), (cos, sin, cos_t, sin_t) = _ffn(
            xt, row(ffn1_norm[l]), ffn1_w_gate[l], ffn1_w_up[l], ffn1_w_down[l], theta,
            [w_in[l], w_branch_a[l], w_branch_b[l], w_out[l], ffn2_w_gate[l], ffn2_w_up[l], ffn2_w_down[l]], seq)
        packed, kt = _mixer_in(
            ret_decay_logit[l], x1, row(mix_norm[l]), w_in_bf, row(b_in[l]),
            row(sgu_norm_g[l]), row(sgu_norm_b[l]), _paired_spatial_weight(sgu_w_s[l]),
            jnp.tile(sgu_b_s[l], (1, SPATIAL_PAIR))[..., None], w_a, cos, sin, cos_t, sin_t, batch, seq)
        xt = _mixer_out(ret_decay_logit[l], packed, kt, x1, w_b, w_o,
                        row(ffn2_norm[l]), w_gate2, w_up2, w_down2,
                        row(final_norm) if last else None, batch, seq)
    return xt.reshape(batch, seq, D_MODEL)
```

```python
import functools

import jax
import jax.numpy as jnp
import numpy as np
from jax import lax
from jax.experimental import pallas as pl
from jax.experimental.pallas import tpu as pltpu

D_MODEL = 1024
D_FF = 2816
CHUNK = 128
SPATIAL_PAIR = 2
SGU_GROUPS = 4
SGU_GROUP_DIM = D_MODEL // SGU_GROUPS
RET_HEADS = 4
RET_DIM = 256
SEG_U, SEG_V, SEG_Q, SEG_K, SEG_VR, SEG_G, SEG_GATE_A, SEG_GATE_B = range(8)
FORWARD, BACKWARD = 0, 1
PACK_Q, PACK_V, PACK_SG, PACK_SGB, PACK_MIXA, PACK_CB = range(6)
N_PACK = 6
ROPE_BASE = 10000.0
NORM_EPS = 1e-6

MXU_COLS = 256
BF16_SUBLANES = 16
FFN_ROWS = 512
MIX_ROWS = 512
RET_ROWS = 512
RET_BLOCK = 256
MIB = 1024 * 1024

BF16 = jnp.bfloat16
F32 = jnp.float32


def _rms(x, g):
    return x * lax.rsqrt(jnp.mean(x * x, axis=-1, keepdims=True) + NORM_EPS) * g


def _gelu(x):
    return 0.5 * x * (1.0 + lax.erf(x * np.float32(np.sqrt(0.5))))


def _silu(x):
    return x * jax.nn.sigmoid(x)


def _dot(a, b):
    return jnp.dot(a, b, preferred_element_type=F32)


def _col_chunks(width=D_MODEL):
    return [slice(c, c + MXU_COLS) for c in range(0, width, MXU_COLS)]


def _packed_views(pk_ref):
    return [pk_ref.at[:, i * D_MODEL:(i + 1) * D_MODEL] for i in range(N_PACK)]


def _resident(shape):
    return pl.BlockSpec(shape, lambda *_: (0,) * len(shape), pipeline_mode=pl.Buffered(1))


def _swiglu_half_step(x, nw_ref, wg_ref, wu_ref, wd_ref, act_ref):
    h = (x * nw_ref[...]).astype(BF16)
    inv_rms = lax.rsqrt(jnp.mean(x * x, axis=-1, keepdims=True) + NORM_EPS)
    for cols in _col_chunks(D_FF):
        g = _dot(h, wg_ref[:, cols].astype(BF16)) * inv_rms
        u = _dot(h, wu_ref[:, cols].astype(BF16)) * inv_rms
        act_ref[:, cols] = (_silu(g) * u).astype(BF16)
    act = act_ref[...]
    down = jnp.concatenate([_dot(act, wd_ref[:, cols].astype(BF16)) for cols in _col_chunks()], axis=-1)
    return x + 0.5 * down


def _ffn_body(*refs, n_cast, pos_rows):
    n = n_cast
    x_ref, nw_ref, wg_ref, wu_ref, wd_ref, theta_ref = refs[:6]
    srcs = refs[6:6 + n]
    o_ref = refs[6 + n]
    dsts = refs[7 + n:7 + 2 * n]
    cos_ref, sin_ref, cost_ref, sint_ref, act_ref, cos0_ref, sin0_ref = refs[7 + 2 * n:]
    step = pl.program_id(0)
    theta = theta_ref[...]

    @pl.when(step == 0)
    def _():
        ang0 = lax.broadcasted_iota(jnp.int32, (pos_rows, RET_DIM // 2), 0).astype(F32) * theta
        cos0_ref[...] = jnp.cos(ang0)
        sin0_ref[...] = jnp.sin(ang0)

    for src, dst in zip(srcs, dsts):
        dst[...] = src[...].astype(BF16)

    ang1 = (step * pos_rows).astype(F32) * theta
    cos1, sin1 = jnp.cos(ang1), jnp.sin(ang1)
    cos0, sin0 = cos0_ref[...], sin0_ref[...]
    cos = cos1 * cos0 - sin1 * sin0
    sin = sin1 * cos0 + cos1 * sin0
    cos_ref[...] = cos
    sin_ref[...] = sin
    cost_ref[...] = cos.T
    sint_ref[...] = sin.T

    o_ref[...] = _swiglu_half_step(x_ref[...], nw_ref, wg_ref, wu_ref, wd_ref, act_ref)


def _ffn_weight_specs():
    return [_resident((1, D_MODEL)), _resident((D_MODEL, D_FF)), _resident((D_MODEL, D_FF)),
            _resident((D_FF, D_MODEL))]


def _cast_rows(n_rows, n_steps):
    for rows in range(BF16_SUBLANES, n_rows + 1, BF16_SUBLANES):
        if n_rows % rows == 0 and n_rows // rows <= n_steps:
            return rows
    raise ValueError(f"cannot split {n_rows} rows over {n_steps} steps")


def _ffn(x, norm_w, w_gate, w_up, w_down, theta, later_weights, seq):
    t = x.shape[0]
    n_steps = t // FFN_ROWS
    pos_rows = seq // n_steps
    half = RET_DIM // 2
    assert seq % n_steps == 0 and pos_rows % 128 == 0
    rows = pl.BlockSpec((FFN_ROWS, D_MODEL), lambda i: (i, 0))

    cast_specs, cast_shapes = [], []
    for w in later_weights:
        r = _cast_rows(w.shape[0], n_steps)
        steps = w.shape[0] // r
        cast_specs.append(pl.BlockSpec((r, w.shape[1]), lambda i, last=steps - 1: (jnp.minimum(i, last), 0)))
        cast_shapes.append(jax.ShapeDtypeStruct(w.shape, BF16))
    table = pl.BlockSpec((pos_rows, half), lambda i: (i, 0))
    table_t = pl.BlockSpec((half, pos_rows), lambda i: (0, i))

    outs = pl.pallas_call(
        functools.partial(_ffn_body, n_cast=len(later_weights), pos_rows=pos_rows),
        grid=(n_steps,),
        in_specs=[rows] + _ffn_weight_specs() + [_resident((1, half))] + cast_specs,
        out_specs=[rows] + cast_specs + [table, table, table_t, table_t],
        out_shape=[jax.ShapeDtypeStruct((t, D_MODEL), F32)] + cast_shapes
                  + [jax.ShapeDtypeStruct((seq, half), F32)] * 2 + [jax.ShapeDtypeStruct((half, seq), F32)] * 2,
        scratch_shapes=[pltpu.VMEM((FFN_ROWS, D_FF), BF16), pltpu.VMEM((pos_rows, half), F32),
                        pltpu.VMEM((pos_rows, half), F32)],
        compiler_params=pltpu.CompilerParams(dimension_semantics=("arbitrary",), vmem_limit_bytes=60 * MIB),
        name="ffn",
    )(x, norm_w, w_gate, w_up, w_down, theta.reshape(1, half), *later_weights)
    n = len(later_weights)
    return outs[0], outs[1:1 + n], outs[1 + n:]


def _log_gamma(logit, shape):
    return jax.nn.log_sigmoid(jnp.full(shape, logit, F32))


def _zero_state_at_sequence_start(st_ref):
    @pl.when(pl.program_id(1) == 0)
    def _():
        st_ref[...] = jnp.zeros_like(st_ref)


def _state_update(st_ref, hd, kt, v, k_scale, block_decay):
    kd = (kt.astype(F32) * k_scale).astype(BF16)
    st_ref[hd] = st_ref[hd] * block_decay + _dot(kd, v)


def _retention_backward_head(dl_ref, hd, q_ref, kt_ref, v_ref, o_ref, st_ref):
    cols = slice(hd * RET_DIM, (hd + 1) * RET_DIM)
    row_idx = lax.broadcasted_iota(jnp.int32, (RET_BLOCK, RET_DIM), 0).astype(F32)
    lane_idx = lax.broadcasted_iota(jnp.int32, (RET_DIM, RET_BLOCK), 1).astype(F32)
    logit = dl_ref[BACKWARD, hd]
    q_scale = jnp.exp((RET_BLOCK - row_idx) * _log_gamma(logit, (RET_BLOCK, RET_DIM)))
    k_scale = jnp.exp(lane_idx * _log_gamma(logit, (RET_DIM, RET_BLOCK)))
    block_decay = jnp.exp(RET_BLOCK * _log_gamma(logit, (1, RET_DIM)))
    for blk in reversed(range(q_ref.shape[0] // RET_BLOCK)):
        rows = slice(blk * RET_BLOCK, (blk + 1) * RET_BLOCK)
        o_ref[rows, cols] = (_dot(q_ref[rows, cols], st_ref[hd].astype(BF16)) * q_scale).astype(BF16)
        _state_update(st_ref, hd, kt_ref[cols, rows], v_ref[rows, cols], k_scale, block_decay)


def _mixer_in_body(dl_ref, x_ref, nw_ref, w_ref, b_ref, lng_ref, lnb_ref, ws_ref, bs_ref, wa_ref,
                   cos_ref, sin_ref, cost_ref, sint_ref,
                   pk_ref, kt_ref, va_ref, vn_ref, a_ref, st_ref):
    q_ref, v_ref, sg_ref, sgb_ref, mixa_ref, cb_ref = _packed_views(pk_ref)
    _zero_state_at_sequence_start(st_ref)
    h = _rms(x_ref[...], nw_ref[...]).astype(BF16)
    half = RET_DIM // 2

    def proj(seg, cols):
        wcols = slice(seg * D_MODEL + cols.start, seg * D_MODEL + cols.stop)
        return _dot(h, w_ref[:, wcols]) + b_ref[:, wcols]

    cos, sin = cos_ref[...], sin_ref[...]
    cost, sint = cost_ref[...], sint_ref[...]
    k_scale = RET_DIM ** -0.5

    def retention_head(hd):
        cols = _col_chunks()[hd]
        t = proj(SEG_Q, cols)
        t1, t2 = t[:, :half], t[:, half:]
        q_ref[:, cols.start:cols.start + half] = (t1 * cos - t2 * sin).astype(BF16)
        q_ref[:, cols.start + half:cols.stop] = (t2 * cos + t1 * sin).astype(BF16)
        tt = proj(SEG_K, cols).T
        t1, t2 = tt[:half], tt[half:]
        kt_ref[cols.start:cols.start + half, :] = ((t1 * cost - t2 * sint) * k_scale).astype(BF16)
        kt_ref[cols.start + half:cols.stop, :] = ((t2 * cost + t1 * sint) * k_scale).astype(BF16)
        v_ref[:, cols] = proj(SEG_VR, cols).astype(BF16)
        sg_ref[:, cols] = _silu(proj(SEG_G, cols)).astype(BF16)
        sgb_ref[:, cols] = jax.nn.sigmoid(proj(SEG_GATE_B, cols)).astype(BF16)
        _retention_backward_head(dl_ref, hd, q_ref, kt_ref, v_ref, cb_ref, st_ref)

    for cols in _col_chunks():
        va_ref[:, cols] = _gelu(proj(SEG_V, cols))
    for hd in range(RET_HEADS // 2):
        retention_head(hd)
    va = va_ref[...]
    mu = jnp.mean(va, axis=-1, keepdims=True)
    vc = va - mu
    var = jnp.mean(vc * vc, axis=-1, keepdims=True)
    vn_ref[...] = (vc * lax.rsqrt(var + NORM_EPS) * lng_ref[...] + lnb_ref[...]).astype(BF16)
    for g, cols in enumerate(_col_chunks()):
        u = _gelu(proj(SEG_U, cols))
        for j in range(MIX_ROWS // (SPATIAL_PAIR * CHUNK)):
            rows = slice(j * SPATIAL_PAIR * CHUNK, (j + 1) * SPATIAL_PAIR * CHUNK)
            s = _dot(ws_ref[g], vn_ref[rows, cols]) + bs_ref[g]
            a_ref[rows, cols] = (u[rows] * s).astype(BF16)
    for hd in range(RET_HEADS // 2, RET_HEADS):
        retention_head(hd)
    a = a_ref[...]
    for cols in _col_chunks():
        mixa_ref[:, cols] = (jax.nn.sigmoid(proj(SEG_GATE_A, cols)) * _dot(a, wa_ref[:, cols])).astype(BF16)


def _paired_spatial_weight(w_s):
    eye = jnp.eye(SPATIAL_PAIR, dtype=w_s.dtype)
    paired = jnp.einsum("pq,gcm->gpcqm", eye, w_s)
    return paired.reshape(w_s.shape[0], SPATIAL_PAIR * CHUNK, SPATIAL_PAIR * CHUNK).astype(BF16)


def _mixer_in(decay_logit, x1, norm_w, w_in, b_in, ln_g, ln_b, w_s, b_s, w_a, cos, sin, cos_t, sin_t,
              batch, seq):
    t = x1.shape[0]
    half = RET_DIM // 2
    steps = seq // MIX_ROWS
    tile = lambda b, j: b * steps + steps - 1 - j
    rows = pl.BlockSpec((MIX_ROWS, D_MODEL), lambda b, j: (tile(b, j), 0))
    lanes = pl.BlockSpec((D_MODEL, MIX_ROWS), lambda b, j: (0, tile(b, j)))
    pos = pl.BlockSpec((MIX_ROWS, half), lambda b, j: (steps - 1 - j, 0))
    pos_t = pl.BlockSpec((half, MIX_ROWS), lambda b, j: (0, steps - 1 - j))
    packed = pl.BlockSpec((MIX_ROWS, N_PACK * D_MODEL), lambda b, j: (tile(b, j), 0))
    width = w_in.shape[1]
    out_t = jax.ShapeDtypeStruct((D_MODEL, t), BF16)
    return pl.pallas_call(
        _mixer_in_body,
        grid=(batch, steps),
        in_specs=[pl.BlockSpec(memory_space=pltpu.SMEM),
                  rows, _resident((1, D_MODEL)), _resident((D_MODEL, width)), _resident((1, width)),
                  _resident((1, D_MODEL)), _resident((1, D_MODEL)), _resident(w_s.shape), _resident(b_s.shape),
                  _resident((D_MODEL, D_MODEL)), pos, pos, pos_t, pos_t],
        out_specs=[packed, lanes],
        out_shape=[jax.ShapeDtypeStruct((t, N_PACK * D_MODEL), BF16), out_t],
        scratch_shapes=[pltpu.VMEM((MIX_ROWS, D_MODEL), F32), pltpu.VMEM((MIX_ROWS, D_MODEL), BF16),
                        pltpu.VMEM((MIX_ROWS, D_MODEL), BF16), pltpu.VMEM((RET_HEADS, RET_DIM, RET_DIM), F32)],
        compiler_params=pltpu.CompilerParams(dimension_semantics=("arbitrary", "arbitrary"),
                                             vmem_limit_bytes=56 * MIB),
        name="mixer_in",
    )(decay_logit, x1, norm_w, w_in, b_in, ln_g, ln_b, w_s, b_s, w_a, cos, sin, cos_t, sin_t)


def _mixer_out_body(*refs, final_norm):
    dl_ref, pk_ref, kt_ref, x1_ref, wb_ref, wo_ref, nw_ref, wg_ref, wu_ref, wd_ref = refs[:10]
    fn_ref = refs[10] if final_norm else None
    q_ref, v_ref, sg_ref, sgb_ref, mixa_ref, cb_ref = _packed_views(pk_ref)
    o_ref, st_ref, p_ref, r_ref, mix_ref, x2_ref, act_ref = refs[-7:]
    _zero_state_at_sequence_start(st_ref)
    row_idx = lax.broadcasted_iota(jnp.int32, (RET_BLOCK, RET_DIM), 0).astype(F32)
    lane_idx = lax.broadcasted_iota(jnp.int32, (RET_DIM, RET_BLOCK), 1).astype(F32)
    diff = (lax.broadcasted_iota(jnp.int32, (RET_BLOCK, RET_BLOCK), 0)
            - lax.broadcasted_iota(jnp.int32, (RET_BLOCK, RET_BLOCK), 1)).astype(F32)
    n_blocks = RET_ROWS // RET_BLOCK
    head_cols = lambda hd: slice(hd * RET_DIM, (hd + 1) * RET_DIM)
    block_rows = lambda blk: slice(blk * RET_BLOCK, (blk + 1) * RET_BLOCK)

    for hd in range(RET_HEADS):
        lg_f = _log_gamma(dl_ref[FORWARD, hd], (RET_BLOCK, RET_BLOCK))
        lg_b = _log_gamma(dl_ref[BACKWARD, hd], (RET_BLOCK, RET_BLOCK))
        decay = jnp.exp(jnp.where(diff >= 0, diff * lg_f, -diff * lg_b))
        for blk in range(n_blocks):
            scores = _dot(q_ref[block_rows(blk), head_cols(hd)], kt_ref[head_cols(hd), block_rows(blk)])
            p_ref[hd * n_blocks + blk] = (scores * decay).astype(BF16)

    for hd in range(RET_HEADS):
        cols = head_cols(hd)
        logit = dl_ref[FORWARD, hd]
        q_scale = jnp.exp((row_idx + 1.0) * _log_gamma(logit, (RET_BLOCK, RET_DIM)))
        k_scale = jnp.exp((RET_BLOCK - 1.0 - lane_idx) * _log_gamma(logit, (RET_DIM, RET_BLOCK)))
        block_decay = jnp.exp(RET_BLOCK * _log_gamma(logit, (1, RET_DIM)))
        for blk in range(n_blocks):
            rows = block_rows(blk)
            v = v_ref[rows, cols]
            r = (_dot(p_ref[hd * n_blocks + blk], v)
                 + _dot(q_ref[rows, cols], st_ref[hd].astype(BF16)) * q_scale
                 + cb_ref[rows, cols].astype(F32))
            r = r * lax.rsqrt(jnp.mean(r * r, axis=-1, keepdims=True) + NORM_EPS)
            r_ref[rows, cols] = (r * sg_ref[rows, cols].astype(F32)).astype(BF16)
            _state_update(st_ref, hd, kt_ref[cols, rows], v, k_scale, block_decay)
    r = r_ref[...]
    for cols in _col_chunks():
        mix_ref[:, cols] = (sgb_ref[:, cols].astype(F32) * _dot(r, wb_ref[:, cols])
                            + mixa_ref[:, cols].astype(F32)).astype(BF16)
    mix = mix_ref[...]
    for cols in _col_chunks():
        x2_ref[:, cols] = x1_ref[:, cols] + _dot(mix, wo_ref[:, cols])
    y = _swiglu_half_step(x2_ref[...], nw_ref, wg_ref, wu_ref, wd_ref, act_ref)
    if final_norm:
        y = _rms(y, fn_ref[...])
    o_ref[...] = y


def _mixer_out(decay_logit, packed, kt, x1, w_b, w_o, norm_w, w_gate, w_up, w_down, final_w, batch, seq):
    steps = seq // RET_ROWS
    rows = pl.BlockSpec((RET_ROWS, D_MODEL), lambda b, j: (b * steps + j, 0))
    packed_rows = pl.BlockSpec((RET_ROWS, N_PACK * D_MODEL), lambda b, j: (b * steps + j, 0))
    lanes = pl.BlockSpec((D_MODEL, RET_ROWS), lambda b, j: (0, b * steps + j))
    in_specs = ([pl.BlockSpec(memory_space=pltpu.SMEM), packed_rows, lanes, rows]
                + [_resident((D_MODEL, D_MODEL)), _resident((D_MODEL, D_MODEL))] + _ffn_weight_specs())
    args = [decay_logit, packed, kt, x1, w_b, w_o, norm_w, w_gate, w_up, w_down]
    if final_w is not None:
        in_specs.append(_resident((1, D_MODEL)))
        args.append(final_w)
    return pl.pallas_call(
        functools.partial(_mixer_out_body, final_norm=final_w is not None),
        grid=(batch, steps),
        in_specs=in_specs,
        out_specs=rows,
        out_shape=jax.ShapeDtypeStruct(x1.shape, F32),
        scratch_shapes=[pltpu.VMEM((RET_HEADS, RET_DIM, RET_DIM), F32),
                        pltpu.VMEM((RET_HEADS * (RET_ROWS // RET_BLOCK), RET_BLOCK, RET_BLOCK), BF16),
                        pltpu.VMEM((RET_ROWS, D_MODEL), BF16), pltpu.VMEM((RET_ROWS, D_MODEL), BF16),
                        pltpu.VMEM((RET_ROWS, D_MODEL), F32), pltpu.VMEM((RET_ROWS, D_FF), BF16)],
        compiler_params=pltpu.CompilerParams(dimension_semantics=("arbitrary", "arbitrary"),
                                             vmem_limit_bytes=60 * MIB),
        name="mixer_out",
    )(*args)


def kernel(x, ffn1_norm, ffn1_w_gate, ffn1_w_up, ffn1_w_down, mix_norm, w_in, b_in, sgu_norm_g, sgu_norm_b, sgu_w_s, sgu_b_s, ret_decay_logit, w_branch_a, w_branch_b, w_out, ffn2_norm, ffn2_w_gate, ffn2_w_up, ffn2_w_down, final_norm):
    batch, seq, _ = x.shape
    depth = ffn1_norm.shape[0]
    assert depth >= 1 and seq % RET_ROWS == 0 and seq % MIX_ROWS == 0 and (batch * seq) % FFN_ROWS == 0
    tokens = batch * seq
    row = lambda p: p.reshape(1, -1)
    theta = ROPE_BASE ** (-jnp.arange(0, RET_DIM, 2, dtype=F32) / RET_DIM)

    xt = x.reshape(tokens, D_MODEL)
    for l in range(depth):
        last = l == depth - 1
        x1, (w_in_bf, w_a, w_b, w_o, w_gate2, w_up2, w_down2), (cos, sin, cos_t, sin_t) = _ffn(
            xt, row(ffn1_norm[l]), ffn1_w_gate[l], ffn1_w_up[l], ffn1_w_down[l], theta,
            [w_in[l], w_branch_a[l], w_branch_b[l], w_out[l], ffn2_w_gate[l], ffn2_w_up[l], ffn2_w_down[l]], seq)
        packed, kt = _mixer_in(
            ret_decay_logit[l], x1, row(mix_norm[l]), w_in_bf, row(b_in[l]),
            row(sgu_norm_g[l]), row(sgu_norm_b[l]), _paired_spatial_weight(sgu_w_s[l]),
            jnp.tile(sgu_b_s[l], (1, SPATIAL_PAIR))[..., None], w_a, cos, sin, cos_t, sin_t, batch, seq)
        xt = _mixer_out(ret_decay_logit[l], packed, kt, x1, w_b, w_o,
                        row(ffn2_norm[l]), w_gate2, w_up2, w_down2,
                        row(final_norm) if last else None, batch, seq)
    return xt.reshape(batch, seq, D_MODEL)
```

```python
import functools

import jax
import jax.numpy as jnp
import numpy as np
from jax import lax
from jax.experimental import pallas as pl
from jax.experimental.pallas import tpu as pltpu

D_MODEL = 1024
D_FF = 2816
CHUNK = 128
SPATIAL_PAIR = 2
SGU_GROUPS = 4
SGU_GROUP_DIM = D_MODEL // SGU_GROUPS
RET_HEADS = 4
RET_DIM = 256
SEG_U, SEG_V, SEG_Q, SEG_K, SEG_VR, SEG_G, SEG_GATE_A, SEG_GATE_B = range(8)
FORWARD, BACKWARD = 0, 1
PACK_Q, PACK_V, PACK_SG, PACK_SGB, PACK_MIXA, PACK_CB = range(6)
N_PACK = 6
ROPE_BASE = 10000.0
NORM_EPS = 1e-6

MXU_COLS = 256
BF16_SUBLANES = 16
LANES = 128
(ROW_FFN1_NORM, ROW_MIX_NORM, ROW_LN_G, ROW_LN_B, ROW_FFN2_NORM, ROW_FINAL_NORM, ROW_THETA) = range(7)
ROW_B_IN = 8
N_PARAM_ROWS = 16
FFN_ROWS = 512
MIX_ROWS = 512
RET_ROWS = 512
RET_BLOCK = 256
MIB = 1024 * 1024

BF16 = jnp.bfloat16
F32 = jnp.float32


def _rms(x, g):
    return x * lax.rsqrt(jnp.mean(x * x, axis=-1, keepdims=True) + NORM_EPS) * g


def _gelu(x):
    return 0.5 * x * (1.0 + lax.erf(x * np.float32(np.sqrt(0.5))))


def _silu(x):
    return x * jax.nn.sigmoid(x)


def _dot(a, b):
    return jnp.dot(a, b, preferred_element_type=F32)


def _col_chunks(width=D_MODEL):
    return [slice(c, c + MXU_COLS) for c in range(0, width, MXU_COLS)]


def _packed_views(pk_ref):
    return [pk_ref.at[:, i * D_MODEL:(i + 1) * D_MODEL] for i in range(N_PACK)]


def _param_row(pr_ref, row):
    return pr_ref[row:row + 1, :]


def _resident(shape):
    return pl.BlockSpec(shape, lambda *_: (0,) * len(shape), pipeline_mode=pl.Buffered(1))


def _swiglu_half_step(x, norm_w, wg_ref, wu_ref, wd_ref, act_ref):
    h = (x * norm_w).astype(BF16)
    inv_rms = lax.rsqrt(jnp.mean(x * x, axis=-1, keepdims=True) + NORM_EPS)
    for cols in _col_chunks(D_FF):
        g = _dot(h, wg_ref[:, cols].astype(BF16)) * inv_rms
        u = _dot(h, wu_ref[:, cols].astype(BF16)) * inv_rms
        act_ref[:, cols] = (_silu(g) * u).astype(BF16)
    act = act_ref[...]
    down = jnp.concatenate([_dot(act, wd_ref[:, cols].astype(BF16)) for cols in _col_chunks()], axis=-1)
    return x + 0.5 * down


def _ffn_body(*refs, n_cast, pos_rows, norm_row):
    n = n_cast
    x_ref, pr_ref, wg_ref, wu_ref, wd_ref = refs[:5]
    srcs = refs[5:5 + n]
    o_ref = refs[5 + n]
    dsts = refs[6 + n:6 + 2 * n]
    cos_ref, sin_ref, cost_ref, sint_ref, act_ref, cos0_ref, sin0_ref = refs[6 + 2 * n:]
    step = pl.program_id(0)
    theta = _param_row(pr_ref, ROW_THETA)[:, :RET_DIM // 2]

    @pl.when(step == 0)
    def _():
        ang0 = lax.broadcasted_iota(jnp.int32, (pos_rows, RET_DIM // 2), 0).astype(F32) * theta
        cos0_ref[...] = jnp.cos(ang0)
        sin0_ref[...] = jnp.sin(ang0)

    for src, dst in zip(srcs, dsts):
        dst[...] = src[...].astype(BF16)

    ang1 = (step * pos_rows).astype(F32) * theta
    cos1, sin1 = jnp.cos(ang1), jnp.sin(ang1)
    cos0, sin0 = cos0_ref[...], sin0_ref[...]
    cos = cos1 * cos0 - sin1 * sin0
    sin = sin1 * cos0 + cos1 * sin0
    cos_ref[...] = cos
    sin_ref[...] = sin
    cost_ref[...] = cos.T
    sint_ref[...] = sin.T

    o_ref[...] = _swiglu_half_step(x_ref[...], _param_row(pr_ref, norm_row), wg_ref, wu_ref, wd_ref, act_ref)


def _ffn_weight_specs():
    return [_resident((N_PARAM_ROWS, D_MODEL)), _resident((D_MODEL, D_FF)), _resident((D_MODEL, D_FF)),
            _resident((D_FF, D_MODEL))]


def _cast_rows(n_rows, n_steps):
    for rows in range(BF16_SUBLANES, n_rows + 1, BF16_SUBLANES):
        if n_rows % rows == 0 and n_rows // rows <= n_steps:
            return rows
    raise ValueError(f"cannot split {n_rows} rows over {n_steps} steps")


def _ffn(x, params, norm_row, w_gate, w_up, w_down, later_weights, seq):
    t = x.shape[0]
    n_steps = t // FFN_ROWS
    pos_rows = seq // n_steps
    half = RET_DIM // 2
    assert seq % n_steps == 0 and pos_rows % LANES == 0
    rows = pl.BlockSpec((FFN_ROWS, D_MODEL), lambda i: (i, 0))

    cast_specs, cast_shapes = [], []
    for w in later_weights:
        r = _cast_rows(w.shape[0], n_steps)
        steps = w.shape[0] // r
        cast_specs.append(pl.BlockSpec((r, w.shape[1]), lambda i, last=steps - 1: (jnp.minimum(i, last), 0)))
        cast_shapes.append(jax.ShapeDtypeStruct(w.shape, BF16))
    table = pl.BlockSpec((pos_rows, half), lambda i: (i, 0))
    table_t = pl.BlockSpec((half, pos_rows), lambda i: (0, i))

    outs = pl.pallas_call(
        functools.partial(_ffn_body, n_cast=len(later_weights), pos_rows=pos_rows, norm_row=norm_row),
        grid=(n_steps,),
        in_specs=[rows] + _ffn_weight_specs() + cast_specs,
        out_specs=[rows] + cast_specs + [table, table, table_t, table_t],
        out_shape=[jax.ShapeDtypeStruct((t, D_MODEL), F32)] + cast_shapes
                  + [jax.ShapeDtypeStruct((seq, half), F32)] * 2 + [jax.ShapeDtypeStruct((half, seq), F32)] * 2,
        scratch_shapes=[pltpu.VMEM((FFN_ROWS, D_FF), BF16), pltpu.VMEM((pos_rows, half), F32),
                        pltpu.VMEM((pos_rows, half), F32)],
        compiler_params=pltpu.CompilerParams(dimension_semantics=("arbitrary",), vmem_limit_bytes=60 * MIB),
        name="ffn",
    )(x, params, w_gate, w_up, w_down, *later_weights)
    n = len(later_weights)
    return outs[0], outs[1:1 + n], outs[1 + n:]


def _log_gamma(logit, shape):
    return jax.nn.log_sigmoid(jnp.full(shape, logit, F32))


def _zero_state_at_sequence_start(st_ref):
    @pl.when(pl.program_id(1) == 0)
    def _():
        st_ref[...] = jnp.zeros_like(st_ref)


def _state_update(st_ref, hd, kt, v, k_scale, block_decay):
    kd = (kt.astype(F32) * k_scale).astype(BF16)
    st_ref[hd] = st_ref[hd] * block_decay + _dot(kd, v)


def _retention_backward_head(dl_ref, hd, q_ref, kt_ref, v_ref, o_ref, st_ref):
    cols = slice(hd * RET_DIM, (hd + 1) * RET_DIM)
    row_idx = lax.broadcasted_iota(jnp.int32, (RET_BLOCK, RET_DIM), 0).astype(F32)
    lane_idx = lax.broadcasted_iota(jnp.int32, (RET_DIM, RET_BLOCK), 1).astype(F32)
    logit = dl_ref[BACKWARD, hd]
    q_scale = jnp.exp((RET_BLOCK - row_idx) * _log_gamma(logit, (RET_BLOCK, RET_DIM)))
    k_scale = jnp.exp(lane_idx * _log_gamma(logit, (RET_DIM, RET_BLOCK)))
    block_decay = jnp.exp(RET_BLOCK * _log_gamma(logit, (1, RET_DIM)))
    for blk in reversed(range(q_ref.shape[0] // RET_BLOCK)):
        rows = slice(blk * RET_BLOCK, (blk + 1) * RET_BLOCK)
        o_ref[rows, cols] = (_dot(q_ref[rows, cols], st_ref[hd].astype(BF16)) * q_scale).astype(BF16)
        _state_update(st_ref, hd, kt_ref[cols, rows], v_ref[rows, cols], k_scale, block_decay)


def _mixer_in_body(dl_ref, x_ref, pr_ref, w_ref, ws_ref, bs_ref, wa_ref,
                   cos_ref, sin_ref, cost_ref, sint_ref,
                   pk_ref, kt_ref, va_ref, vn_ref, a_ref, st_ref):
    q_ref, v_ref, sg_ref, sgb_ref, mixa_ref, cb_ref = _packed_views(pk_ref)
    _zero_state_at_sequence_start(st_ref)
    h = _rms(x_ref[...], _param_row(pr_ref, ROW_MIX_NORM)).astype(BF16)
    half = RET_DIM // 2

    def proj(seg, cols):
        wcols = slice(seg * D_MODEL + cols.start, seg * D_MODEL + cols.stop)
        return _dot(h, w_ref[:, wcols]) + pr_ref[ROW_B_IN + seg:ROW_B_IN + seg + 1, cols]

    cos, sin = cos_ref[...], sin_ref[...]
    cost, sint = cost_ref[...], sint_ref[...]
    k_scale = RET_DIM ** -0.5

    def project_head(hd):
        cols = _col_chunks()[hd]
        t = proj(SEG_Q, cols)
        t1, t2 = t[:, :half], t[:, half:]
        q_ref[:, cols.start:cols.start + half] = (t1 * cos - t2 * sin).astype(BF16)
        q_ref[:, cols.start + half:cols.stop] = (t2 * cos + t1 * sin).astype(BF16)
        tt = proj(SEG_K, cols).T
        t1, t2 = tt[:half], tt[half:]
        kt_ref[cols.start:cols.start + half, :] = ((t1 * cost - t2 * sint) * k_scale).astype(BF16)
        kt_ref[cols.start + half:cols.stop, :] = ((t2 * cost + t1 * sint) * k_scale).astype(BF16)
        v_ref[:, cols] = proj(SEG_VR, cols).astype(BF16)
        sg_ref[:, cols] = _silu(proj(SEG_G, cols)).astype(BF16)
        sgb_ref[:, cols] = jax.nn.sigmoid(proj(SEG_GATE_B, cols)).astype(BF16)

    def backward_head(hd):
        _retention_backward_head(dl_ref, hd, q_ref, kt_ref, v_ref, cb_ref, st_ref)

    def layer_norm_gelu_v():
        va = va_ref[...]
        mu = jnp.mean(va, axis=-1, keepdims=True)
        vc = va - mu
        var = jnp.mean(vc * vc, axis=-1, keepdims=True)
        vn_ref[...] = (vc * lax.rsqrt(var + NORM_EPS) * _param_row(pr_ref, ROW_LN_G)
                       + _param_row(pr_ref, ROW_LN_B)).astype(BF16)

    def spatial_gate(g):
        cols = _col_chunks()[g]
        u = _gelu(proj(SEG_U, cols))
        for j in range(MIX_ROWS // (SPATIAL_PAIR * CHUNK)):
            rows = slice(j * SPATIAL_PAIR * CHUNK, (j + 1) * SPATIAL_PAIR * CHUNK)
            s = _dot(ws_ref[g], vn_ref[rows, cols]) + bs_ref[g]
            a_ref[rows, cols] = (u[rows] * s).astype(BF16)

    for cols in _col_chunks():
        va_ref[:, cols] = _gelu(proj(SEG_V, cols))
    for hd in range(RET_HEADS // 2):
        project_head(hd)
        backward_head(hd)
    layer_norm_gelu_v()
    for g in range(SGU_GROUPS):
        spatial_gate(g)
    for hd in range(RET_HEADS // 2, RET_HEADS):
        project_head(hd)
        backward_head(hd)
    a = a_ref[...]
    for cols in _col_chunks():
        mixa_ref[:, cols] = (jax.nn.sigmoid(proj(SEG_GATE_A, cols)) * _dot(a, wa_ref[:, cols])).astype(BF16)


def _paired_spatial_weight(w_s):
    eye = jnp.eye(SPATIAL_PAIR, dtype=w_s.dtype)
    paired = jnp.einsum("pq,gcm->gpcqm", eye, w_s)
    return paired.reshape(w_s.shape[0], SPATIAL_PAIR * CHUNK, SPATIAL_PAIR * CHUNK).astype(BF16)


def _mixer_in(decay_logit, x1, params, w_in, w_s, b_s, w_a, cos, sin, cos_t, sin_t, batch, seq):
    t = x1.shape[0]
    half = RET_DIM // 2
    steps = seq // MIX_ROWS
    tile = lambda b, j: b * steps + steps - 1 - j
    rows = pl.BlockSpec((MIX_ROWS, D_MODEL), lambda b, j: (tile(b, j), 0))
    lanes = pl.BlockSpec((D_MODEL, MIX_ROWS), lambda b, j: (0, tile(b, j)))
    pos = pl.BlockSpec((MIX_ROWS, half), lambda b, j: (steps - 1 - j, 0))
    pos_t = pl.BlockSpec((half, MIX_ROWS), lambda b, j: (0, steps - 1 - j))
    packed = pl.BlockSpec((MIX_ROWS, N_PACK * D_MODEL), lambda b, j: (tile(b, j), 0))
    width = w_in.shape[1]
    out_t = jax.ShapeDtypeStruct((D_MODEL, t), BF16)
    return pl.pallas_call(
        _mixer_in_body,
        grid=(batch, steps),
        in_specs=[pl.BlockSpec(memory_space=pltpu.SMEM),
                  rows, _resident((N_PARAM_ROWS, D_MODEL)), _resident((D_MODEL, width)),
                  _resident(w_s.shape), _resident(b_s.shape),
                  _resident((D_MODEL, D_MODEL)), pos, pos, pos_t, pos_t],
        out_specs=[packed, lanes],
        out_shape=[jax.ShapeDtypeStruct((t, N_PACK * D_MODEL), BF16), out_t],
        scratch_shapes=[pltpu.VMEM((MIX_ROWS, D_MODEL), F32), pltpu.VMEM((MIX_ROWS, D_MODEL), BF16),
                        pltpu.VMEM((MIX_ROWS, D_MODEL), BF16), pltpu.VMEM((RET_HEADS, RET_DIM, RET_DIM), F32)],
        compiler_params=pltpu.CompilerParams(dimension_semantics=("arbitrary", "arbitrary"),
                                             vmem_limit_bytes=56 * MIB),
        name="mixer_in",
    )(decay_logit, x1, params, w_in, w_s, b_s, w_a, cos, sin, cos_t, sin_t)


def _mixer_out_body(*refs, final_norm):
    dl_ref, pk_ref, kt_ref, x1_ref, wb_ref, wo_ref, pr_ref, wg_ref, wu_ref, wd_ref = refs[:10]
    q_ref, v_ref, sg_ref, sgb_ref, mixa_ref, cb_ref = _packed_views(pk_ref)
    o_ref, st_ref, p_ref, r_ref, mix_ref, x2_ref, act_ref = refs[-7:]
    _zero_state_at_sequence_start(st_ref)
    row_idx = lax.broadcasted_iota(jnp.int32, (RET_BLOCK, RET_DIM), 0).astype(F32)
    lane_idx = lax.broadcasted_iota(jnp.int32, (RET_DIM, RET_BLOCK), 1).astype(F32)
    diff = (lax.broadcasted_iota(jnp.int32, (RET_BLOCK, RET_BLOCK), 0)
            - lax.broadcasted_iota(jnp.int32, (RET_BLOCK, RET_BLOCK), 1)).astype(F32)
    n_blocks = RET_ROWS // RET_BLOCK
    head_cols = lambda hd: slice(hd * RET_DIM, (hd + 1) * RET_DIM)
    block_rows = lambda blk: slice(blk * RET_BLOCK, (blk + 1) * RET_BLOCK)

    for hd in range(RET_HEADS):
        lg_f = _log_gamma(dl_ref[FORWARD, hd], (RET_BLOCK, RET_BLOCK))
        lg_b = _log_gamma(dl_ref[BACKWARD, hd], (RET_BLOCK, RET_BLOCK))
        decay = jnp.exp(jnp.where(diff >= 0, diff * lg_f, -diff * lg_b))
        for blk in range(n_blocks):
            scores = _dot(q_ref[block_rows(blk), head_cols(hd)], kt_ref[head_cols(hd), block_rows(blk)])
            p_ref[hd * n_blocks + blk] = (scores * decay).astype(BF16)

    for hd in range(RET_HEADS):
        cols = head_cols(hd)
        logit = dl_ref[FORWARD, hd]
        q_scale = jnp.exp((row_idx + 1.0) * _log_gamma(logit, (RET_BLOCK, RET_DIM)))
        k_scale = jnp.exp((RET_BLOCK - 1.0 - lane_idx) * _log_gamma(logit, (RET_DIM, RET_BLOCK)))
        block_decay = jnp.exp(RET_BLOCK * _log_gamma(logit, (1, RET_DIM)))
        for blk in range(n_blocks):
            rows = block_rows(blk)
            v = v_ref[rows, cols]
            r = (_dot(p_ref[hd * n_blocks + blk], v)
                 + _dot(q_ref[rows, cols], st_ref[hd].astype(BF16)) * q_scale
                 + cb_ref[rows, cols].astype(F32))
            r = r * lax.rsqrt(jnp.mean(r * r, axis=-1, keepdims=True) + NORM_EPS)
            r_ref[rows, cols] = (r * sg_ref[rows, cols].astype(F32)).astype(BF16)
            _state_update(st_ref, hd, kt_ref[cols, rows], v, k_scale, block_decay)
    r = r_ref[...]
    for cols in _col_chunks():
        mix_ref[:, cols] = (sgb_ref[:, cols].astype(F32) * _dot(r, wb_ref[:, cols])
                            + mixa_ref[:, cols].astype(F32)).astype(BF16)
    mix = mix_ref[...]
    for cols in _col_chunks():
        x2_ref[:, cols] = x1_ref[:, cols] + _dot(mix, wo_ref[:, cols])
    y = _swiglu_half_step(x2_ref[...], _param_row(pr_ref, ROW_FFN2_NORM), wg_ref, wu_ref, wd_ref, act_ref)
    if final_norm:
        y = _rms(y, _param_row(pr_ref, ROW_FINAL_NORM))
    o_ref[...] = y


def _mixer_out(decay_logit, packed, kt, x1, w_b, w_o, params, w_gate, w_up, w_down, final_norm, batch, seq):
    steps = seq // RET_ROWS
    rows = pl.BlockSpec((RET_ROWS, D_MODEL), lambda b, j: (b * steps + j, 0))
    packed_rows = pl.BlockSpec((RET_ROWS, N_PACK * D_MODEL), lambda b, j: (b * steps + j, 0))
    lanes = pl.BlockSpec((D_MODEL, RET_ROWS), lambda b, j: (0, b * steps + j))
    in_specs = ([pl.BlockSpec(memory_space=pltpu.SMEM), packed_rows, lanes, rows]
                + [_resident((D_MODEL, D_MODEL)), _resident((D_MODEL, D_MODEL))] + _ffn_weight_specs())
    args = [decay_logit, packed, kt, x1, w_b, w_o, params, w_gate, w_up, w_down]
    return pl.pallas_call(
        functools.partial(_mixer_out_body, final_norm=final_norm),
        grid=(batch, steps),
        in_specs=in_specs,
        out_specs=rows,
        out_shape=jax.ShapeDtypeStruct(x1.shape, F32),
        scratch_shapes=[pltpu.VMEM((RET_HEADS, RET_DIM, RET_DIM), F32),
                        pltpu.VMEM((RET_HEADS * (RET_ROWS // RET_BLOCK), RET_BLOCK, RET_BLOCK), BF16),
                        pltpu.VMEM((RET_ROWS, D_MODEL), BF16), pltpu.VMEM((RET_ROWS, D_MODEL), BF16),
                        pltpu.VMEM((RET_ROWS, D_MODEL), F32), pltpu.VMEM((RET_ROWS, D_FF), BF16)],
        compiler_params=pltpu.CompilerParams(dimension_semantics=("arbitrary", "arbitrary"),
                                             vmem_limit_bytes=60 * MIB),
        name="mixer_out",
    )(*args)


def kernel(x, ffn1_norm, ffn1_w_gate, ffn1_w_up, ffn1_w_down, mix_norm, w_in, b_in, sgu_norm_g, sgu_norm_b, sgu_w_s, sgu_b_s, ret_decay_logit, w_branch_a, w_branch_b, w_out, ffn2_norm, ffn2_w_gate, ffn2_w_up, ffn2_w_down, final_norm):
    batch, seq, _ = x.shape
    depth = ffn1_norm.shape[0]
    assert depth >= 1 and seq % RET_ROWS == 0 and seq % MIX_ROWS == 0 and (batch * seq) % FFN_ROWS == 0
    tokens = batch * seq
    theta = ROPE_BASE ** (-jnp.arange(0, RET_DIM, 2, dtype=F32) / RET_DIM)

    xt = x.reshape(tokens, D_MODEL)
    for l in range(depth):
        rows = [ffn1_norm[l], mix_norm[l], sgu_norm_g[l], sgu_norm_b[l], ffn2_norm[l], final_norm,
                jnp.pad(theta, (0, D_MODEL - theta.shape[0]))]
        params = jnp.zeros((N_PARAM_ROWS, D_MODEL), F32)
        params = params.at[:len(rows)].set(jnp.stack(rows)).at[ROW_B_IN:].set(b_in[l].reshape(-1, D_MODEL))
        x1, (w_in_bf, w_a, w_b, w_o, w_gate2, w_up2, w_down2), (cos, sin, cos_t, sin_t) = _ffn(
            xt, params, ROW_FFN1_NORM, ffn1_w_gate[l], ffn1_w_up[l], ffn1_w_down[l],
            [w_in[l], w_branch_a[l], w_branch_b[l], w_out[l], ffn2_w_gate[l], ffn2_w_up[l], ffn2_w_down[l]], seq)
        packed, kt = _mixer_in(
            ret_decay_logit[l], x1, params, w_in_bf, _paired_spatial_weight(sgu_w_s[l]),
            jnp.tile(sgu_b_s[l], (1, SPATIAL_PAIR))[..., None], w_a, cos, sin, cos_t, sin_t, batch, seq)
        xt = _mixer_out(ret_decay_logit[l], packed, kt, x1, w_b, w_o, params, w_gate2, w_up2, w_down2,
                        l == depth - 1, batch, seq)
    return xt.reshape(batch, seq, D_MODEL)
```

```python
import functools

import jax
import jax.numpy as jnp
import numpy as np
from jax import lax
from jax.experimental import pallas as pl
from jax.experimental.pallas import tpu as pltpu

D_MODEL = 1024
D_FF = 2816
CHUNK = 128
SPATIAL_PAIR = 2
SGU_GROUPS = 4
SGU_GROUP_DIM = D_MODEL // SGU_GROUPS
RET_HEADS = 4
RET_DIM = 256
SEG_U, SEG_V, SEG_Q, SEG_K, SEG_VR, SEG_G, SEG_GATE_A, SEG_GATE_B = range(8)
FORWARD, BACKWARD = 0, 1
PACK_Q, PACK_V, PACK_SG, PACK_SGB, PACK_MIXA, PACK_CB = range(6)
N_PACK = 6
ROPE_BASE = 10000.0
NORM_EPS = 1e-6

MXU_COLS = 256
BF16_SUBLANES = 16
LANES = 128
FFN_ROWS = 512
MIX_ROWS = 512
RET_ROWS = 512
RET_BLOCK = 256
MIB = 1024 * 1024

BF16 = jnp.bfloat16
F32 = jnp.float32

assert RET_DIM == MXU_COLS and SGU_GROUP_DIM == MXU_COLS


def _rms(x, g):
    return x * lax.rsqrt(jnp.mean(x * x, axis=-1, keepdims=True) + NORM_EPS) * g


def _gelu(x):
    return 0.5 * x * (1.0 + lax.erf(x * np.float32(np.sqrt(0.5))))


def _silu(x):
    return x * jax.nn.sigmoid(x)


def _dot(a, b):
    return jnp.dot(a, b, preferred_element_type=F32)


def _col_chunks(width=D_MODEL):
    return [slice(c, c + MXU_COLS) for c in range(0, width, MXU_COLS)]


def _packed_views(pk_ref):
    return [pk_ref.at[:, i * D_MODEL:(i + 1) * D_MODEL] for i in range(N_PACK)]


def _resident(shape):
    return pl.BlockSpec(shape, lambda *_: (0,) * len(shape), pipeline_mode=pl.Buffered(1))


def _swiglu_half_step(x, nw_ref, wg_ref, wu_ref, wd_ref, act_ref):
    h = (x * nw_ref[...]).astype(BF16)
    inv_rms = lax.rsqrt(jnp.mean(x * x, axis=-1, keepdims=True) + NORM_EPS)
    for cols in _col_chunks(D_FF):
        g = _dot(h, wg_ref[:, cols].astype(BF16)) * inv_rms
        u = _dot(h, wu_ref[:, cols].astype(BF16)) * inv_rms
        act_ref[:, cols] = (_silu(g) * u).astype(BF16)
    act = act_ref[...]
    down = jnp.concatenate([_dot(act, wd_ref[:, cols].astype(BF16)) for cols in _col_chunks()], axis=-1)
    return x + 0.5 * down


def _ffn_body(*refs, n_cast, pos_rows):
    n = n_cast
    x_ref, nw_ref, wg_ref, wu_ref, wd_ref, theta_ref = refs[:6]
    srcs = refs[6:6 + n]
    o_ref = refs[6 + n]
    dsts = refs[7 + n:7 + 2 * n]
    cos_ref, sin_ref, cost_ref, sint_ref, act_ref, cos0_ref, sin0_ref = refs[7 + 2 * n:]
    step = pl.program_id(0)
    theta = theta_ref[...]

    @pl.when(step == 0)
    def _():
        ang0 = lax.broadcasted_iota(jnp.int32, (pos_rows, RET_DIM // 2), 0).astype(F32) * theta
        cos0_ref[...] = jnp.cos(ang0)
        sin0_ref[...] = jnp.sin(ang0)

    for src, dst in zip(srcs, dsts):
        dst[...] = src[...].astype(BF16)

    ang1 = (step * pos_rows).astype(F32) * theta
    cos1, sin1 = jnp.cos(ang1), jnp.sin(ang1)
    cos0, sin0 = cos0_ref[...], sin0_ref[...]
    cos = cos1 * cos0 - sin1 * sin0
    sin = sin1 * cos0 + cos1 * sin0
    cos_ref[...] = cos
    sin_ref[...] = sin
    cost_ref[...] = cos.T
    sint_ref[...] = sin.T

    o_ref[...] = _swiglu_half_step(x_ref[...], nw_ref, wg_ref, wu_ref, wd_ref, act_ref)


def _ffn_weight_specs():
    return [_resident((1, D_MODEL)), _resident((D_MODEL, D_FF)), _resident((D_MODEL, D_FF)),
            _resident((D_FF, D_MODEL))]


def _cast_rows(n_rows, n_steps):
    for rows in range(BF16_SUBLANES, n_rows + 1, BF16_SUBLANES):
        if n_rows % rows == 0 and n_rows // rows <= n_steps:
            return rows
    raise ValueError(f"cannot split {n_rows} rows over {n_steps} steps")


def _ffn(x, norm_w, w_gate, w_up, w_down, theta, later_weights, seq):
    t = x.shape[0]
    n_steps = t // FFN_ROWS
    pos_rows = seq // n_steps
    half = RET_DIM // 2
    assert seq % n_steps == 0 and pos_rows % LANES == 0
    rows = pl.BlockSpec((FFN_ROWS, D_MODEL), lambda i: (i, 0))

    cast_specs, cast_shapes = [], []
    for w in later_weights:
        r = _cast_rows(w.shape[0], n_steps)
        steps = w.shape[0] // r
        cast_specs.append(pl.BlockSpec((r, w.shape[1]), lambda i, last=steps - 1: (jnp.minimum(i, last), 0)))
        cast_shapes.append(jax.ShapeDtypeStruct(w.shape, BF16))
    table = pl.BlockSpec((pos_rows, half), lambda i: (i, 0))
    table_t = pl.BlockSpec((half, pos_rows), lambda i: (0, i))

    outs = pl.pallas_call(
        functools.partial(_ffn_body, n_cast=len(later_weights), pos_rows=pos_rows),
        grid=(n_steps,),
        in_specs=[rows] + _ffn_weight_specs() + [_resident((1, half))] + cast_specs,
        out_specs=[rows] + cast_specs + [table, table, table_t, table_t],
        out_shape=[jax.ShapeDtypeStruct((t, D_MODEL), F32)] + cast_shapes
                  + [jax.ShapeDtypeStruct((seq, half), F32)] * 2 + [jax.ShapeDtypeStruct((half, seq), F32)] * 2,
        scratch_shapes=[pltpu.VMEM((FFN_ROWS, D_FF), BF16), pltpu.VMEM((pos_rows, half), F32),
                        pltpu.VMEM((pos_rows, half), F32)],
        compiler_params=pltpu.CompilerParams(dimension_semantics=("arbitrary",), vmem_limit_bytes=60 * MIB),
        name="ffn",
    )(x, norm_w, w_gate, w_up, w_down, theta.reshape(1, half), *later_weights)
    n = len(later_weights)
    return outs[0], outs[1:1 + n], outs[1 + n:]


def _log_gamma(logit, shape):
    return jax.nn.log_sigmoid(jnp.full(shape, logit, F32))


def _zero_state_at_sequence_start(st_ref):
    @pl.when(pl.program_id(1) == 0)
    def _():
        st_ref[...] = jnp.zeros_like(st_ref)


def _state_update(st_ref, hd, kt, v, k_scale, block_decay):
    kd = (kt.astype(F32) * k_scale).astype(BF16)
    st_ref[hd] = st_ref[hd] * block_decay + _dot(kd, v)


def _retention_backward_head(dl_ref, hd, q_ref, kt_ref, v_ref, o_ref, st_ref):
    cols = slice(hd * RET_DIM, (hd + 1) * RET_DIM)
    row_idx = lax.broadcasted_iota(jnp.int32, (RET_BLOCK, RET_DIM), 0).astype(F32)
    lane_idx = lax.broadcasted_iota(jnp.int32, (RET_DIM, RET_BLOCK), 1).astype(F32)
    logit = dl_ref[BACKWARD, hd]
    q_scale = jnp.exp((RET_BLOCK - row_idx) * _log_gamma(logit, (RET_BLOCK, RET_DIM)))
    k_scale = jnp.exp(lane_idx * _log_gamma(logit, (RET_DIM, RET_BLOCK)))
    block_decay = jnp.exp(RET_BLOCK * _log_gamma(logit, (1, RET_DIM)))
    for blk in reversed(range(q_ref.shape[0] // RET_BLOCK)):
        rows = slice(blk * RET_BLOCK, (blk + 1) * RET_BLOCK)
        o_ref[rows, cols] = (_dot(q_ref[rows, cols], st_ref[hd].astype(BF16)) * q_scale).astype(BF16)
        _state_update(st_ref, hd, kt_ref[cols, rows], v_ref[rows, cols], k_scale, block_decay)


def _mixer_in_body(dl_ref, x_ref, nw_ref, w_ref, b_ref, lng_ref, lnb_ref, ws_ref, bs_ref, wa_ref,
                   cos_ref, sin_ref, cost_ref, sint_ref,
                   pk_ref, kt_ref, va_ref, vn_ref, a_ref, st_ref):
    q_ref, v_ref, sg_ref, sgb_ref, mixa_ref, cb_ref = _packed_views(pk_ref)
    _zero_state_at_sequence_start(st_ref)
    h = _rms(x_ref[...], nw_ref[...]).astype(BF16)
    half = RET_DIM // 2

    def proj(seg, cols):
        wcols = slice(seg * D_MODEL + cols.start, seg * D_MODEL + cols.stop)
        return _dot(h, w_ref[:, wcols]) + b_ref[:, wcols]

    cos, sin = cos_ref[...], sin_ref[...]
    cost, sint = cost_ref[...], sint_ref[...]
    k_scale = RET_DIM ** -0.5

    def retention_head(hd):
        cols = _col_chunks()[hd]
        t = proj(SEG_Q, cols)
        t1, t2 = t[:, :half], t[:, half:]
        q_ref[:, cols.start:cols.start + half] = (t1 * cos - t2 * sin).astype(BF16)
        q_ref[:, cols.start + half:cols.stop] = (t2 * cos + t1 * sin).astype(BF16)
        tt = proj(SEG_K, cols).T
        t1, t2 = tt[:half], tt[half:]
        kt_ref[cols.start:cols.start + half, :] = ((t1 * cost - t2 * sint) * k_scale).astype(BF16)
        kt_ref[cols.start + half:cols.stop, :] = ((t2 * cost + t1 * sint) * k_scale).astype(BF16)
        v_ref[:, cols] = proj(SEG_VR, cols).astype(BF16)
        sg_ref[:, cols] = _silu(proj(SEG_G, cols)).astype(BF16)
        sgb_ref[:, cols] = jax.nn.sigmoid(proj(SEG_GATE_B, cols)).astype(BF16)
        _retention_backward_head(dl_ref, hd, q_ref, kt_ref, v_ref, cb_ref, st_ref)

    for cols in _col_chunks():
        va_ref[:, cols] = _gelu(proj(SEG_V, cols))
    for hd in range(RET_HEADS // 2):
        retention_head(hd)
    va = va_ref[...]
    mu = jnp.mean(va, axis=-1, keepdims=True)
    vc = va - mu
    var = jnp.mean(vc * vc, axis=-1, keepdims=True)
    vn_ref[...] = (vc * lax.rsqrt(var + NORM_EPS) * lng_ref[...] + lnb_ref[...]).astype(BF16)
    for g, cols in enumerate(_col_chunks()):
        u = _gelu(proj(SEG_U, cols))
        for j in range(MIX_ROWS // (SPATIAL_PAIR * CHUNK)):
            rows = slice(j * SPATIAL_PAIR * CHUNK, (j + 1) * SPATIAL_PAIR * CHUNK)
            s = _dot(ws_ref[g], vn_ref[rows, cols]) + bs_ref[g]
            a_ref[rows, cols] = (u[rows] * s).astype(BF16)
    for hd in range(RET_HEADS // 2, RET_HEADS):
        retention_head(hd)
    a = a_ref[...]
    for cols in _col_chunks():
        mixa_ref[:, cols] = (jax.nn.sigmoid(proj(SEG_GATE_A, cols)) * _dot(a, wa_ref[:, cols])).astype(BF16)


def _paired_spatial_weight(w_s):
    eye = jnp.eye(SPATIAL_PAIR, dtype=w_s.dtype)
    paired = jnp.einsum("pq,gcm->gpcqm", eye, w_s)
    return paired.reshape(w_s.shape[0], SPATIAL_PAIR * CHUNK, SPATIAL_PAIR * CHUNK).astype(BF16)


def _mixer_in(decay_logit, x1, norm_w, w_in, b_in, ln_g, ln_b, w_s, b_s, w_a, cos, sin, cos_t, sin_t,
              batch, seq):
    t = x1.shape[0]
    half = RET_DIM // 2
    steps = seq // MIX_ROWS
    tile = lambda b, j: b * steps + steps - 1 - j
    rows = pl.BlockSpec((MIX_ROWS, D_MODEL), lambda b, j: (tile(b, j), 0))
    lanes = pl.BlockSpec((D_MODEL, MIX_ROWS), lambda b, j: (0, tile(b, j)))
    pos = pl.BlockSpec((MIX_ROWS, half), lambda b, j: (steps - 1 - j, 0))
    pos_t = pl.BlockSpec((half, MIX_ROWS), lambda b, j: (0, steps - 1 - j))
    packed = pl.BlockSpec((MIX_ROWS, N_PACK * D_MODEL), lambda b, j: (tile(b, j), 0))
    width = w_in.shape[1]
    out_t = jax.ShapeDtypeStruct((D_MODEL, t), BF16)
    return pl.pallas_call(
        _mixer_in_body,
        grid=(batch, steps),
        in_specs=[pl.BlockSpec(memory_space=pltpu.SMEM),
                  rows, _resident((1, D_MODEL)), _resident((D_MODEL, width)), _resident((1, width)),
                  _resident((1, D_MODEL)), _resident((1, D_MODEL)), _resident(w_s.shape), _resident(b_s.shape),
                  _resident((D_MODEL, D_MODEL)), pos, pos, pos_t, pos_t],
        out_specs=[packed, lanes],
        out_shape=[jax.ShapeDtypeStruct((t, N_PACK * D_MODEL), BF16), out_t],
        scratch_shapes=[pltpu.VMEM((MIX_ROWS, D_MODEL), F32), pltpu.VMEM((MIX_ROWS, D_MODEL), BF16),
                        pltpu.VMEM((MIX_ROWS, D_MODEL), BF16), pltpu.VMEM((RET_HEADS, RET_DIM, RET_DIM), F32)],
        compiler_params=pltpu.CompilerParams(dimension_semantics=("arbitrary", "arbitrary"),
                                             vmem_limit_bytes=56 * MIB),
        name="mixer_in",
    )(decay_logit, x1, norm_w, w_in, b_in, ln_g, ln_b, w_s, b_s, w_a, cos, sin, cos_t, sin_t)


def _mixer_out_body(*refs, final_norm):
    dl_ref, pk_ref, kt_ref, x1_ref, wb_ref, wo_ref, nw_ref, wg_ref, wu_ref, wd_ref = refs[:10]
    fn_ref = refs[10] if final_norm else None
    q_ref, v_ref, sg_ref, sgb_ref, mixa_ref, cb_ref = _packed_views(pk_ref)
    o_ref, st_ref, p_ref, r_ref, mix_ref, x2_ref, act_ref = refs[-7:]
    _zero_state_at_sequence_start(st_ref)
    row_idx = lax.broadcasted_iota(jnp.int32, (RET_BLOCK, RET_DIM), 0).astype(F32)
    lane_idx = lax.broadcasted_iota(jnp.int32, (RET_DIM, RET_BLOCK), 1).astype(F32)
    diff = (lax.broadcasted_iota(jnp.int32, (RET_BLOCK, RET_BLOCK), 0)
            - lax.broadcasted_iota(jnp.int32, (RET_BLOCK, RET_BLOCK), 1)).astype(F32)
    n_blocks = RET_ROWS // RET_BLOCK
    head_cols = lambda hd: slice(hd * RET_DIM, (hd + 1) * RET_DIM)
    block_rows = lambda blk: slice(blk * RET_BLOCK, (blk + 1) * RET_BLOCK)

    for hd in range(RET_HEADS):
        lg_f = _log_gamma(dl_ref[FORWARD, hd], (RET_BLOCK, RET_BLOCK))
        lg_b = _log_gamma(dl_ref[BACKWARD, hd], (RET_BLOCK, RET_BLOCK))
        decay = jnp.exp(jnp.where(diff >= 0, diff * lg_f, -diff * lg_b))
        for blk in range(n_blocks):
            scores = _dot(q_ref[block_rows(blk), head_cols(hd)], kt_ref[head_cols(hd), block_rows(blk)])
            p_ref[hd * n_blocks + blk] = (scores * decay).astype(BF16)

    for hd in range(RET_HEADS):
        cols = head_cols(hd)
        logit = dl_ref[FORWARD, hd]
        q_scale = jnp.exp((row_idx + 1.0) * _log_gamma(logit, (RET_BLOCK, RET_DIM)))
        k_scale = jnp.exp((RET_BLOCK - 1.0 - lane_idx) * _log_gamma(logit, (RET_DIM, RET_BLOCK)))
        block_decay = jnp.exp(RET_BLOCK * _log_gamma(logit, (1, RET_DIM)))
        for blk in range(n_blocks):
            rows = block_rows(blk)
            v = v_ref[rows, cols]
            r = (_dot(p_ref[hd * n_blocks + blk], v)
                 + _dot(q_ref[rows, cols], st_ref[hd].astype(BF16)) * q_scale
                 + cb_ref[rows, cols].astype(F32))
            r = r * lax.rsqrt(jnp.mean(r * r, axis=-1, keepdims=True) + NORM_EPS)
            r_ref[rows, cols] = (r * sg_ref[rows, cols].astype(F32)).astype(BF16)
            _state_update(st_ref, hd, kt_ref[cols, rows], v, k_scale, block_decay)
    r = r_ref[...]
    for cols in _col_chunks():
        mix_ref[:, cols] = (sgb_ref[:, cols].astype(F32) * _dot(r, wb_ref[:, cols])
                            + mixa_ref[:, cols].astype(F32)).astype(BF16)
    mix = mix_ref[...]
    for cols in _col_chunks():
        x2_ref[:, cols] = x1_ref[:, cols] + _dot(mix, wo_ref[:, cols])
    y = _swiglu_half_step(x2_ref[...], nw_ref, wg_ref, wu_ref, wd_ref, act_ref)
    if final_norm:
        y = _rms(y, fn_ref[...])
    o_ref[...] = y


def _mixer_out(decay_logit, packed, kt, x1, w_b, w_o, norm_w, w_gate, w_up, w_down, final_w, batch, seq):
    steps = seq // RET_ROWS
    rows = pl.BlockSpec((RET_ROWS, D_MODEL), lambda b, j: (b * steps + j, 0))
    packed_rows = pl.BlockSpec((RET_ROWS, N_PACK * D_MODEL), lambda b, j: (b * steps + j, 0))
    lanes = pl.BlockSpec((D_MODEL, RET_ROWS), lambda b, j: (0, b * steps + j))
    in_specs = ([pl.BlockSpec(memory_space=pltpu.SMEM), packed_rows, lanes, rows]
                + [_resident((D_MODEL, D_MODEL)), _resident((D_MODEL, D_MODEL))] + _ffn_weight_specs())
    args = [decay_logit, packed, kt, x1, w_b, w_o, norm_w, w_gate, w_up, w_down]
    if final_w is not None:
        in_specs.append(_resident((1, D_MODEL)))
        args.append(final_w)
    return pl.pallas_call(
        functools.partial(_mixer_out_body, final_norm=final_w is not None),
        grid=(batch, steps),
        in_specs=in_specs,
        out_specs=rows,
        out_shape=jax.ShapeDtypeStruct(x1.shape, F32),
        scratch_shapes=[pltpu.VMEM((RET_HEADS, RET_DIM, RET_DIM), F32),
                        pltpu.VMEM((RET_HEADS * (RET_ROWS // RET_BLOCK), RET_BLOCK, RET_BLOCK), BF16),
                        pltpu.VMEM((RET_ROWS, D_MODEL), BF16), pltpu.VMEM((RET_ROWS, D_MODEL), BF16),
                        pltpu.VMEM((RET_ROWS, D_MODEL), F32), pltpu.VMEM((RET_ROWS, D_FF), BF16)],
        compiler_params=pltpu.CompilerParams(dimension_semantics=("arbitrary", "arbitrary"),
                                             vmem_limit_bytes=60 * MIB),
        name="mixer_out",
    )(*args)


def kernel(x, ffn1_norm, ffn1_w_gate, ffn1_w_up, ffn1_w_down, mix_norm, w_in, b_in, sgu_norm_g, sgu_norm_b, sgu_w_s, sgu_b_s, ret_decay_logit, w_branch_a, w_branch_b, w_out, ffn2_norm, ffn2_w_gate, ffn2_w_up, ffn2_w_down, final_norm):
    batch, seq, _ = x.shape
    depth = ffn1_norm.shape[0]
    assert depth >= 1 and seq % RET_ROWS == 0 and seq % MIX_ROWS == 0 and (batch * seq) % FFN_ROWS == 0
    tokens = batch * seq
    row = lambda p: p.reshape(1, -1)
    theta = ROPE_BASE ** (-jnp.arange(0, RET_DIM, 2, dtype=F32) / RET_DIM)

    xt = x.reshape(tokens, D_MODEL)
    for l in range(depth):
        last = l == depth - 1
        x1, (w_in_bf, w_a, w_b, w_o, w_gate2, w_up2, w_down2), (cos, sin, cos_t, sin_t) = _ffn(
            xt, row(ffn1_norm[l]), ffn1_w_gate[l], ffn1_w_up[l], ffn1_w_down[l], theta,
            [w_in[l], w_branch_a[l], w_branch_b[l], w_out[l], ffn2_w_gate[l], ffn2_w_up[l], ffn2_w_down[l]], seq)
        packed, kt = _mixer_in(
            ret_decay_logit[l], x1, row(mix_norm[l]), w_in_bf, row(b_in[l]),
            row(sgu_norm_g[l]), row(sgu_norm_b[l]), _paired_spatial_weight(sgu_w_s[l]),
            jnp.tile(sgu_b_s[l], (1, SPATIAL_PAIR))[..., None], w_a, cos, sin, cos_t, sin_t, batch, seq)
        xt = _mixer_out(ret_decay_logit[l], packed, kt, x1, w_b, w_o,
                        row(ffn2_norm[l]), w_gate2, w_up2, w_down2,
                        row(final_norm) if last else None, batch, seq)
    return xt.reshape(batch, seq, D_MODEL)
```

```python
import functools

import jax
import jax.numpy as jnp
import numpy as np
from jax import lax
from jax.experimental import pallas as pl
from jax.experimental.pallas import tpu as pltpu

D_MODEL = 1024
D_FF = 2816
CHUNK = 128
SPATIAL_PAIR = 2
SGU_GROUPS = 4
SGU_GROUP_DIM = D_MODEL // SGU_GROUPS
RET_HEADS = 4
RET_DIM = 256
SEG_U, SEG_V, SEG_Q, SEG_K, SEG_VR, SEG_G, SEG_GATE_A, SEG_GATE_B = range(8)
FORWARD, BACKWARD = 0, 1
PACK_Q, PACK_V, PACK_SG, PACK_SGB, PACK_MIXA, PACK_CB = range(6)
N_PACK = 6
ROPE_BASE = 10000.0
NORM_EPS = 1e-6

MXU_COLS = 256
BF16_SUBLANES = 16
LANES = 128
FFN_ROWS = 512
MIX_ROWS = 512
RET_ROWS = 512
RET_BLOCK = 256
MIB = 1024 * 1024

BF16 = jnp.bfloat16
F32 = jnp.float32

assert RET_DIM == MXU_COLS and SGU_GROUP_DIM == MXU_COLS


def _rms(x, g):
    return x * lax.rsqrt(jnp.mean(x * x, axis=-1, keepdims=True) + NORM_EPS) * g


def _gelu(x):
    return 0.5 * x * (1.0 + lax.erf(x * np.float32(np.sqrt(0.5))))


def _silu(x):
    return x * jax.nn.sigmoid(x)


def _dot(a, b):
    return jnp.dot(a, b, preferred_element_type=F32)


def _col_chunks(width=D_MODEL):
    return [slice(c, c + MXU_COLS) for c in range(0, width, MXU_COLS)]


def _packed_views(pk_ref):
    return [pk_ref.at[:, i * D_MODEL:(i + 1) * D_MODEL] for i in range(N_PACK)]


def _resident(shape):
    return pl.BlockSpec(shape, lambda *_: (0,) * len(shape), pipeline_mode=pl.Buffered(1))


def _swiglu_half_step(x, nw_ref, wg_ref, wu_ref, wd_ref, act_ref):
    h = (x * nw_ref[...]).astype(BF16)
    inv_rms = lax.rsqrt(jnp.mean(x * x, axis=-1, keepdims=True) + NORM_EPS)
    for cols in _col_chunks(D_FF):
        g = _dot(h, wg_ref[:, cols].astype(BF16)) * inv_rms
        u = _dot(h, wu_ref[:, cols].astype(BF16)) * inv_rms
        act_ref[:, cols] = (_silu(g) * u).astype(BF16)
    act = act_ref[...]
    down = jnp.concatenate([_dot(act, wd_ref[:, cols].astype(BF16)) for cols in _col_chunks()], axis=-1)
    return x + 0.5 * down


def _ffn_body(*refs, n_cast, pos_rows, stream_in_specs, stream_out_specs, n_steps):
    n = n_cast
    x_hbm, nw_ref, wg_ref, wu_ref, wd_ref, theta_ref = refs[:6]
    streamed_in = (x_hbm,) + refs[6:6 + n]
    streamed_out = refs[6 + n:11 + 2 * n]
    act_ref, cos0_ref, sin0_ref = refs[11 + 2 * n:]
    theta = theta_ref[...]

    def step_body(indices, x_ref, *tiles):
        srcs = tiles[:n]
        o_ref = tiles[n]
        dsts = tiles[n + 1:2 * n + 1]
        cos_ref, sin_ref, cost_ref, sint_ref = tiles[2 * n + 1:]
        step = indices[0]

        @pl.when(step == 0)
        def _():
            ang0 = lax.broadcasted_iota(jnp.int32, (pos_rows, RET_DIM // 2), 0).astype(F32) * theta
            cos0_ref[...] = jnp.cos(ang0)
            sin0_ref[...] = jnp.sin(ang0)

        for src, dst in zip(srcs, dsts):
            dst[...] = src[...].astype(BF16)

        ang1 = (step * pos_rows).astype(F32) * theta
        cos1, sin1 = jnp.cos(ang1), jnp.sin(ang1)
        cos0, sin0 = cos0_ref[...], sin0_ref[...]
        cos = cos1 * cos0 - sin1 * sin0
        sin = sin1 * cos0 + cos1 * sin0
        cos_ref[...] = cos
        sin_ref[...] = sin
        cost_ref[...] = cos.T
        sint_ref[...] = sin.T

        o_ref[...] = _swiglu_half_step(x_ref[...], nw_ref, wg_ref, wu_ref, wd_ref, act_ref)

    pltpu.emit_pipeline(step_body, grid=(n_steps,), in_specs=stream_in_specs, out_specs=stream_out_specs,
                        _explicit_indices=True)(*streamed_in, *streamed_out)


def _ffn_weight_specs():
    return [_resident((1, D_MODEL)), _resident((D_MODEL, D_FF)), _resident((D_MODEL, D_FF)),
            _resident((D_FF, D_MODEL))]


def _cast_rows(n_rows, n_steps):
    for rows in range(BF16_SUBLANES, n_rows + 1, BF16_SUBLANES):
        if n_rows % rows == 0 and n_rows // rows <= n_steps:
            return rows
    raise ValueError(f"cannot split {n_rows} rows over {n_steps} steps")


def _ffn(x, norm_w, w_gate, w_up, w_down, theta, later_weights, seq):
    t = x.shape[0]
    n_steps = t // FFN_ROWS
    pos_rows = seq // n_steps
    half = RET_DIM // 2
    assert seq % n_steps == 0 and pos_rows % LANES == 0
    rows = pl.BlockSpec((FFN_ROWS, D_MODEL), lambda i: (i, 0))

    cast_specs, cast_shapes = [], []
    for w in later_weights:
        r = _cast_rows(w.shape[0], n_steps)
        steps = w.shape[0] // r
        cast_specs.append(pl.BlockSpec((r, w.shape[1]), lambda i, last=steps - 1: (jnp.minimum(i, last), 0)))
        cast_shapes.append(jax.ShapeDtypeStruct(w.shape, BF16))
    table = pl.BlockSpec((pos_rows, half), lambda i: (i, 0))
    table_t = pl.BlockSpec((half, pos_rows), lambda i: (0, i))

    in_vmem = pl.BlockSpec(memory_space=pltpu.VMEM)
    in_hbm = pl.BlockSpec(memory_space=pl.ANY)
    n_out = 1 + len(later_weights) + 4
    outs = pl.pallas_call(
        functools.partial(_ffn_body, n_cast=len(later_weights), pos_rows=pos_rows, n_steps=n_steps,
                          stream_in_specs=[rows] + cast_specs,
                          stream_out_specs=[rows] + cast_specs + [table, table, table_t, table_t]),
        in_specs=[in_hbm] + [in_vmem] * 5 + [in_hbm] * len(later_weights),
        out_specs=[in_hbm] * n_out,
        out_shape=[jax.ShapeDtypeStruct((t, D_MODEL), F32)] + cast_shapes
                  + [jax.ShapeDtypeStruct((seq, half), F32)] * 2 + [jax.ShapeDtypeStruct((half, seq), F32)] * 2,
        scratch_shapes=[pltpu.VMEM((FFN_ROWS, D_FF), BF16), pltpu.VMEM((pos_rows, half), F32),
                        pltpu.VMEM((pos_rows, half), F32)],
        compiler_params=pltpu.CompilerParams(vmem_limit_bytes=60 * MIB),
        name="ffn",
    )(x, norm_w, w_gate, w_up, w_down, theta.reshape(1, half), *later_weights)
    n = len(later_weights)
    return outs[0], outs[1:1 + n], outs[1 + n:]


def _log_gamma(logit, shape):
    return jax.nn.log_sigmoid(jnp.full(shape, logit, F32))


def _zero_state_at_sequence_start(st_ref):
    @pl.when(pl.program_id(1) == 0)
    def _():
        st_ref[...] = jnp.zeros_like(st_ref)


def _state_update(st_ref, hd, kt, v, k_scale, block_decay):
    kd = (kt.astype(F32) * k_scale).astype(BF16)
    st_ref[hd] = st_ref[hd] * block_decay + _dot(kd, v)


def _retention_backward_head(dl_ref, hd, q_ref, kt_ref, v_ref, o_ref, st_ref):
    cols = slice(hd * RET_DIM, (hd + 1) * RET_DIM)
    row_idx = lax.broadcasted_iota(jnp.int32, (RET_BLOCK, RET_DIM), 0).astype(F32)
    lane_idx = lax.broadcasted_iota(jnp.int32, (RET_DIM, RET_BLOCK), 1).astype(F32)
    logit = dl_ref[BACKWARD, hd]
    q_scale = jnp.exp((RET_BLOCK - row_idx) * _log_gamma(logit, (RET_BLOCK, RET_DIM)))
    k_scale = jnp.exp(lane_idx * _log_gamma(logit, (RET_DIM, RET_BLOCK)))
    block_decay = jnp.exp(RET_BLOCK * _log_gamma(logit, (1, RET_DIM)))
    for blk in reversed(range(q_ref.shape[0] // RET_BLOCK)):
        rows = slice(blk * RET_BLOCK, (blk + 1) * RET_BLOCK)
        o_ref[rows, cols] = (_dot(q_ref[rows, cols], st_ref[hd].astype(BF16)) * q_scale).astype(BF16)
        _state_update(st_ref, hd, kt_ref[cols, rows], v_ref[rows, cols], k_scale, block_decay)


def _mixer_in_body(dl_ref, x_ref, nw_ref, w_ref, b_ref, lng_ref, lnb_ref, ws_ref, bs_ref, wa_ref,
                   cos_ref, sin_ref, cost_ref, sint_ref,
                   pk_ref, kt_ref, va_ref, vn_ref, a_ref, st_ref):
    q_ref, v_ref, sg_ref, sgb_ref, mixa_ref, cb_ref = _packed_views(pk_ref)
    _zero_state_at_sequence_start(st_ref)
    h = _rms(x_ref[...], nw_ref[...]).astype(BF16)
    half = RET_DIM // 2

    def proj(seg, cols):
        wcols = slice(seg * D_MODEL + cols.start, seg * D_MODEL + cols.stop)
        return _dot(h, w_ref[:, wcols]) + b_ref[:, wcols]

    cos, sin = cos_ref[...], sin_ref[...]
    cost, sint = cost_ref[...], sint_ref[...]
    k_scale = RET_DIM ** -0.5

    def retention_head(hd):
        cols = _col_chunks()[hd]
        t = proj(SEG_Q, cols)
        t1, t2 = t[:, :half], t[:, half:]
        q_ref[:, cols.start:cols.start + half] = (t1 * cos - t2 * sin).astype(BF16)
        q_ref[:, cols.start + half:cols.stop] = (t2 * cos + t1 * sin).astype(BF16)
        tt = proj(SEG_K, cols).T
        t1, t2 = tt[:half], tt[half:]
        kt_ref[cols.start:cols.start + half, :] = ((t1 * cost - t2 * sint) * k_scale).astype(BF16)
        kt_ref[cols.start + half:cols.stop, :] = ((t2 * cost + t1 * sint) * k_scale).astype(BF16)
        v_ref[:, cols] = proj(SEG_VR, cols).astype(BF16)
        sg_ref[:, cols] = _silu(proj(SEG_G, cols)).astype(BF16)
        sgb_ref[:, cols] = jax.nn.sigmoid(proj(SEG_GATE_B, cols)).astype(BF16)
        _retention_backward_head(dl_ref, hd, q_ref, kt_ref, v_ref, cb_ref, st_ref)

    for cols in _col_chunks():
        va_ref[:, cols] = _gelu(proj(SEG_V, cols))
    for hd in range(RET_HEADS // 2):
        retention_head(hd)
    va = va_ref[...]
    mu = jnp.mean(va, axis=-1, keepdims=True)
    vc = va - mu
    var = jnp.mean(vc * vc, axis=-1, keepdims=True)
    vn_ref[...] = (vc * lax.rsqrt(var + NORM_EPS) * lng_ref[...] + lnb_ref[...]).astype(BF16)
    for g, cols in enumerate(_col_chunks()):
        u = _gelu(proj(SEG_U, cols))
        for j in range(MIX_ROWS // (SPATIAL_PAIR * CHUNK)):
            rows = slice(j * SPATIAL_PAIR * CHUNK, (j + 1) * SPATIAL_PAIR * CHUNK)
            s = _dot(ws_ref[g], vn_ref[rows, cols]) + bs_ref[g]
            a_ref[rows, cols] = (u[rows] * s).astype(BF16)
    for hd in range(RET_HEADS // 2, RET_HEADS):
        retention_head(hd)
    a = a_ref[...]
    for cols in _col_chunks():
        mixa_ref[:, cols] = (jax.nn.sigmoid(proj(SEG_GATE_A, cols)) * _dot(a, wa_ref[:, cols])).astype(BF16)


def _paired_spatial_weight(w_s):
    eye = jnp.eye(SPATIAL_PAIR, dtype=w_s.dtype)
    paired = jnp.einsum("pq,gcm->gpcqm", eye, w_s)
    return paired.reshape(w_s.shape[0], SPATIAL_PAIR * CHUNK, SPATIAL_PAIR * CHUNK).astype(BF16)


def _mixer_in(decay_logit, x1, norm_w, w_in, b_in, ln_g, ln_b, w_s, b_s, w_a, cos, sin, cos_t, sin_t,
              batch, seq):
    t = x1.shape[0]
    half = RET_DIM // 2
    steps = seq // MIX_ROWS
    tile = lambda b, j: b * steps + steps - 1 - j
    rows = pl.BlockSpec((MIX_ROWS, D_MODEL), lambda b, j: (tile(b, j), 0))
    lanes = pl.BlockSpec((D_MODEL, MIX_ROWS), lambda b, j: (0, tile(b, j)))
    pos = pl.BlockSpec((MIX_ROWS, half), lambda b, j: (steps - 1 - j, 0))
    pos_t = pl.BlockSpec((half, MIX_ROWS), lambda b, j: (0, steps - 1 - j))
    packed = pl.BlockSpec((MIX_ROWS, N_PACK * D_MODEL), lambda b, j: (tile(b, j), 0))
    width = w_in.shape[1]
    out_t = jax.ShapeDtypeStruct((D_MODEL, t), BF16)
    return pl.pallas_call(
        _mixer_in_body,
        grid=(batch, steps),
        in_specs=[pl.BlockSpec(memory_space=pltpu.SMEM),
                  rows, _resident((1, D_MODEL)), _resident((D_MODEL, width)), _resident((1, width)),
                  _resident((1, D_MODEL)), _resident((1, D_MODEL)), _resident(w_s.shape), _resident(b_s.shape),
                  _resident((D_MODEL, D_MODEL)), pos, pos, pos_t, pos_t],
        out_specs=[packed, lanes],
        out_shape=[jax.ShapeDtypeStruct((t, N_PACK * D_MODEL), BF16), out_t],
        scratch_shapes=[pltpu.VMEM((MIX_ROWS, D_MODEL), F32), pltpu.VMEM((MIX_ROWS, D_MODEL), BF16),
                        pltpu.VMEM((MIX_ROWS, D_MODEL), BF16), pltpu.VMEM((RET_HEADS, RET_DIM, RET_DIM), F32)],
        compiler_params=pltpu.CompilerParams(dimension_semantics=("arbitrary", "arbitrary"),
                                             vmem_limit_bytes=56 * MIB),
        name="mixer_in",
    )(decay_logit, x1, norm_w, w_in, b_in, ln_g, ln_b, w_s, b_s, w_a, cos, sin, cos_t, sin_t)


def _mixer_out_body(*refs, final_norm):
    dl_ref, pk_ref, kt_ref, x1_ref, wb_ref, wo_ref, nw_ref, wg_ref, wu_ref, wd_ref = refs[:10]
    fn_ref = refs[10] if final_norm else None
    q_ref, v_ref, sg_ref, sgb_ref, mixa_ref, cb_ref = _packed_views(pk_ref)
    o_ref, st_ref, p_ref, r_ref, mix_ref, x2_ref, act_ref = refs[-7:]
    _zero_state_at_sequence_start(st_ref)
    row_idx = lax.broadcasted_iota(jnp.int32, (RET_BLOCK, RET_DIM), 0).astype(F32)
    lane_idx = lax.broadcasted_iota(jnp.int32, (RET_DIM, RET_BLOCK), 1).astype(F32)
    diff = (lax.broadcasted_iota(jnp.int32, (RET_BLOCK, RET_BLOCK), 0)
            - lax.broadcasted_iota(jnp.int32, (RET_BLOCK, RET_BLOCK), 1)).astype(F32)
    n_blocks = RET_ROWS // RET_BLOCK
    head_cols = lambda hd: slice(hd * RET_DIM, (hd + 1) * RET_DIM)
    block_rows = lambda blk: slice(blk * RET_BLOCK, (blk + 1) * RET_BLOCK)

    for hd in range(RET_HEADS):
        lg_f = _log_gamma(dl_ref[FORWARD, hd], (RET_BLOCK, RET_BLOCK))
        lg_b = _log_gamma(dl_ref[BACKWARD, hd], (RET_BLOCK, RET_BLOCK))
        decay = jnp.exp(jnp.where(diff >= 0, diff * lg_f, -diff * lg_b))
        for blk in range(n_blocks):
            scores = _dot(q_ref[block_rows(blk), head_cols(hd)], kt_ref[head_cols(hd), block_rows(blk)])
            p_ref[hd * n_blocks + blk] = (scores * decay).astype(BF16)

    for hd in range(RET_HEADS):
        cols = head_cols(hd)
        logit = dl_ref[FORWARD, hd]
        q_scale = jnp.exp((row_idx + 1.0) * _log_gamma(logit, (RET_BLOCK, RET_DIM)))
        k_scale = jnp.exp((RET_BLOCK - 1.0 - lane_idx) * _log_gamma(logit, (RET_DIM, RET_BLOCK)))
        block_decay = jnp.exp(RET_BLOCK * _log_gamma(logit, (1, RET_DIM)))
        for blk in range(n_blocks):
            rows = block_rows(blk)
            v = v_ref[rows, cols]
            r = (_dot(p_ref[hd * n_blocks + blk], v)
                 + _dot(q_ref[rows, cols], st_ref[hd].astype(BF16)) * q_scale
                 + cb_ref[rows, cols].astype(F32))
            r = r * lax.rsqrt(jnp.mean(r * r, axis=-1, keepdims=True) + NORM_EPS)
            r_ref[rows, cols] = (r * sg_ref[rows, cols].astype(F32)).astype(BF16)
            _state_update(st_ref, hd, kt_ref[cols, rows], v, k_scale, block_decay)
    r = r_ref[...]
    for cols in _col_chunks():
        mix_ref[:, cols] = (sgb_ref[:, cols].astype(F32) * _dot(r, wb_ref[:, cols])
                            + mixa_ref[:, cols].astype(F32)).astype(BF16)
    mix = mix_ref[...]
    for cols in _col_chunks():
        x2_ref[:, cols] = x1_ref[:, cols] + _dot(mix, wo_ref[:, cols])
    y = _swiglu_half_step(x2_ref[...], nw_ref, wg_ref, wu_ref, wd_ref, act_ref)
    if final_norm:
        y = _rms(y, fn_ref[...])
    o_ref[...] = y


def _mixer_out(decay_logit, packed, kt, x1, w_b, w_o, norm_w, w_gate, w_up, w_down, final_w, batch, seq):
    steps = seq // RET_ROWS
    rows = pl.BlockSpec((RET_ROWS, D_MODEL), lambda b, j: (b * steps + j, 0))
    packed_rows = pl.BlockSpec((RET_ROWS, N_PACK * D_MODEL), lambda b, j: (b * steps + j, 0))
    lanes = pl.BlockSpec((D_MODEL, RET_ROWS), lambda b, j: (0, b * steps + j))
    in_specs = ([pl.BlockSpec(memory_space=pltpu.SMEM), packed_rows, lanes, rows]
                + [_resident((D_MODEL, D_MODEL)), _resident((D_MODEL, D_MODEL))] + _ffn_weight_specs())
    args = [decay_logit, packed, kt, x1, w_b, w_o, norm_w, w_gate, w_up, w_down]
    if final_w is not None:
        in_specs.append(_resident((1, D_MODEL)))
        args.append(final_w)
    return pl.pallas_call(
        functools.partial(_mixer_out_body, final_norm=final_w is not None),
        grid=(batch, steps),
        in_specs=in_specs,
        out_specs=rows,
        out_shape=jax.ShapeDtypeStruct(x1.shape, F32),
        scratch_shapes=[pltpu.VMEM((RET_HEADS, RET_DIM, RET_DIM), F32),
                        pltpu.VMEM((RET_HEADS * (RET_ROWS // RET_BLOCK), RET_BLOCK, RET_BLOCK), BF16),
                        pltpu.VMEM((RET_ROWS, D_MODEL), BF16), pltpu.VMEM((RET_ROWS, D_MODEL), BF16),
                        pltpu.VMEM((RET_ROWS, D_MODEL), F32), pltpu.VMEM((RET_ROWS, D_FF), BF16)],
        compiler_params=pltpu.CompilerParams(dimension_semantics=("arbitrary", "arbitrary"),
                                             vmem_limit_bytes=60 * MIB),
        name="mixer_out",
    )(*args)


def kernel(x, ffn1_norm, ffn1_w_gate, ffn1_w_up, ffn1_w_down, mix_norm, w_in, b_in, sgu_norm_g, sgu_norm_b, sgu_w_s, sgu_b_s, ret_decay_logit, w_branch_a, w_branch_b, w_out, ffn2_norm, ffn2_w_gate, ffn2_w_up, ffn2_w_down, final_norm):
    batch, seq, _ = x.shape
    depth = ffn1_norm.shape[0]
    assert depth >= 1 and seq % RET_ROWS == 0 and seq % MIX_ROWS == 0 and (batch * seq) % FFN_ROWS == 0
    tokens = batch * seq
    row = lambda p: p.reshape(1, -1)
    theta = ROPE_BASE ** (-jnp.arange(0, RET_DIM, 2, dtype=F32) / RET_DIM)

    xt = x.reshape(tokens, D_MODEL)
    for l in range(depth):
        last = l == depth - 1
        x1, (w_in_bf, w_a, w_b, w_o, w_gate2, w_up2, w_down2), (cos, sin, cos_t, sin_t) = _ffn(
            xt, row(ffn1_norm[l]), ffn1_w_gate[l], ffn1_w_up[l], ffn1_w_down[l], theta,
            [w_in[l], w_branch_a[l], w_branch_b[l], w_out[l], ffn2_w_gate[l], ffn2_w_up[l], ffn2_w_down[l]], seq)
        packed, kt = _mixer_in(
            ret_decay_logit[l], x1, row(mix_norm[l]), w_in_bf, row(b_in[l]),
            row(sgu_norm_g[l]), row(sgu_norm_b[l]), _paired_spatial_weight(sgu_w_s[l]),
            jnp.tile(sgu_b_s[l], (1, SPATIAL_PAIR))[..., None], w_a, cos, sin, cos_t, sin_t, batch, seq)
        xt = _mixer_out(ret_decay_logit[l], packed, kt, x1, w_b, w_o,
                        row(ffn2_norm[l]), w_gate2, w_up2, w_down2,
                        row(final_norm) if last else None, batch, seq)
    return xt.reshape(batch, seq, D_MODEL)
```

```python
import functools

import jax
import jax.numpy as jnp
import numpy as np
from jax import lax
from jax.experimental import pallas as pl
from jax.experimental.pallas import tpu as pltpu

D_MODEL = 1024
D_FF = 2816
CHUNK = 128
SPATIAL_PAIR = 2
SGU_GROUPS = 4
SGU_GROUP_DIM = D_MODEL // SGU_GROUPS
RET_HEADS = 4
RET_DIM = 256
SEG_U, SEG_V, SEG_Q, SEG_K, SEG_VR, SEG_G, SEG_GATE_A, SEG_GATE_B = range(8)
FORWARD, BACKWARD = 0, 1
PACK_Q, PACK_V, PACK_SG, PACK_SGB, PACK_MIXA, PACK_CB = range(6)
N_PACK = 6
ROPE_BASE = 10000.0
NORM_EPS = 1e-6

MXU_COLS = 256
BF16_SUBLANES = 16
LANES = 128
FFN_ROWS = 512
MIX_ROWS = 512
RET_ROWS = 512
RET_BLOCK = 256
HEADS_BEFORE_LAYER_NORM = 1
MIB = 1024 * 1024

BF16 = jnp.bfloat16
F32 = jnp.float32

assert RET_DIM == MXU_COLS and SGU_GROUP_DIM == MXU_COLS


def _rms(x, g):
    return x * lax.rsqrt(jnp.mean(x * x, axis=-1, keepdims=True) + NORM_EPS) * g


def _gelu(x):
    return 0.5 * x * (1.0 + lax.erf(x * np.float32(np.sqrt(0.5))))


def _silu(x):
    return x * jax.nn.sigmoid(x)


def _dot(a, b):
    return jnp.dot(a, b, preferred_element_type=F32)


def _col_chunks(width=D_MODEL):
    return [slice(c, c + MXU_COLS) for c in range(0, width, MXU_COLS)]


def _packed_views(pk_ref):
    return [pk_ref.at[:, i * D_MODEL:(i + 1) * D_MODEL] for i in range(N_PACK)]


def _resident(shape):
    return pl.BlockSpec(shape, lambda *_: (0,) * len(shape), pipeline_mode=pl.Buffered(1))


def _swiglu_half_step(x, nw_ref, wg_ref, wu_ref, wd_ref, act_ref):
    h = (x * nw_ref[...]).astype(BF16)
    inv_rms = lax.rsqrt(jnp.mean(x * x, axis=-1, keepdims=True) + NORM_EPS)
    for cols in _col_chunks(D_FF):
        g = _dot(h, wg_ref[:, cols].astype(BF16)) * inv_rms
        u = _dot(h, wu_ref[:, cols].astype(BF16)) * inv_rms
        act_ref[:, cols] = (_silu(g) * u).astype(BF16)
    act = act_ref[...]
    down = _dot(act, wd_ref[...].astype(BF16))
    return x + 0.5 * down


def _ffn_body(*refs, n_cast, pos_rows):
    n = n_cast
    x_ref, nw_ref, wg_ref, wu_ref, wd_ref, theta_ref = refs[:6]
    srcs = refs[6:6 + n]
    o_ref = refs[6 + n]
    dsts = refs[7 + n:7 + 2 * n]
    cos_ref, sin_ref, cost_ref, sint_ref, act_ref, cos0_ref, sin0_ref = refs[7 + 2 * n:]
    step = pl.program_id(0)
    theta = theta_ref[...]

    @pl.when(step == 0)
    def _():
        ang0 = lax.broadcasted_iota(jnp.int32, (pos_rows, RET_DIM // 2), 0).astype(F32) * theta
        cos0_ref[...] = jnp.cos(ang0)
        sin0_ref[...] = jnp.sin(ang0)

    for src, dst in zip(srcs, dsts):
        dst[...] = src[...].astype(BF16)

    ang1 = (step * pos_rows).astype(F32) * theta
    cos1, sin1 = jnp.cos(ang1), jnp.sin(ang1)
    cos0, sin0 = cos0_ref[...], sin0_ref[...]
    cos = cos1 * cos0 - sin1 * sin0
    sin = sin1 * cos0 + cos1 * sin0
    cos_ref[...] = cos
    sin_ref[...] = sin
    cost_ref[...] = cos.T
    sint_ref[...] = sin.T

    o_ref[...] = _swiglu_half_step(x_ref[...], nw_ref, wg_ref, wu_ref, wd_ref, act_ref)


def _ffn_weight_specs():
    return [_resident((1, D_MODEL)), _resident((D_MODEL, D_FF)), _resident((D_MODEL, D_FF)),
            _resident((D_FF, D_MODEL))]


def _cast_rows(n_rows, n_steps):
    for rows in range(BF16_SUBLANES, n_rows + 1, BF16_SUBLANES):
        if n_rows % rows == 0 and n_rows // rows <= n_steps:
            return rows
    raise ValueError(f"cannot split {n_rows} rows over {n_steps} steps")


def _ffn(x, norm_w, w_gate, w_up, w_down, theta, later_weights, seq):
    t = x.shape[0]
    n_steps = t // FFN_ROWS
    pos_rows = seq // n_steps
    half = RET_DIM // 2
    assert seq % n_steps == 0 and pos_rows % LANES == 0
    rows = pl.BlockSpec((FFN_ROWS, D_MODEL), lambda i: (i, 0))

    cast_specs, cast_shapes = [], []
    for w in later_weights:
        r = _cast_rows(w.shape[0], n_steps)
        steps = w.shape[0] // r
        cast_specs.append(pl.BlockSpec((r, w.shape[1]), lambda i, last=steps - 1: (jnp.minimum(i, last), 0)))
        cast_shapes.append(jax.ShapeDtypeStruct(w.shape, BF16))
    table = pl.BlockSpec((pos_rows, half), lambda i: (i, 0))
    table_t = pl.BlockSpec((half, pos_rows), lambda i: (0, i))

    outs = pl.pallas_call(
        functools.partial(_ffn_body, n_cast=len(later_weights), pos_rows=pos_rows),
        grid=(n_steps,),
        in_specs=[rows] + _ffn_weight_specs() + [_resident((1, half))] + cast_specs,
        out_specs=[rows] + cast_specs + [table, table, table_t, table_t],
        out_shape=[jax.ShapeDtypeStruct((t, D_MODEL), F32)] + cast_shapes
                  + [jax.ShapeDtypeStruct((seq, half), F32)] * 2 + [jax.ShapeDtypeStruct((half, seq), F32)] * 2,
        scratch_shapes=[pltpu.VMEM((FFN_ROWS, D_FF), BF16), pltpu.VMEM((pos_rows, half), F32),
                        pltpu.VMEM((pos_rows, half), F32)],
        compiler_params=pltpu.CompilerParams(dimension_semantics=("arbitrary",), vmem_limit_bytes=60 * MIB),
        name="ffn",
    )(x, norm_w, w_gate, w_up, w_down, theta.reshape(1, half), *later_weights)
    n = len(later_weights)
    return outs[0], outs[1:1 + n], outs[1 + n:]


def _log_gamma(logit, shape):
    return jax.nn.log_sigmoid(jnp.full(shape, logit, F32))


def _zero_state_at_sequence_start(st_ref):
    @pl.when(pl.program_id(1) == 0)
    def _():
        st_ref[...] = jnp.zeros_like(st_ref)


def _state_update(st_ref, hd, kt, v, k_scale, block_decay):
    kd = (kt.astype(F32) * k_scale).astype(BF16)
    st_ref[hd] = st_ref[hd] * block_decay + _dot(kd, v)


def _retention_backward_head(dl_ref, hd, q_ref, kt_ref, v_ref, o_ref, st_ref):
    cols = slice(hd * RET_DIM, (hd + 1) * RET_DIM)
    row_idx = lax.broadcasted_iota(jnp.int32, (RET_BLOCK, RET_DIM), 0).astype(F32)
    lane_idx = lax.broadcasted_iota(jnp.int32, (RET_DIM, RET_BLOCK), 1).astype(F32)
    logit = dl_ref[BACKWARD, hd]
    q_scale = jnp.exp((RET_BLOCK - row_idx) * _log_gamma(logit, (RET_BLOCK, RET_DIM)))
    k_scale = jnp.exp(lane_idx * _log_gamma(logit, (RET_DIM, RET_BLOCK)))
    block_decay = jnp.exp(RET_BLOCK * _log_gamma(logit, (1, RET_DIM)))
    for blk in reversed(range(q_ref.shape[0] // RET_BLOCK)):
        rows = slice(blk * RET_BLOCK, (blk + 1) * RET_BLOCK)
        o_ref[rows, cols] = (_dot(q_ref[rows, cols], st_ref[hd].astype(BF16)) * q_scale).astype(BF16)
        _state_update(st_ref, hd, kt_ref[cols, rows], v_ref[rows, cols], k_scale, block_decay)


def _mixer_in_body(dl_ref, x_ref, nw_ref, w_ref, b_ref, lng_ref, lnb_ref, ws_ref, bs_ref, wa_ref,
                   cos_ref, sin_ref, cost_ref, sint_ref,
                   pk_ref, kt_ref, va_ref, vn_ref, a_ref, st_ref):
    q_ref, v_ref, sg_ref, sgb_ref, mixa_ref, cb_ref = _packed_views(pk_ref)
    _zero_state_at_sequence_start(st_ref)
    h = _rms(x_ref[...], nw_ref[...]).astype(BF16)
    half = RET_DIM // 2

    def proj(seg, cols):
        wcols = slice(seg * D_MODEL + cols.start, seg * D_MODEL + cols.stop)
        return _dot(h, w_ref[:, wcols]) + b_ref[:, wcols]

    cos, sin = cos_ref[...], sin_ref[...]
    cost, sint = cost_ref[...], sint_ref[...]
    k_scale = RET_DIM ** -0.5

    def retention_head(hd):
        cols = _col_chunks()[hd]
        t = proj(SEG_Q, cols)
        t1, t2 = t[:, :half], t[:, half:]
        q_ref[:, cols.start:cols.start + half] = (t1 * cos - t2 * sin).astype(BF16)
        q_ref[:, cols.start + half:cols.stop] = (t2 * cos + t1 * sin).astype(BF16)
        tt = proj(SEG_K, cols).T
        t1, t2 = tt[:half], tt[half:]
        kt_ref[cols.start:cols.start + half, :] = ((t1 * cost - t2 * sint) * k_scale).astype(BF16)
        kt_ref[cols.start + half:cols.stop, :] = ((t2 * cost + t1 * sint) * k_scale).astype(BF16)
        v_ref[:, cols] = proj(SEG_VR, cols).astype(BF16)
        sg_ref[:, cols] = _silu(proj(SEG_G, cols)).astype(BF16)
        sgb_ref[:, cols] = jax.nn.sigmoid(proj(SEG_GATE_B, cols)).astype(BF16)
        _retention_backward_head(dl_ref, hd, q_ref, kt_ref, v_ref, cb_ref, st_ref)

    for cols in _col_chunks():
        va_ref[:, cols] = _gelu(proj(SEG_V, cols))
    for hd in range(HEADS_BEFORE_LAYER_NORM):
        retention_head(hd)
    va = va_ref[...]
    mu = jnp.mean(va, axis=-1, keepdims=True)
    vc = va - mu
    var = jnp.mean(vc * vc, axis=-1, keepdims=True)
    vn_ref[...] = (vc * lax.rsqrt(var + NORM_EPS) * lng_ref[...] + lnb_ref[...]).astype(BF16)
    for g, cols in enumerate(_col_chunks()):
        u = _gelu(proj(SEG_U, cols))
        for j in range(MIX_ROWS // (SPATIAL_PAIR * CHUNK)):
            rows = slice(j * SPATIAL_PAIR * CHUNK, (j + 1) * SPATIAL_PAIR * CHUNK)
            s = _dot(ws_ref[g], vn_ref[rows, cols]) + bs_ref[g]
            a_ref[rows, cols] = (u[rows] * s).astype(BF16)
    for hd in range(HEADS_BEFORE_LAYER_NORM, RET_HEADS):
        retention_head(hd)
    a = a_ref[...]
    for cols in _col_chunks():
        mixa_ref[:, cols] = (jax.nn.sigmoid(proj(SEG_GATE_A, cols)) * _dot(a, wa_ref[:, cols])).astype(BF16)


def _paired_spatial_weight(w_s):
    eye = jnp.eye(SPATIAL_PAIR, dtype=w_s.dtype)
    paired = jnp.einsum("pq,gcm->gpcqm", eye, w_s)
    return paired.reshape(w_s.shape[0], SPATIAL_PAIR * CHUNK, SPATIAL_PAIR * CHUNK).astype(BF16)


def _mixer_in(decay_logit, x1, norm_w, w_in, b_in, ln_g, ln_b, w_s, b_s, w_a, cos, sin, cos_t, sin_t,
              batch, seq):
    t = x1.shape[0]
    half = RET_DIM // 2
    steps = seq // MIX_ROWS
    tile = lambda b, j: b * steps + steps - 1 - j
    rows = pl.BlockSpec((MIX_ROWS, D_MODEL), lambda b, j: (tile(b, j), 0))
    lanes = pl.BlockSpec((D_MODEL, MIX_ROWS), lambda b, j: (0, tile(b, j)))
    pos = pl.BlockSpec((MIX_ROWS, half), lambda b, j: (steps - 1 - j, 0))
    pos_t = pl.BlockSpec((half, MIX_ROWS), lambda b, j: (0, steps - 1 - j))
    packed = pl.BlockSpec((MIX_ROWS, N_PACK * D_MODEL), lambda b, j: (tile(b, j), 0))
    width = w_in.shape[1]
    out_t = jax.ShapeDtypeStruct((D_MODEL, t), BF16)
    return pl.pallas_call(
        _mixer_in_body,
        grid=(batch, steps),
        in_specs=[pl.BlockSpec(memory_space=pltpu.SMEM),
                  rows, _resident((1, D_MODEL)), _resident((D_MODEL, width)), _resident((1, width)),
                  _resident((1, D_MODEL)), _resident((1, D_MODEL)), _resident(w_s.shape), _resident(b_s.shape),
                  _resident((D_MODEL, D_MODEL)), pos, pos, pos_t, pos_t],
        out_specs=[packed, lanes],
        out_shape=[jax.ShapeDtypeStruct((t, N_PACK * D_MODEL), BF16), out_t],
        scratch_shapes=[pltpu.VMEM((MIX_ROWS, D_MODEL), F32), pltpu.VMEM((MIX_ROWS, D_MODEL), BF16),
                        pltpu.VMEM((MIX_ROWS, D_MODEL), BF16), pltpu.VMEM((RET_HEADS, RET_DIM, RET_DIM), F32)],
        compiler_params=pltpu.CompilerParams(dimension_semantics=("arbitrary", "arbitrary"),
                                             vmem_limit_bytes=56 * MIB),
        name="mixer_in",
    )(decay_logit, x1, norm_w, w_in, b_in, ln_g, ln_b, w_s, b_s, w_a, cos, sin, cos_t, sin_t)


def _mixer_out_body(*refs, final_norm):
    dl_ref, pk_ref, kt_ref, x1_ref, wb_ref, wo_ref, nw_ref, wg_ref, wu_ref, wd_ref = refs[:10]
    fn_ref = refs[10] if final_norm else None
    q_ref, v_ref, sg_ref, sgb_ref, mixa_ref, cb_ref = _packed_views(pk_ref)
    o_ref, st_ref, p_ref, r_ref, act_ref = refs[-5:]
    _zero_state_at_sequence_start(st_ref)
    row_idx = lax.broadcasted_iota(jnp.int32, (RET_BLOCK, RET_DIM), 0).astype(F32)
    lane_idx = lax.broadcasted_iota(jnp.int32, (RET_DIM, RET_BLOCK), 1).astype(F32)
    diff = (lax.broadcasted_iota(jnp.int32, (RET_BLOCK, RET_BLOCK), 0)
            - lax.broadcasted_iota(jnp.int32, (RET_BLOCK, RET_BLOCK), 1)).astype(F32)
    n_blocks = RET_ROWS // RET_BLOCK
    head_cols = lambda hd: slice(hd * RET_DIM, (hd + 1) * RET_DIM)
    block_rows = lambda blk: slice(blk * RET_BLOCK, (blk + 1) * RET_BLOCK)

    for hd in range(RET_HEADS):
        lg_f = _log_gamma(dl_ref[FORWARD, hd], (RET_BLOCK, RET_BLOCK))
        lg_b = _log_gamma(dl_ref[BACKWARD, hd], (RET_BLOCK, RET_BLOCK))
        decay = jnp.exp(jnp.where(diff >= 0, diff * lg_f, -diff * lg_b))
        for blk in range(n_blocks):
            scores = _dot(q_ref[block_rows(blk), head_cols(hd)], kt_ref[head_cols(hd), block_rows(blk)])
            p_ref[hd * n_blocks + blk] = (scores * decay).astype(BF16)

    for blk in range(n_blocks):
        rows = block_rows(blk)
        for hd in range(RET_HEADS):
            cols = head_cols(hd)
            logit = dl_ref[FORWARD, hd]
            q_scale = jnp.exp((row_idx + 1.0) * _log_gamma(logit, (RET_BLOCK, RET_DIM)))
            k_scale = jnp.exp((RET_BLOCK - 1.0 - lane_idx) * _log_gamma(logit, (RET_DIM, RET_BLOCK)))
            block_decay = jnp.exp(RET_BLOCK * _log_gamma(logit, (1, RET_DIM)))
            v = v_ref[rows, cols]
            r = (_dot(p_ref[hd * n_blocks + blk], v)
                 + _dot(q_ref[rows, cols], st_ref[hd].astype(BF16)) * q_scale
                 + cb_ref[rows, cols].astype(F32))
            r = r * lax.rsqrt(jnp.mean(r * r, axis=-1, keepdims=True) + NORM_EPS)
            r_ref[rows, cols] = (r * sg_ref[rows, cols].astype(F32)).astype(BF16)
            _state_update(st_ref, hd, kt_ref[cols, rows], v, k_scale, block_decay)
    mix = (sgb_ref[...].astype(F32) * _dot(r_ref[...], wb_ref[...]) + mixa_ref[...].astype(F32)).astype(BF16)
    x2 = x1_ref[...] + _dot(mix, wo_ref[...])
    y = _swiglu_half_step(x2, nw_ref, wg_ref, wu_ref, wd_ref, act_ref)
    if final_norm:
        y = _rms(y, fn_ref[...])
    o_ref[...] = y


def _mixer_out(decay_logit, packed, kt, x1, w_b, w_o, norm_w, w_gate, w_up, w_down, final_w, batch, seq):
    steps = seq // RET_ROWS
    rows = pl.BlockSpec((RET_ROWS, D_MODEL), lambda b, j: (b * steps + j, 0))
    packed_rows = pl.BlockSpec((RET_ROWS, N_PACK * D_MODEL), lambda b, j: (b * steps + j, 0))
    lanes = pl.BlockSpec((D_MODEL, RET_ROWS), lambda b, j: (0, b * steps + j))
    in_specs = ([pl.BlockSpec(memory_space=pltpu.SMEM), packed_rows, lanes, rows]
                + [_resident((D_MODEL, D_MODEL)), _resident((D_MODEL, D_MODEL))] + _ffn_weight_specs())
    args = [decay_logit, packed, kt, x1, w_b, w_o, norm_w, w_gate, w_up, w_down]
    if final_w is not None:
        in_specs.append(_resident((1, D_MODEL)))
        args.append(final_w)
    return pl.pallas_call(
        functools.partial(_mixer_out_body, final_norm=final_w is not None),
        grid=(batch, steps),
        in_specs=in_specs,
        out_specs=rows,
        out_shape=jax.ShapeDtypeStruct(x1.shape, F32),
        scratch_shapes=[pltpu.VMEM((RET_HEADS, RET_DIM, RET_DIM), F32),
                        pltpu.VMEM((RET_HEADS * (RET_ROWS // RET_BLOCK), RET_BLOCK, RET_BLOCK), BF16),
                        pltpu.VMEM((RET_ROWS, D_MODEL), BF16), pltpu.VMEM((RET_ROWS, D_FF), BF16)],
        compiler_params=pltpu.CompilerParams(dimension_semantics=("arbitrary", "arbitrary"),
                                             vmem_limit_bytes=60 * MIB),
        name="mixer_out",
    )(*args)


def kernel(x, ffn1_norm, ffn1_w_gate, ffn1_w_up, ffn1_w_down, mix_norm, w_in, b_in, sgu_norm_g, sgu_norm_b, sgu_w_s, sgu_b_s, ret_decay_logit, w_branch_a, w_branch_b, w_out, ffn2_norm, ffn2_w_gate, ffn2_w_up, ffn2_w_down, final_norm):
    batch, seq, _ = x.shape
    depth = ffn1_norm.shape[0]
    assert depth >= 1 and seq % RET_ROWS == 0 and seq % MIX_ROWS == 0 and (batch * seq) % FFN_ROWS == 0
    tokens = batch * seq
    row = lambda p: p.reshape(1, -1)
    theta = ROPE_BASE ** (-jnp.arange(0, RET_DIM, 2, dtype=F32) / RET_DIM)

    xt = x.reshape(tokens, D_MODEL)
    for l in range(depth):
        last = l == depth - 1
        x1, (w_in_bf, w_a, w_b, w_o, w_gate2, w_up2, w_down2), (cos, sin, cos_t, sin_t) = _ffn(
            xt, row(ffn1_norm[l]), ffn1_w_gate[l], ffn1_w_up[l], ffn1_w_down[l], theta,
            [w_in[l], w_branch_a[l], w_branch_b[l], w_out[l], ffn2_w_gate[l], ffn2_w_up[l], ffn2_w_down[l]], seq)
        packed, kt = _mixer_in(
            ret_decay_logit[l], x1, row(mix_norm[l]), w_in_bf, row(b_in[l]),
            row(sgu_norm_g[l]), row(sgu_norm_b[l]), _paired_spatial_weight(sgu_w_s[l]),
            jnp.tile(sgu_b_s[l], (1, SPATIAL_PAIR))[..., None], w_a, cos, sin, cos_t, sin_t, batch, seq)
        xt = _mixer_out(ret_decay_logit[l], packed, kt, x1, w_b, w_o,
                        row(ffn2_norm[l]), w_gate2, w_up2, w_down2,
                        row(final_norm) if last else None, batch, seq)
    return xt.reshape(batch, seq, D_MODEL)
```

```python
import functools
import math

import jax
import jax.numpy as jnp
import numpy as np
from jax import lax
from jax.experimental import pallas as pl
from jax.experimental.pallas import tpu as pltpu

D_MODEL = 1024
D_FF = 2816
CHUNK = 128
SPATIAL_PAIR = 2
SGU_GROUPS = 4
SGU_GROUP_DIM = D_MODEL // SGU_GROUPS
RET_HEADS = 4
RET_DIM = 256
SEG_U, SEG_V, SEG_Q, SEG_K, SEG_VR, SEG_G, SEG_GATE_A, SEG_GATE_B = range(8)
FORWARD, BACKWARD = 0, 1
PACK_Q, PACK_V, PACK_SG, PACK_SGB, PACK_MIXA, PACK_CB = range(6)
N_PACK = 6
ROPE_BASE = 10000.0
NORM_EPS = 1e-6

MXU_COLS = 256
BF16_SUBLANES = 16
LANES = 128
FFN_ROWS = 512
MIX_ROWS = 512
RET_ROWS = 512
RET_BLOCK = 256
HEADS_BEFORE_LAYER_NORM = 1
MIB = 1024 * 1024
V7X_VMEM_BYTES = 64 * MIB
VMEM_LEFT_TO_COMPILER = 4 * MIB
SPILL_ALLOWANCE = 8 * MIB

BF16 = jnp.bfloat16
F32 = jnp.float32

assert RET_DIM == MXU_COLS and SGU_GROUP_DIM == MXU_COLS


def _rms(x, g):
    return x * lax.rsqrt(jnp.mean(x * x, axis=-1, keepdims=True) + NORM_EPS) * g


def _gelu(x):
    return 0.5 * x * (1.0 + lax.erf(x * np.float32(np.sqrt(0.5))))


def _silu(x):
    return x * jax.nn.sigmoid(x)


def _dot(a, b):
    return jnp.dot(a, b, preferred_element_type=F32)


def _col_chunks(width=D_MODEL):
    return [slice(c, c + MXU_COLS) for c in range(0, width, MXU_COLS)]


def _packed_views(pk_ref):
    return [pk_ref.at[:, i * D_MODEL:(i + 1) * D_MODEL] for i in range(N_PACK)]


def _vmem_limit(resident, streamed, scratch):
    nbytes = lambda items: sum(math.prod(shape) * jnp.dtype(dtype).itemsize for shape, dtype in items)
    need = nbytes(resident) + 2 * nbytes(streamed) + nbytes(scratch) + SPILL_ALLOWANCE
    return min(need, V7X_VMEM_BYTES - VMEM_LEFT_TO_COMPILER)


def _resident(shape):
    return pl.BlockSpec(shape, lambda *_: (0,) * len(shape), pipeline_mode=pl.Buffered(1))


def _swiglu_half_step(x, nw_ref, wg_ref, wu_ref, wd_ref, act_ref):
    h = (x * nw_ref[...]).astype(BF16)
    inv_rms = lax.rsqrt(jnp.mean(x * x, axis=-1, keepdims=True) + NORM_EPS)
    for cols in _col_chunks(D_FF):
        g = _dot(h, wg_ref[:, cols].astype(BF16)) * inv_rms
        u = _dot(h, wu_ref[:, cols].astype(BF16)) * inv_rms
        act_ref[:, cols] = (_silu(g) * u).astype(BF16)
    act = act_ref[...]
    down = _dot(act, wd_ref[...].astype(BF16))
    return x + 0.5 * down


def _ffn_body(*refs, n_cast, pos_rows):
    n = n_cast
    x_ref, nw_ref, wg_ref, wu_ref, wd_ref, theta_ref = refs[:6]
    srcs = refs[6:6 + n]
    o_ref = refs[6 + n]
    dsts = refs[7 + n:7 + 2 * n]
    cos_ref, sin_ref, cost_ref, sint_ref, act_ref, cos0_ref, sin0_ref = refs[7 + 2 * n:]
    step = pl.program_id(0)
    theta = theta_ref[...]

    @pl.when(step == 0)
    def _():
        ang0 = lax.broadcasted_iota(jnp.int32, (pos_rows, RET_DIM // 2), 0).astype(F32) * theta
        cos0_ref[...] = jnp.cos(ang0)
        sin0_ref[...] = jnp.sin(ang0)

    for src, dst in zip(srcs, dsts):
        dst[...] = src[...].astype(BF16)

    ang1 = (step * pos_rows).astype(F32) * theta
    cos1, sin1 = jnp.cos(ang1), jnp.sin(ang1)
    cos0, sin0 = cos0_ref[...], sin0_ref[...]
    cos = cos1 * cos0 - sin1 * sin0
    sin = sin1 * cos0 + cos1 * sin0
    cos_ref[...] = cos
    sin_ref[...] = sin
    cost_ref[...] = cos.T
    sint_ref[...] = sin.T

    o_ref[...] = _swiglu_half_step(x_ref[...], nw_ref, wg_ref, wu_ref, wd_ref, act_ref)


def _ffn_weight_specs():
    return [_resident((1, D_MODEL)), _resident((D_MODEL, D_FF)), _resident((D_MODEL, D_FF)),
            _resident((D_FF, D_MODEL))]


def _cast_rows(n_rows, n_steps):
    for rows in range(BF16_SUBLANES, n_rows + 1, BF16_SUBLANES):
        if n_rows % rows == 0 and n_rows // rows <= n_steps:
            return rows
    raise ValueError(f"cannot split {n_rows} rows over {n_steps} steps")


def _ffn(x, norm_w, w_gate, w_up, w_down, theta, later_weights, seq):
    t = x.shape[0]
    n_steps = t // FFN_ROWS
    pos_rows = seq // n_steps
    half = RET_DIM // 2
    assert seq % n_steps == 0 and pos_rows % LANES == 0
    rows = pl.BlockSpec((FFN_ROWS, D_MODEL), lambda i: (i, 0))

    cast_specs, cast_shapes = [], []
    for w in later_weights:
        r = _cast_rows(w.shape[0], n_steps)
        steps = w.shape[0] // r
        cast_specs.append(pl.BlockSpec((r, w.shape[1]), lambda i, last=steps - 1: (jnp.minimum(i, last), 0)))
        cast_shapes.append(jax.ShapeDtypeStruct(w.shape, BF16))
    table = pl.BlockSpec((pos_rows, half), lambda i: (i, 0))
    table_t = pl.BlockSpec((half, pos_rows), lambda i: (0, i))

    scratch = [((FFN_ROWS, D_FF), BF16), ((pos_rows, half), F32), ((pos_rows, half), F32)]
    vmem_limit = _vmem_limit(
        resident=[(w.shape, w.dtype) for w in (norm_w, w_gate, w_up, w_down)] + [((1, half), F32)],
        streamed=[((FFN_ROWS, D_MODEL), F32)] * 2 + [((pos_rows, half), F32)] * 4
                 + [(spec.block_shape, dt) for spec in cast_specs for dt in (F32, BF16)],
        scratch=scratch)
    outs = pl.pallas_call(
        functools.partial(_ffn_body, n_cast=len(later_weights), pos_rows=pos_rows),
        grid=(n_steps,),
        in_specs=[rows] + _ffn_weight_specs() + [_resident((1, half))] + cast_specs,
        out_specs=[rows] + cast_specs + [table, table, table_t, table_t],
        out_shape=[jax.ShapeDtypeStruct((t, D_MODEL), F32)] + cast_shapes
                  + [jax.ShapeDtypeStruct((seq, half), F32)] * 2 + [jax.ShapeDtypeStruct((half, seq), F32)] * 2,
        scratch_shapes=[pltpu.VMEM(shape, dtype) for shape, dtype in scratch],
        compiler_params=pltpu.CompilerParams(dimension_semantics=("arbitrary",), vmem_limit_bytes=vmem_limit),
        name="ffn",
    )(x, norm_w, w_gate, w_up, w_down, theta.reshape(1, half), *later_weights)
    n = len(later_weights)
    return outs[0], outs[1:1 + n], outs[1 + n:]


def _log_gamma(logit, shape):
    return jax.nn.log_sigmoid(jnp.full(shape, logit, F32))


def _zero_state_at_sequence_start(st_ref):
    @pl.when(pl.program_id(1) == 0)
    def _():
        st_ref[...] = jnp.zeros_like(st_ref)


def _state_update(st_ref, hd, kt, v, k_scale, block_decay):
    kd = (kt.astype(F32) * k_scale).astype(BF16)
    st_ref[hd] = st_ref[hd] * block_decay + _dot(kd, v)


def _retention_backward_head(dl_ref, hd, q_ref, kt_ref, v_ref, o_ref, st_ref):
    cols = slice(hd * RET_DIM, (hd + 1) * RET_DIM)
    row_idx = lax.broadcasted_iota(jnp.int32, (RET_BLOCK, RET_DIM), 0).astype(F32)
    lane_idx = lax.broadcasted_iota(jnp.int32, (RET_DIM, RET_BLOCK), 1).astype(F32)
    logit = dl_ref[BACKWARD, hd]
    q_scale = jnp.exp((RET_BLOCK - row_idx) * _log_gamma(logit, (RET_BLOCK, RET_DIM)))
    k_scale = jnp.exp(lane_idx * _log_gamma(logit, (RET_DIM, RET_BLOCK)))
    block_decay = jnp.exp(RET_BLOCK * _log_gamma(logit, (1, RET_DIM)))
    for blk in reversed(range(q_ref.shape[0] // RET_BLOCK)):
        rows = slice(blk * RET_BLOCK, (blk + 1) * RET_BLOCK)
        o_ref[rows, cols] = (_dot(q_ref[rows, cols], st_ref[hd].astype(BF16)) * q_scale).astype(BF16)
        _state_update(st_ref, hd, kt_ref[cols, rows], v_ref[rows, cols], k_scale, block_decay)


def _mixer_in_body(dl_ref, x_ref, nw_ref, w_ref, b_ref, lng_ref, lnb_ref, ws_ref, bs_ref, wa_ref,
                   cos_ref, sin_ref, cost_ref, sint_ref,
                   pk_ref, kt_ref, va_ref, vn_ref, a_ref, st_ref):
    q_ref, v_ref, sg_ref, sgb_ref, mixa_ref, cb_ref = _packed_views(pk_ref)
    _zero_state_at_sequence_start(st_ref)
    h = _rms(x_ref[...], nw_ref[...]).astype(BF16)
    half = RET_DIM // 2

    def proj(seg, cols):
        wcols = slice(seg * D_MODEL + cols.start, seg * D_MODEL + cols.stop)
        return _dot(h, w_ref[:, wcols]) + b_ref[:, wcols]

    cos, sin = cos_ref[...], sin_ref[...]
    cost, sint = cost_ref[...], sint_ref[...]
    k_scale = RET_DIM ** -0.5

    def retention_head(hd):
        cols = _col_chunks()[hd]
        t = proj(SEG_Q, cols)
        t1, t2 = t[:, :half], t[:, half:]
        q_ref[:, cols.start:cols.start + half] = (t1 * cos - t2 * sin).astype(BF16)
        q_ref[:, cols.start + half:cols.stop] = (t2 * cos + t1 * sin).astype(BF16)
        tt = proj(SEG_K, cols).T
        t1, t2 = tt[:half], tt[half:]
        kt_ref[cols.start:cols.start + half, :] = ((t1 * cost - t2 * sint) * k_scale).astype(BF16)
        kt_ref[cols.start + half:cols.stop, :] = ((t2 * cost + t1 * sint) * k_scale).astype(BF16)
        v_ref[:, cols] = proj(SEG_VR, cols).astype(BF16)
        sg_ref[:, cols] = _silu(proj(SEG_G, cols)).astype(BF16)
        sgb_ref[:, cols] = jax.nn.sigmoid(proj(SEG_GATE_B, cols)).astype(BF16)
        _retention_backward_head(dl_ref, hd, q_ref, kt_ref, v_ref, cb_ref, st_ref)

    for cols in _col_chunks():
        va_ref[:, cols] = _gelu(proj(SEG_V, cols))
    for hd in range(HEADS_BEFORE_LAYER_NORM):
        retention_head(hd)
    va = va_ref[...]
    mu = jnp.mean(va, axis=-1, keepdims=True)
    vc = va - mu
    var = jnp.mean(vc * vc, axis=-1, keepdims=True)
    vn_ref[...] = (vc * lax.rsqrt(var + NORM_EPS) * lng_ref[...] + lnb_ref[...]).astype(BF16)
    for g, cols in enumerate(_col_chunks()):
        u = _gelu(proj(SEG_U, cols))
        for j in range(MIX_ROWS // (SPATIAL_PAIR * CHUNK)):
            rows = slice(j * SPATIAL_PAIR * CHUNK, (j + 1) * SPATIAL_PAIR * CHUNK)
            s = _dot(ws_ref[g], vn_ref[rows, cols]) + bs_ref[g]
            a_ref[rows, cols] = (u[rows] * s).astype(BF16)
    for hd in range(HEADS_BEFORE_LAYER_NORM, RET_HEADS):
        retention_head(hd)
    a = a_ref[...]
    for cols in _col_chunks():
        mixa_ref[:, cols] = (jax.nn.sigmoid(proj(SEG_GATE_A, cols)) * _dot(a, wa_ref[:, cols])).astype(BF16)


def _paired_spatial_weight(w_s):
    eye = jnp.eye(SPATIAL_PAIR, dtype=w_s.dtype)
    paired = jnp.einsum("pq,gcm->gpcqm", eye, w_s)
    return paired.reshape(w_s.shape[0], SPATIAL_PAIR * CHUNK, SPATIAL_PAIR * CHUNK).astype(BF16)


def _mixer_in(decay_logit, x1, norm_w, w_in, b_in, ln_g, ln_b, w_s, b_s, w_a, cos, sin, cos_t, sin_t,
              batch, seq):
    t = x1.shape[0]
    half = RET_DIM // 2
    steps = seq // MIX_ROWS
    tile = lambda b, j: b * steps + steps - 1 - j
    rows = pl.BlockSpec((MIX_ROWS, D_MODEL), lambda b, j: (tile(b, j), 0))
    lanes = pl.BlockSpec((D_MODEL, MIX_ROWS), lambda b, j: (0, tile(b, j)))
    pos = pl.BlockSpec((MIX_ROWS, half), lambda b, j: (steps - 1 - j, 0))
    pos_t = pl.BlockSpec((half, MIX_ROWS), lambda b, j: (0, steps - 1 - j))
    packed = pl.BlockSpec((MIX_ROWS, N_PACK * D_MODEL), lambda b, j: (tile(b, j), 0))
    width = w_in.shape[1]
    out_t = jax.ShapeDtypeStruct((D_MODEL, t), BF16)
    scratch = [((MIX_ROWS, D_MODEL), F32), ((MIX_ROWS, D_MODEL), BF16), ((MIX_ROWS, D_MODEL), BF16),
               ((RET_HEADS, RET_DIM, RET_DIM), F32)]
    vmem_limit = _vmem_limit(
        resident=[(a.shape, a.dtype) for a in (norm_w, w_in, b_in, ln_g, ln_b, w_s, w_a)]
                 + [(b_s.shape[:-1] + (LANES,), b_s.dtype)],
        streamed=[((MIX_ROWS, D_MODEL), F32), ((MIX_ROWS, N_PACK * D_MODEL), BF16), ((D_MODEL, MIX_ROWS), BF16)]
                 + [((MIX_ROWS, half), F32)] * 4,
        scratch=scratch)
    return pl.pallas_call(
        _mixer_in_body,
        grid=(batch, steps),
        in_specs=[pl.BlockSpec(memory_space=pltpu.SMEM),
                  rows, _resident((1, D_MODEL)), _resident((D_MODEL, width)), _resident((1, width)),
                  _resident((1, D_MODEL)), _resident((1, D_MODEL)), _resident(w_s.shape), _resident(b_s.shape),
                  _resident((D_MODEL, D_MODEL)), pos, pos, pos_t, pos_t],
        out_specs=[packed, lanes],
        out_shape=[jax.ShapeDtypeStruct((t, N_PACK * D_MODEL), BF16), out_t],
        scratch_shapes=[pltpu.VMEM(shape, dtype) for shape, dtype in scratch],
        compiler_params=pltpu.CompilerParams(dimension_semantics=("arbitrary", "arbitrary"),
                                             vmem_limit_bytes=vmem_limit),
        name="mixer_in",
    )(decay_logit, x1, norm_w, w_in, b_in, ln_g, ln_b, w_s, b_s, w_a, cos, sin, cos_t, sin_t)


def _mixer_out_body(*refs, final_norm):
    dl_ref, pk_ref, kt_ref, x1_ref, wb_ref, wo_ref, nw_ref, wg_ref, wu_ref, wd_ref = refs[:10]
    fn_ref = refs[10] if final_norm else None
    q_ref, v_ref, sg_ref, sgb_ref, mixa_ref, cb_ref = _packed_views(pk_ref)
    o_ref, st_ref, p_ref, r_ref, act_ref = refs[-5:]
    _zero_state_at_sequence_start(st_ref)
    row_idx = lax.broadcasted_iota(jnp.int32, (RET_BLOCK, RET_DIM), 0).astype(F32)
    lane_idx = lax.broadcasted_iota(jnp.int32, (RET_DIM, RET_BLOCK), 1).astype(F32)
    diff = (lax.broadcasted_iota(jnp.int32, (RET_BLOCK, RET_BLOCK), 0)
            - lax.broadcasted_iota(jnp.int32, (RET_BLOCK, RET_BLOCK), 1)).astype(F32)
    n_blocks = RET_ROWS // RET_BLOCK
    head_cols = lambda hd: slice(hd * RET_DIM, (hd + 1) * RET_DIM)
    block_rows = lambda blk: slice(blk * RET_BLOCK, (blk + 1) * RET_BLOCK)

    for hd in range(RET_HEADS):
        lg_f = _log_gamma(dl_ref[FORWARD, hd], (RET_BLOCK, RET_BLOCK))
        lg_b = _log_gamma(dl_ref[BACKWARD, hd], (RET_BLOCK, RET_BLOCK))
        decay = jnp.exp(jnp.where(diff >= 0, diff * lg_f, -diff * lg_b))
        for blk in range(n_blocks):
            scores = _dot(q_ref[block_rows(blk), head_cols(hd)], kt_ref[head_cols(hd), block_rows(blk)])
            p_ref[hd * n_blocks + blk] = (scores * decay).astype(BF16)

    for blk in range(n_blocks):
        rows = block_rows(blk)
        for hd in range(RET_HEADS):
            cols = head_cols(hd)
            logit = dl_ref[FORWARD, hd]
            q_scale = jnp.exp((row_idx + 1.0) * _log_gamma(logit, (RET_BLOCK, RET_DIM)))
            k_scale = jnp.exp((RET_BLOCK - 1.0 - lane_idx) * _log_gamma(logit, (RET_DIM, RET_BLOCK)))
            block_decay = jnp.exp(RET_BLOCK * _log_gamma(logit, (1, RET_DIM)))
            v = v_ref[rows, cols]
            r = (_dot(p_ref[hd * n_blocks + blk], v)
                 + _dot(q_ref[rows, cols], st_ref[hd].astype(BF16)) * q_scale
                 + cb_ref[rows, cols].astype(F32))
            r = r * lax.rsqrt(jnp.mean(r * r, axis=-1, keepdims=True) + NORM_EPS)
            r_ref[rows, cols] = (r * sg_ref[rows, cols].astype(F32)).astype(BF16)
            _state_update(st_ref, hd, kt_ref[cols, rows], v, k_scale, block_decay)
    mix = (sgb_ref[...].astype(F32) * _dot(r_ref[...], wb_ref[...]) + mixa_ref[...].astype(F32)).astype(BF16)
    x2 = x1_ref[...] + _dot(mix, wo_ref[...])
    y = _swiglu_half_step(x2, nw_ref, wg_ref, wu_ref, wd_ref, act_ref)
    if final_norm:
        y = _rms(y, fn_ref[...])
    o_ref[...] = y


def _mixer_out(decay_logit, packed, kt, x1, w_b, w_o, norm_w, w_gate, w_up, w_down, final_w, batch, seq):
    steps = seq // RET_ROWS
    rows = pl.BlockSpec((RET_ROWS, D_MODEL), lambda b, j: (b * steps + j, 0))
    packed_rows = pl.BlockSpec((RET_ROWS, N_PACK * D_MODEL), lambda b, j: (b * steps + j, 0))
    lanes = pl.BlockSpec((D_MODEL, RET_ROWS), lambda b, j: (0, b * steps + j))
    in_specs = ([pl.BlockSpec(memory_space=pltpu.SMEM), packed_rows, lanes, rows]
                + [_resident((D_MODEL, D_MODEL)), _resident((D_MODEL, D_MODEL))] + _ffn_weight_specs())
    args = [decay_logit, packed, kt, x1, w_b, w_o, norm_w, w_gate, w_up, w_down]
    if final_w is not None:
        in_specs.append(_resident((1, D_MODEL)))
        args.append(final_w)
    scratch = [((RET_HEADS, RET_DIM, RET_DIM), F32),
               ((RET_HEADS * (RET_ROWS // RET_BLOCK), RET_BLOCK, RET_BLOCK), BF16),
               ((RET_ROWS, D_MODEL), BF16), ((RET_ROWS, D_FF), BF16)]
    vmem_limit = _vmem_limit(
        resident=[(a.shape, a.dtype) for a in args[4:]],
        streamed=[((RET_ROWS, N_PACK * D_MODEL), BF16), ((D_MODEL, RET_ROWS), BF16)]
                 + [((RET_ROWS, D_MODEL), F32)] * 2,
        scratch=scratch)
    return pl.pallas_call(
        functools.partial(_mixer_out_body, final_norm=final_w is not None),
        grid=(batch, steps),
        in_specs=in_specs,
        out_specs=rows,
        out_shape=jax.ShapeDtypeStruct(x1.shape, F32),
        scratch_shapes=[pltpu.VMEM(shape, dtype) for shape, dtype in scratch],
        compiler_params=pltpu.CompilerParams(dimension_semantics=("arbitrary", "arbitrary"),
                                             vmem_limit_bytes=vmem_limit),
        name="mixer_out",
    )(*args)


def kernel(x, ffn1_norm, ffn1_w_gate, ffn1_w_up, ffn1_w_down, mix_norm, w_in, b_in, sgu_norm_g, sgu_norm_b, sgu_w_s, sgu_b_s, ret_decay_logit, w_branch_a, w_branch_b, w_out, ffn2_norm, ffn2_w_gate, ffn2_w_up, ffn2_w_down, final_norm):
    batch, seq, _ = x.shape
    depth = ffn1_norm.shape[0]
    assert depth >= 1 and seq % RET_ROWS == 0 and seq % MIX_ROWS == 0 and (batch * seq) % FFN_ROWS == 0
    tokens = batch * seq
    row = lambda p: p.reshape(1, -1)
    theta = ROPE_BASE ** (-jnp.arange(0, RET_DIM, 2, dtype=F32) / RET_DIM)

    xt = x.reshape(tokens, D_MODEL)
    for l in range(depth):
        last = l == depth - 1
        x1, (w_in_bf, w_a, w_b, w_o, w_gate2, w_up2, w_down2), (cos, sin, cos_t, sin_t) = _ffn(
            xt, row(ffn1_norm[l]), ffn1_w_gate[l], ffn1_w_up[l], ffn1_w_down[l], theta,
            [w_in[l], w_branch_a[l], w_branch_b[l], w_out[l], ffn2_w_gate[l], ffn2_w_up[l], ffn2_w_down[l]], seq)
        packed, kt = _mixer_in(
            ret_decay_logit[l], x1, row(mix_norm[l]), w_in_bf, row(b_in[l]),
            row(sgu_norm_g[l]), row(sgu_norm_b[l]), _paired_spatial_weight(sgu_w_s[l]),
            jnp.tile(sgu_b_s[l], (1, SPATIAL_PAIR))[..., None], w_a, cos, sin, cos_t, sin_t, batch, seq)
        xt = _mixer_out(ret_decay_logit[l], packed, kt, x1, w_b, w_o,
                        row(ffn2_norm[l]), w_gate2, w_up2, w_down2,
                        row(final_norm) if last else None, batch, seq)
    return xt.reshape(batch, seq, D_MODEL)
```

```python
import functools
import math

import jax
import jax.numpy as jnp
import numpy as np
from jax import lax
from jax.experimental import pallas as pl
from jax.experimental.pallas import tpu as pltpu

D_MODEL = 1024
D_FF = 2816
CHUNK = 128
SPATIAL_PAIR = 2
SGU_GROUPS = 4
SGU_GROUP_DIM = D_MODEL // SGU_GROUPS
RET_HEADS = 4
RET_DIM = 256
SEG_U, SEG_V, SEG_Q, SEG_K, SEG_VR, SEG_G, SEG_GATE_A, SEG_GATE_B = range(8)
FORWARD, BACKWARD = 0, 1
PACK_Q, PACK_V, PACK_SG, PACK_SGB, PACK_MIXA, PACK_CB = range(6)
N_PACK = 6
ROPE_BASE = 10000.0
NORM_EPS = 1e-6

MXU_COLS = 256
BF16_SUBLANES = 16
LANES = 128
FFN_ROWS = 512
MIX_ROWS = 512
RET_ROWS = 512
RET_BLOCK = 256
HEADS_BEFORE_LAYER_NORM = 1
MIB = 1024 * 1024
V7X_VMEM_BYTES = 64 * MIB
VMEM_LEFT_TO_COMPILER = 4 * MIB
SPILL_ALLOWANCE = 8 * MIB

BF16 = jnp.bfloat16
F32 = jnp.float32

assert RET_DIM == MXU_COLS and SGU_GROUP_DIM == MXU_COLS


def _rms(x, g):
    return x * lax.rsqrt(jnp.mean(x * x, axis=-1, keepdims=True) + NORM_EPS) * g


def _gelu(x):
    return 0.5 * x * (1.0 + lax.erf(x * np.float32(np.sqrt(0.5))))


def _silu(x):
    return x * jax.nn.sigmoid(x)


def _dot(a, b):
    return jnp.dot(a, b, preferred_element_type=F32)


def _col_chunks(width=D_MODEL):
    return [slice(c, c + MXU_COLS) for c in range(0, width, MXU_COLS)]


def _packed_views(pk_ref):
    return [pk_ref.at[:, i * D_MODEL:(i + 1) * D_MODEL] for i in range(N_PACK)]


def _vmem_limit(resident, streamed, scratch):
    nbytes = lambda items: sum(math.prod(shape) * jnp.dtype(dtype).itemsize for shape, dtype in items)
    need = nbytes(resident) + 2 * nbytes(streamed) + nbytes(scratch) + SPILL_ALLOWANCE
    return min(need, V7X_VMEM_BYTES - VMEM_LEFT_TO_COMPILER)


def _resident(shape):
    return pl.BlockSpec(shape, lambda *_: (0,) * len(shape), pipeline_mode=pl.Buffered(1))


def _swiglu_half_step(x, nw_ref, wg_ref, wu_ref, wd_ref, act_ref):
    h = (x * nw_ref[...]).astype(BF16)
    inv_rms = lax.rsqrt(jnp.mean(x * x, axis=-1, keepdims=True) + NORM_EPS)
    for cols in _col_chunks(D_FF):
        g = _dot(h, wg_ref[:, cols].astype(BF16)) * inv_rms
        u = _dot(h, wu_ref[:, cols].astype(BF16)) * inv_rms
        act_ref[:, cols] = (_silu(g) * u).astype(BF16)
    act = act_ref[...]
    down = _dot(act, wd_ref[...].astype(BF16))
    return x + 0.5 * down


def _ffn_body(*refs, n_cast, pos_rows):
    n = n_cast
    x_ref, nw_ref, wg_ref, wu_ref, wd_ref, theta_ref = refs[:6]
    srcs = refs[6:6 + n]
    o_ref = refs[6 + n]
    dsts = refs[7 + n:7 + 2 * n]
    cos_ref, sin_ref, cost_ref, sint_ref, act_ref, cos0_ref, sin0_ref = refs[7 + 2 * n:]
    step = pl.program_id(0)
    theta = theta_ref[...]

    @pl.when(step == 0)
    def _():
        ang0 = lax.broadcasted_iota(jnp.int32, (pos_rows, RET_DIM // 2), 0).astype(F32) * theta
        cos0_ref[...] = jnp.cos(ang0)
        sin0_ref[...] = jnp.sin(ang0)

    for src, dst in zip(srcs, dsts):
        dst[...] = src[...].astype(BF16)

    ang1 = (step * pos_rows).astype(F32) * theta
    cos1, sin1 = jnp.cos(ang1), jnp.sin(ang1)
    cos0, sin0 = cos0_ref[...], sin0_ref[...]
    cos = cos1 * cos0 - sin1 * sin0
    sin = sin1 * cos0 + cos1 * sin0
    cos_ref[...] = cos
    sin_ref[...] = sin
    cost_ref[...] = cos.T
    sint_ref[...] = sin.T

    o_ref[...] = _swiglu_half_step(x_ref[...], nw_ref, wg_ref, wu_ref, wd_ref, act_ref)


def _ffn_weight_specs():
    return [_resident((1, D_MODEL)), _resident((D_MODEL, D_FF)), _resident((D_MODEL, D_FF)),
            _resident((D_FF, D_MODEL))]


def _cast_rows(n_rows, n_steps):
    for rows in range(BF16_SUBLANES, n_rows + 1, BF16_SUBLANES):
        if n_rows % rows == 0 and n_rows // rows <= n_steps:
            return rows
    raise ValueError(f"cannot split {n_rows} rows over {n_steps} steps")


def _ffn(x, norm_w, w_gate, w_up, w_down, theta, later_weights, seq):
    t = x.shape[0]
    n_steps = t // FFN_ROWS
    pos_rows = seq // n_steps
    half = RET_DIM // 2
    assert seq % n_steps == 0 and pos_rows % LANES == 0
    rows = pl.BlockSpec((FFN_ROWS, D_MODEL), lambda i: (i, 0))

    cast_specs, cast_shapes = [], []
    for w in later_weights:
        r = _cast_rows(w.shape[0], n_steps)
        steps = w.shape[0] // r
        cast_specs.append(pl.BlockSpec((r, w.shape[1]), lambda i, last=steps - 1: (jnp.minimum(i, last), 0)))
        cast_shapes.append(jax.ShapeDtypeStruct(w.shape, BF16))
    table = pl.BlockSpec((pos_rows, half), lambda i: (i, 0))
    table_t = pl.BlockSpec((half, pos_rows), lambda i: (0, i))

    scratch = [((FFN_ROWS, D_FF), BF16), ((pos_rows, half), F32), ((pos_rows, half), F32)]
    vmem_limit = _vmem_limit(
        resident=[(w.shape, w.dtype) for w in (norm_w, w_gate, w_up, w_down)] + [((1, half), F32)],
        streamed=[((FFN_ROWS, D_MODEL), F32)] * 2 + [((pos_rows, half), F32)] * 4
                 + [(spec.block_shape, dt) for spec in cast_specs for dt in (F32, BF16)],
        scratch=scratch)
    outs = pl.pallas_call(
        functools.partial(_ffn_body, n_cast=len(later_weights), pos_rows=pos_rows),
        grid=(n_steps,),
        in_specs=[rows] + _ffn_weight_specs() + [_resident((1, half))] + cast_specs,
        out_specs=[rows] + cast_specs + [table, table, table_t, table_t],
        out_shape=[jax.ShapeDtypeStruct((t, D_MODEL), F32)] + cast_shapes
                  + [jax.ShapeDtypeStruct((seq, half), F32)] * 2 + [jax.ShapeDtypeStruct((half, seq), F32)] * 2,
        scratch_shapes=[pltpu.VMEM(shape, dtype) for shape, dtype in scratch],
        compiler_params=pltpu.CompilerParams(dimension_semantics=("arbitrary",), vmem_limit_bytes=vmem_limit),
        name="ffn",
    )(x, norm_w, w_gate, w_up, w_down, theta.reshape(1, half), *later_weights)
    n = len(later_weights)
    return outs[0], outs[1:1 + n], outs[1 + n:]


def _log_gamma(logit, shape):
    return jax.nn.log_sigmoid(jnp.full(shape, logit, F32))


def _zero_state_at_sequence_start(st_ref):
    @pl.when(pl.program_id(1) == 0)
    def _():
        st_ref[...] = jnp.zeros_like(st_ref)


def _state_update(st_ref, hd, kt, v, k_scale, block_decay):
    kd = (kt.astype(F32) * k_scale).astype(BF16)
    st_ref[hd] = st_ref[hd] * block_decay + _dot(kd, v)


def _retention_backward_head(dl_ref, hd, q_ref, kt_ref, v_ref, o_ref, st_ref):
    cols = slice(hd * RET_DIM, (hd + 1) * RET_DIM)
    row_idx = lax.broadcasted_iota(jnp.int32, (RET_BLOCK, RET_DIM), 0).astype(F32)
    lane_idx = lax.broadcasted_iota(jnp.int32, (RET_DIM, RET_BLOCK), 1).astype(F32)
    logit = dl_ref[BACKWARD, hd]
    q_scale = jnp.exp((RET_BLOCK - row_idx) * _log_gamma(logit, (RET_BLOCK, RET_DIM)))
    k_scale = jnp.exp(lane_idx * _log_gamma(logit, (RET_DIM, RET_BLOCK)))
    block_decay = jnp.exp(RET_BLOCK * _log_gamma(logit, (1, RET_DIM)))
    for blk in reversed(range(q_ref.shape[0] // RET_BLOCK)):
        rows = slice(blk * RET_BLOCK, (blk + 1) * RET_BLOCK)
        o_ref[rows, cols] = (_dot(q_ref[rows, cols], st_ref[hd].astype(BF16)) * q_scale).astype(BF16)
        _state_update(st_ref, hd, kt_ref[cols, rows], v_ref[rows, cols], k_scale, block_decay)


def _mixer_in_body(dl_ref, x_ref, nw_ref, w_ref, b_ref, lng_ref, lnb_ref, ws_ref, bs_ref, wa_ref,
                   cos_ref, sin_ref, cost_ref, sint_ref,
                   pk_ref, kt_ref, va_ref, vn_ref, a_ref, st_ref):
    q_ref, v_ref, sg_ref, sgb_ref, mixa_ref, cb_ref = _packed_views(pk_ref)
    _zero_state_at_sequence_start(st_ref)
    x = x_ref[...]
    inv_rms = lax.rsqrt(jnp.mean(x * x, axis=-1, keepdims=True) + NORM_EPS)
    h = (x * inv_rms * nw_ref[...]).astype(BF16)
    h_unscaled = (x * nw_ref[...]).astype(BF16)
    half = RET_DIM // 2

    def proj(seg, cols, scale_result=False):
        wcols = slice(seg * D_MODEL + cols.start, seg * D_MODEL + cols.stop)
        if scale_result:
            return _dot(h_unscaled, w_ref[:, wcols]) * inv_rms + b_ref[:, wcols]
        return _dot(h, w_ref[:, wcols]) + b_ref[:, wcols]

    cos, sin = cos_ref[...], sin_ref[...]
    cost, sint = cost_ref[...], sint_ref[...]
    k_scale = RET_DIM ** -0.5

    def retention_head(hd):
        cols = _col_chunks()[hd]
        t = proj(SEG_Q, cols)
        t1, t2 = t[:, :half], t[:, half:]
        q_ref[:, cols.start:cols.start + half] = (t1 * cos - t2 * sin).astype(BF16)
        q_ref[:, cols.start + half:cols.stop] = (t2 * cos + t1 * sin).astype(BF16)
        tt = proj(SEG_K, cols).T
        t1, t2 = tt[:half], tt[half:]
        kt_ref[cols.start:cols.start + half, :] = ((t1 * cost - t2 * sint) * k_scale).astype(BF16)
        kt_ref[cols.start + half:cols.stop, :] = ((t2 * cost + t1 * sint) * k_scale).astype(BF16)
        v_ref[:, cols] = proj(SEG_VR, cols).astype(BF16)
        sg_ref[:, cols] = _silu(proj(SEG_G, cols)).astype(BF16)
        sgb_ref[:, cols] = jax.nn.sigmoid(proj(SEG_GATE_B, cols)).astype(BF16)
        _retention_backward_head(dl_ref, hd, q_ref, kt_ref, v_ref, cb_ref, st_ref)

    for cols in _col_chunks():
        va_ref[:, cols] = _gelu(proj(SEG_V, cols, scale_result=True))
    for hd in range(HEADS_BEFORE_LAYER_NORM):
        retention_head(hd)
    va = va_ref[...]
    mu = jnp.mean(va, axis=-1, keepdims=True)
    vc = va - mu
    var = jnp.mean(vc * vc, axis=-1, keepdims=True)
    vn_ref[...] = (vc * lax.rsqrt(var + NORM_EPS) * lng_ref[...] + lnb_ref[...]).astype(BF16)
    for g, cols in enumerate(_col_chunks()):
        u = _gelu(proj(SEG_U, cols))
        for j in range(MIX_ROWS // (SPATIAL_PAIR * CHUNK)):
            rows = slice(j * SPATIAL_PAIR * CHUNK, (j + 1) * SPATIAL_PAIR * CHUNK)
            s = _dot(ws_ref[g], vn_ref[rows, cols]) + bs_ref[g]
            a_ref[rows, cols] = (u[rows] * s).astype(BF16)
    for hd in range(HEADS_BEFORE_LAYER_NORM, RET_HEADS):
        retention_head(hd)
    a = a_ref[...]
    for cols in _col_chunks():
        mixa_ref[:, cols] = (jax.nn.sigmoid(proj(SEG_GATE_A, cols)) * _dot(a, wa_ref[:, cols])).astype(BF16)


def _paired_spatial_weight(w_s):
    eye = jnp.eye(SPATIAL_PAIR, dtype=w_s.dtype)
    paired = jnp.einsum("pq,gcm->gpcqm", eye, w_s)
    return paired.reshape(w_s.shape[0], SPATIAL_PAIR * CHUNK, SPATIAL_PAIR * CHUNK).astype(BF16)


def _mixer_in(decay_logit, x1, norm_w, w_in, b_in, ln_g, ln_b, w_s, b_s, w_a, cos, sin, cos_t, sin_t,
              batch, seq):
    t = x1.shape[0]
    half = RET_DIM // 2
    steps = seq // MIX_ROWS
    tile = lambda b, j: b * steps + steps - 1 - j
    rows = pl.BlockSpec((MIX_ROWS, D_MODEL), lambda b, j: (tile(b, j), 0))
    lanes = pl.BlockSpec((D_MODEL, MIX_ROWS), lambda b, j: (0, tile(b, j)))
    pos = pl.BlockSpec((MIX_ROWS, half), lambda b, j: (steps - 1 - j, 0))
    pos_t = pl.BlockSpec((half, MIX_ROWS), lambda b, j: (0, steps - 1 - j))
    packed = pl.BlockSpec((MIX_ROWS, N_PACK * D_MODEL), lambda b, j: (tile(b, j), 0))
    width = w_in.shape[1]
    out_t = jax.ShapeDtypeStruct((D_MODEL, t), BF16)
    scratch = [((MIX_ROWS, D_MODEL), F32), ((MIX_ROWS, D_MODEL), BF16), ((MIX_ROWS, D_MODEL), BF16),
               ((RET_HEADS, RET_DIM, RET_DIM), F32)]
    vmem_limit = _vmem_limit(
        resident=[(a.shape, a.dtype) for a in (norm_w, w_in, b_in, ln_g, ln_b, w_s, w_a)]
                 + [(b_s.shape[:-1] + (LANES,), b_s.dtype)],
        streamed=[((MIX_ROWS, D_MODEL), F32), ((MIX_ROWS, N_PACK * D_MODEL), BF16), ((D_MODEL, MIX_ROWS), BF16)]
                 + [((MIX_ROWS, half), F32)] * 4,
        scratch=scratch)
    return pl.pallas_call(
        _mixer_in_body,
        grid=(batch, steps),
        in_specs=[pl.BlockSpec(memory_space=pltpu.SMEM),
                  rows, _resident((1, D_MODEL)), _resident((D_MODEL, width)), _resident((1, width)),
                  _resident((1, D_MODEL)), _resident((1, D_MODEL)), _resident(w_s.shape), _resident(b_s.shape),
                  _resident((D_MODEL, D_MODEL)), pos, pos, pos_t, pos_t],
        out_specs=[packed, lanes],
        out_shape=[jax.ShapeDtypeStruct((t, N_PACK * D_MODEL), BF16), out_t],
        scratch_shapes=[pltpu.VMEM(shape, dtype) for shape, dtype in scratch],
        compiler_params=pltpu.CompilerParams(dimension_semantics=("arbitrary", "arbitrary"),
                                             vmem_limit_bytes=vmem_limit),
        name="mixer_in",
    )(decay_logit, x1, norm_w, w_in, b_in, ln_g, ln_b, w_s, b_s, w_a, cos, sin, cos_t, sin_t)


def _mixer_out_body(*refs, final_norm):
    dl_ref, pk_ref, kt_ref, x1_ref, wb_ref, wo_ref, nw_ref, wg_ref, wu_ref, wd_ref = refs[:10]
    fn_ref = refs[10] if final_norm else None
    q_ref, v_ref, sg_ref, sgb_ref, mixa_ref, cb_ref = _packed_views(pk_ref)
    o_ref, st_ref, p_ref, r_ref, act_ref = refs[-5:]
    _zero_state_at_sequence_start(st_ref)
    row_idx = lax.broadcasted_iota(jnp.int32, (RET_BLOCK, RET_DIM), 0).astype(F32)
    lane_idx = lax.broadcasted_iota(jnp.int32, (RET_DIM, RET_BLOCK), 1).astype(F32)
    diff = (lax.broadcasted_iota(jnp.int32, (RET_BLOCK, RET_BLOCK), 0)
            - lax.broadcasted_iota(jnp.int32, (RET_BLOCK, RET_BLOCK), 1)).astype(F32)
    n_blocks = RET_ROWS // RET_BLOCK
    head_cols = lambda hd: slice(hd * RET_DIM, (hd + 1) * RET_DIM)
    block_rows = lambda blk: slice(blk * RET_BLOCK, (blk + 1) * RET_BLOCK)

    for hd in range(RET_HEADS):
        lg_f = _log_gamma(dl_ref[FORWARD, hd], (RET_BLOCK, RET_BLOCK))
        lg_b = _log_gamma(dl_ref[BACKWARD, hd], (RET_BLOCK, RET_BLOCK))
        decay = jnp.exp(jnp.where(diff >= 0, diff * lg_f, -diff * lg_b))
        for blk in range(n_blocks):
            scores = _dot(q_ref[block_rows(blk), head_cols(hd)], kt_ref[head_cols(hd), block_rows(blk)])
            p_ref[hd * n_blocks + blk] = (scores * decay).astype(BF16)

    for blk in range(n_blocks):
        rows = block_rows(blk)
        for hd in range(RET_HEADS):
            cols = head_cols(hd)
            logit = dl_ref[FORWARD, hd]
            q_scale = jnp.exp((row_idx + 1.0) * _log_gamma(logit, (RET_BLOCK, RET_DIM)))
            k_scale = jnp.exp((RET_BLOCK - 1.0 - lane_idx) * _log_gamma(logit, (RET_DIM, RET_BLOCK)))
            block_decay = jnp.exp(RET_BLOCK * _log_gamma(logit, (1, RET_DIM)))
            v = v_ref[rows, cols]
            r = (_dot(p_ref[hd * n_blocks + blk], v)
                 + _dot(q_ref[rows, cols], st_ref[hd].astype(BF16)) * q_scale
                 + cb_ref[rows, cols].astype(F32))
            r = r * lax.rsqrt(jnp.mean(r * r, axis=-1, keepdims=True) + NORM_EPS)
            r_ref[rows, cols] = (r * sg_ref[rows, cols].astype(F32)).astype(BF16)
            _state_update(st_ref, hd, kt_ref[cols, rows], v, k_scale, block_decay)
    mix = (sgb_ref[...].astype(F32) * _dot(r_ref[...], wb_ref[...]) + mixa_ref[...].astype(F32)).astype(BF16)
    x2 = x1_ref[...] + _dot(mix, wo_ref[...])
    y = _swiglu_half_step(x2, nw_ref, wg_ref, wu_ref, wd_ref, act_ref)
    if final_norm:
        y = _rms(y, fn_ref[...])
    o_ref[...] = y


def _mixer_out(decay_logit, packed, kt, x1, w_b, w_o, norm_w, w_gate, w_up, w_down, final_w, batch, seq):
    steps = seq // RET_ROWS
    rows = pl.BlockSpec((RET_ROWS, D_MODEL), lambda b, j: (b * steps + j, 0))
    packed_rows = pl.BlockSpec((RET_ROWS, N_PACK * D_MODEL), lambda b, j: (b * steps + j, 0))
    lanes = pl.BlockSpec((D_MODEL, RET_ROWS), lambda b, j: (0, b * steps + j))
    in_specs = ([pl.BlockSpec(memory_space=pltpu.SMEM), packed_rows, lanes, rows]
                + [_resident((D_MODEL, D_MODEL)), _resident((D_MODEL, D_MODEL))] + _ffn_weight_specs())
    args = [decay_logit, packed, kt, x1, w_b, w_o, norm_w, w_gate, w_up, w_down]
    if final_w is not None:
        in_specs.append(_resident((1, D_MODEL)))
        args.append(final_w)
    scratch = [((RET_HEADS, RET_DIM, RET_DIM), F32),
               ((RET_HEADS * (RET_ROWS // RET_BLOCK), RET_BLOCK, RET_BLOCK), BF16),
               ((RET_ROWS, D_MODEL), BF16), ((RET_ROWS, D_FF), BF16)]
    vmem_limit = _vmem_limit(
        resident=[(a.shape, a.dtype) for a in args[4:]],
        streamed=[((RET_ROWS, N_PACK * D_MODEL), BF16), ((D_MODEL, RET_ROWS), BF16)]
                 + [((RET_ROWS, D_MODEL), F32)] * 2,
        scratch=scratch)
    return pl.pallas_call(
        functools.partial(_mixer_out_body, final_norm=final_w is not None),
        grid=(batch, steps),
        in_specs=in_specs,
        out_specs=rows,
        out_shape=jax.ShapeDtypeStruct(x1.shape, F32),
        scratch_shapes=[pltpu.VMEM(shape, dtype) for shape, dtype in scratch],
        compiler_params=pltpu.CompilerParams(dimension_semantics=("arbitrary", "arbitrary"),
                                             vmem_limit_bytes=vmem_limit),
        name="mixer_out",
    )(*args)


def kernel(x, ffn1_norm, ffn1_w_gate, ffn1_w_up, ffn1_w_down, mix_norm, w_in, b_in, sgu_norm_g, sgu_norm_b, sgu_w_s, sgu_b_s, ret_decay_logit, w_branch_a, w_branch_b, w_out, ffn2_norm, ffn2_w_gate, ffn2_w_up, ffn2_w_down, final_norm):
    batch, seq, _ = x.shape
    depth = ffn1_norm.shape[0]
    assert depth >= 1 and seq % RET_ROWS == 0 and seq % MIX_ROWS == 0 and (batch * seq) % FFN_ROWS == 0
    tokens = batch * seq
    row = lambda p: p.reshape(1, -1)
    theta = ROPE_BASE ** (-jnp.arange(0, RET_DIM, 2, dtype=F32) / RET_DIM)

    xt = x.reshape(tokens, D_MODEL)
    for l in range(depth):
        last = l == depth - 1
        x1, (w_in_bf, w_a, w_b, w_o, w_gate2, w_up2, w_down2), (cos, sin, cos_t, sin_t) = _ffn(
            xt, row(ffn1_norm[l]), ffn1_w_gate[l], ffn1_w_up[l], ffn1_w_down[l], theta,
            [w_in[l], w_branch_a[l], w_branch_b[l], w_out[l], ffn2_w_gate[l], ffn2_w_up[l], ffn2_w_down[l]], seq)
        packed, kt = _mixer_in(
            ret_decay_logit[l], x1, row(mix_norm[l]), w_in_bf, row(b_in[l]),
            row(sgu_norm_g[l]), row(sgu_norm_b[l]), _paired_spatial_weight(sgu_w_s[l]),
            jnp.tile(sgu_b_s[l], (1, SPATIAL_PAIR))[..., None], w_a, cos, sin, cos_t, sin_t, batch, seq)
        xt = _mixer_out(ret_decay_logit[l], packed, kt, x1, w_b, w_o,
                        row(ffn2_norm[l]), w_gate2, w_up2, w_down2,
                        row(final_norm) if last else None, batch, seq)
    return xt.reshape(batch, seq, D_MODEL)
```

```python
import functools
import math

import jax
import jax.numpy as jnp
import numpy as np
from jax import lax
from jax.experimental import pallas as pl
from jax.experimental.pallas import tpu as pltpu

D_MODEL = 1024
D_FF = 2816
CHUNK = 128
SPATIAL_PAIR = 2
SGU_GROUPS = 4
SGU_GROUP_DIM = D_MODEL // SGU_GROUPS
RET_HEADS = 4
RET_DIM = 256
SEG_U, SEG_V, SEG_Q, SEG_K, SEG_VR, SEG_G, SEG_GATE_A, SEG_GATE_B = range(8)
FORWARD, BACKWARD = 0, 1
PACK_Q, PACK_V, PACK_SG, PACK_SGB, PACK_MIXA, PACK_CB = range(6)
N_PACK = 6
ROPE_BASE = 10000.0
NORM_EPS = 1e-6

MXU_COLS = 256
BF16_SUBLANES = 16
LANES = 128
FFN_ROWS = 512
MIX_ROWS = 512
RET_ROWS = 512
RET_BLOCK = 256
HEADS_BEFORE_LAYER_NORM = 1
MIB = 1024 * 1024
V7X_VMEM_BYTES = 64 * MIB
VMEM_LEFT_TO_COMPILER = 4 * MIB
SPILL_ALLOWANCE = 8 * MIB

BF16 = jnp.bfloat16
F32 = jnp.float32

assert RET_DIM == MXU_COLS and SGU_GROUP_DIM == MXU_COLS


def _rms(x, g):
    return x * lax.rsqrt(jnp.mean(x * x, axis=-1, keepdims=True) + NORM_EPS) * g


def _gelu(x):
    return 0.5 * x * (1.0 + lax.erf(x * np.float32(np.sqrt(0.5))))


def _silu(x):
    return x * jax.nn.sigmoid(x)


def _dot(a, b):
    return jnp.dot(a, b, preferred_element_type=F32)


def _col_chunks(width=D_MODEL):
    return [slice(c, c + MXU_COLS) for c in range(0, width, MXU_COLS)]


def _packed_views(pk_ref):
    return [pk_ref.at[:, i * D_MODEL:(i + 1) * D_MODEL] for i in range(N_PACK)]


def _vmem_limit(resident, streamed, scratch):
    nbytes = lambda items: sum(math.prod(shape) * jnp.dtype(dtype).itemsize for shape, dtype in items)
    need = nbytes(resident) + 2 * nbytes(streamed) + nbytes(scratch) + SPILL_ALLOWANCE
    return min(need, V7X_VMEM_BYTES - VMEM_LEFT_TO_COMPILER)


def _resident(shape):
    return pl.BlockSpec(shape, lambda *_: (0,) * len(shape), pipeline_mode=pl.Buffered(1))


def _swiglu_half_step(x, nw_ref, wg_ref, wu_ref, wd_ref, act_ref):
    h = (x * nw_ref[...]).astype(BF16)
    inv_rms = lax.rsqrt(jnp.mean(x * x, axis=-1, keepdims=True) + NORM_EPS)
    for cols in _col_chunks(D_FF):
        g = _dot(h, wg_ref[:, cols].astype(BF16)) * inv_rms
        u = _dot(h, wu_ref[:, cols].astype(BF16)) * inv_rms
        act_ref[:, cols] = (_silu(g) * u).astype(BF16)
    act = act_ref[...]
    down = _dot(act, wd_ref[...].astype(BF16))
    return x + 0.5 * down


def _ffn_body(*refs, n_cast, pos_rows):
    n = n_cast
    x_ref, nw_ref, wg_ref, wu_ref, wd_ref, theta_ref = refs[:6]
    srcs = refs[6:6 + n]
    o_ref = refs[6 + n]
    dsts = refs[7 + n:7 + 2 * n]
    cos_ref, sin_ref, cost_ref, sint_ref, act_ref, cos0_ref, sin0_ref = refs[7 + 2 * n:]
    step = pl.program_id(0)
    theta = theta_ref[...]

    @pl.when(step == 0)
    def _():
        ang0 = lax.broadcasted_iota(jnp.int32, (pos_rows, RET_DIM // 2), 0).astype(F32) * theta
        cos0_ref[...] = jnp.cos(ang0)
        sin0_ref[...] = jnp.sin(ang0)

    for src, dst in zip(srcs, dsts):
        dst[...] = src[...].astype(BF16)

    ang1 = (step * pos_rows).astype(F32) * theta
    cos1, sin1 = jnp.cos(ang1), jnp.sin(ang1)
    cos0, sin0 = cos0_ref[...], sin0_ref[...]
    cos = cos1 * cos0 - sin1 * sin0
    sin = sin1 * cos0 + cos1 * sin0
    cos_ref[...] = cos
    sin_ref[...] = sin
    cost_ref[...] = cos.T
    sint_ref[...] = sin.T

    o_ref[...] = _swiglu_half_step(x_ref[...], nw_ref, wg_ref, wu_ref, wd_ref, act_ref)


def _ffn_weight_specs():
    return [_resident((1, D_MODEL)), _resident((D_MODEL, D_FF)), _resident((D_MODEL, D_FF)),
            _resident((D_FF, D_MODEL))]


def _cast_rows(n_rows, n_steps):
    for rows in range(BF16_SUBLANES, n_rows + 1, BF16_SUBLANES):
        if n_rows % rows == 0 and n_rows // rows <= n_steps:
            return rows
    raise ValueError(f"cannot split {n_rows} rows over {n_steps} steps")


def _ffn(x, norm_w, w_gate, w_up, w_down, theta, later_weights, seq):
    t = x.shape[0]
    n_steps = t // FFN_ROWS
    pos_rows = seq // n_steps
    half = RET_DIM // 2
    assert seq % n_steps == 0 and pos_rows % LANES == 0
    rows = pl.BlockSpec((FFN_ROWS, D_MODEL), lambda i: (i, 0))

    cast_specs, cast_shapes = [], []
    for w in later_weights:
        r = _cast_rows(w.shape[0], n_steps)
        steps = w.shape[0] // r
        cast_specs.append(pl.BlockSpec((r, w.shape[1]), lambda i, last=steps - 1: (jnp.minimum(i, last), 0)))
        cast_shapes.append(jax.ShapeDtypeStruct(w.shape, BF16))
    table = pl.BlockSpec((pos_rows, half), lambda i: (i, 0))
    table_t = pl.BlockSpec((half, pos_rows), lambda i: (0, i))

    scratch = [((FFN_ROWS, D_FF), BF16), ((pos_rows, half), F32), ((pos_rows, half), F32)]
    vmem_limit = _vmem_limit(
        resident=[(w.shape, w.dtype) for w in (norm_w, w_gate, w_up, w_down)] + [((1, half), F32)],
        streamed=[((FFN_ROWS, D_MODEL), F32)] * 2 + [((pos_rows, half), F32)] * 4
                 + [(spec.block_shape, dt) for spec in cast_specs for dt in (F32, BF16)],
        scratch=scratch)
    outs = pl.pallas_call(
        functools.partial(_ffn_body, n_cast=len(later_weights), pos_rows=pos_rows),
        grid=(n_steps,),
        in_specs=[rows] + _ffn_weight_specs() + [_resident((1, half))] + cast_specs,
        out_specs=[rows] + cast_specs + [table, table, table_t, table_t],
        out_shape=[jax.ShapeDtypeStruct((t, D_MODEL), F32)] + cast_shapes
                  + [jax.ShapeDtypeStruct((seq, half), F32)] * 2 + [jax.ShapeDtypeStruct((half, seq), F32)] * 2,
        scratch_shapes=[pltpu.VMEM(shape, dtype) for shape, dtype in scratch],
        compiler_params=pltpu.CompilerParams(dimension_semantics=("arbitrary",), vmem_limit_bytes=vmem_limit),
        name="ffn",
    )(x, norm_w, w_gate, w_up, w_down, theta.reshape(1, half), *later_weights)
    n = len(later_weights)
    return outs[0], outs[1:1 + n], outs[1 + n:]


def _log_gamma(logit, shape):
    return jax.nn.log_sigmoid(jnp.full(shape, logit, F32))


def _zero_state_at_sequence_start(st_ref):
    @pl.when(pl.program_id(1) == 0)
    def _():
        st_ref[...] = jnp.zeros_like(st_ref)


def _state_update(st_ref, hd, kt, v, k_scale, block_decay):
    kd = (kt.astype(F32) * k_scale).astype(BF16)
    st_ref[hd] = st_ref[hd] * block_decay + _dot(kd, v)


def _retention_backward_head(dl_ref, hd, q_ref, kt_ref, v_ref, o_ref, st_ref, between=None):
    n_blocks = q_ref.shape[0] // RET_BLOCK
    cols = slice(hd * RET_DIM, (hd + 1) * RET_DIM)
    row_idx = lax.broadcasted_iota(jnp.int32, (RET_BLOCK, RET_DIM), 0).astype(F32)
    lane_idx = lax.broadcasted_iota(jnp.int32, (RET_DIM, RET_BLOCK), 1).astype(F32)
    logit = dl_ref[BACKWARD, hd]
    q_scale = jnp.exp((RET_BLOCK - row_idx) * _log_gamma(logit, (RET_BLOCK, RET_DIM)))
    k_scale = jnp.exp(lane_idx * _log_gamma(logit, (RET_DIM, RET_BLOCK)))
    block_decay = jnp.exp(RET_BLOCK * _log_gamma(logit, (1, RET_DIM)))
    for blk in reversed(range(n_blocks)):
        rows = slice(blk * RET_BLOCK, (blk + 1) * RET_BLOCK)
        o_ref[rows, cols] = (_dot(q_ref[rows, cols], st_ref[hd].astype(BF16)) * q_scale).astype(BF16)
        _state_update(st_ref, hd, kt_ref[cols, rows], v_ref[rows, cols], k_scale, block_decay)
        if blk == n_blocks - 1 and between is not None:
            between()


def _mixer_in_body(dl_ref, x_ref, nw_ref, w_ref, b_ref, lng_ref, lnb_ref, ws_ref, bs_ref, wa_ref,
                   cos_ref, sin_ref, cost_ref, sint_ref,
                   pk_ref, kt_ref, va_ref, vn_ref, a_ref, st_ref):
    q_ref, v_ref, sg_ref, sgb_ref, mixa_ref, cb_ref = _packed_views(pk_ref)
    _zero_state_at_sequence_start(st_ref)
    x = x_ref[...]
    inv_rms = lax.rsqrt(jnp.mean(x * x, axis=-1, keepdims=True) + NORM_EPS)
    h = (x * inv_rms * nw_ref[...]).astype(BF16)
    h_unscaled = (x * nw_ref[...]).astype(BF16)
    half = RET_DIM // 2

    def proj(seg, cols, scale_result=False):
        wcols = slice(seg * D_MODEL + cols.start, seg * D_MODEL + cols.stop)
        if scale_result:
            return _dot(h_unscaled, w_ref[:, wcols]) * inv_rms + b_ref[:, wcols]
        return _dot(h, w_ref[:, wcols]) + b_ref[:, wcols]

    cos, sin = cos_ref[...], sin_ref[...]
    cost, sint = cost_ref[...], sint_ref[...]
    k_scale = RET_DIM ** -0.5

    def retention_head(hd, between=None):
        cols = _col_chunks()[hd]
        t = proj(SEG_Q, cols)
        t1, t2 = t[:, :half], t[:, half:]
        q_ref[:, cols.start:cols.start + half] = (t1 * cos - t2 * sin).astype(BF16)
        q_ref[:, cols.start + half:cols.stop] = (t2 * cos + t1 * sin).astype(BF16)
        tt = proj(SEG_K, cols).T
        t1, t2 = tt[:half], tt[half:]
        kt_ref[cols.start:cols.start + half, :] = ((t1 * cost - t2 * sint) * k_scale).astype(BF16)
        kt_ref[cols.start + half:cols.stop, :] = ((t2 * cost + t1 * sint) * k_scale).astype(BF16)
        v_ref[:, cols] = proj(SEG_VR, cols).astype(BF16)
        sg_ref[:, cols] = _silu(proj(SEG_G, cols)).astype(BF16)
        sgb_ref[:, cols] = jax.nn.sigmoid(proj(SEG_GATE_B, cols)).astype(BF16)
        _retention_backward_head(dl_ref, hd, q_ref, kt_ref, v_ref, cb_ref, st_ref, between)

    for cols in _col_chunks():
        va_ref[:, cols] = _gelu(proj(SEG_V, cols, scale_result=True))
    for hd in range(HEADS_BEFORE_LAYER_NORM):
        retention_head(hd)
    va = va_ref[...]
    mu = jnp.mean(va, axis=-1, keepdims=True)
    vc = va - mu
    var = jnp.mean(vc * vc, axis=-1, keepdims=True)
    vn_ref[...] = (vc * lax.rsqrt(var + NORM_EPS) * lng_ref[...] + lnb_ref[...]).astype(BF16)
    for g, cols in enumerate(_col_chunks()):
        u = _gelu(proj(SEG_U, cols))
        for j in range(MIX_ROWS // (SPATIAL_PAIR * CHUNK)):
            rows = slice(j * SPATIAL_PAIR * CHUNK, (j + 1) * SPATIAL_PAIR * CHUNK)
            s = _dot(ws_ref[g], vn_ref[rows, cols]) + bs_ref[g]
            a_ref[rows, cols] = (u[rows] * s).astype(BF16)

    def gated_branch_a(cols):
        mixa_ref[:, cols] = (jax.nn.sigmoid(proj(SEG_GATE_A, cols))
                             * _dot(a_ref[...], wa_ref[:, cols])).astype(BF16)

    chunks = _col_chunks()
    for hd in range(HEADS_BEFORE_LAYER_NORM, RET_HEADS):
        retention_head(hd, between=functools.partial(gated_branch_a, chunks.pop(0)))
    for cols in chunks:
        gated_branch_a(cols)


def _paired_spatial_weight(w_s):
    eye = jnp.eye(SPATIAL_PAIR, dtype=w_s.dtype)
    paired = jnp.einsum("pq,gcm->gpcqm", eye, w_s)
    return paired.reshape(w_s.shape[0], SPATIAL_PAIR * CHUNK, SPATIAL_PAIR * CHUNK).astype(BF16)


def _mixer_in(decay_logit, x1, norm_w, w_in, b_in, ln_g, ln_b, w_s, b_s, w_a, cos, sin, cos_t, sin_t,
              batch, seq):
    t = x1.shape[0]
    half = RET_DIM // 2
    steps = seq // MIX_ROWS
    tile = lambda b, j: b * steps + steps - 1 - j
    rows = pl.BlockSpec((MIX_ROWS, D_MODEL), lambda b, j: (tile(b, j), 0))
    lanes = pl.BlockSpec((D_MODEL, MIX_ROWS), lambda b, j: (0, tile(b, j)))
    pos = pl.BlockSpec((MIX_ROWS, half), lambda b, j: (steps - 1 - j, 0))
    pos_t = pl.BlockSpec((half, MIX_ROWS), lambda b, j: (0, steps - 1 - j))
    packed = pl.BlockSpec((MIX_ROWS, N_PACK * D_MODEL), lambda b, j: (tile(b, j), 0))
    width = w_in.shape[1]
    out_t = jax.ShapeDtypeStruct((D_MODEL, t), BF16)
    scratch = [((MIX_ROWS, D_MODEL), F32), ((MIX_ROWS, D_MODEL), BF16), ((MIX_ROWS, D_MODEL), BF16),
               ((RET_HEADS, RET_DIM, RET_DIM), F32)]
    vmem_limit = _vmem_limit(
        resident=[(a.shape, a.dtype) for a in (norm_w, w_in, b_in, ln_g, ln_b, w_s, w_a)]
                 + [(b_s.shape[:-1] + (LANES,), b_s.dtype)],
        streamed=[((MIX_ROWS, D_MODEL), F32), ((MIX_ROWS, N_PACK * D_MODEL), BF16), ((D_MODEL, MIX_ROWS), BF16)]
                 + [((MIX_ROWS, half), F32)] * 4,
        scratch=scratch)
    return pl.pallas_call(
        _mixer_in_body,
        grid=(batch, steps),
        in_specs=[pl.BlockSpec(memory_space=pltpu.SMEM),
                  rows, _resident((1, D_MODEL)), _resident((D_MODEL, width)), _resident((1, width)),
                  _resident((1, D_MODEL)), _resident((1, D_MODEL)), _resident(w_s.shape), _resident(b_s.shape),
                  _resident((D_MODEL, D_MODEL)), pos, pos, pos_t, pos_t],
        out_specs=[packed, lanes],
        out_shape=[jax.ShapeDtypeStruct((t, N_PACK * D_MODEL), BF16), out_t],
        scratch_shapes=[pltpu.VMEM(shape, dtype) for shape, dtype in scratch],
        compiler_params=pltpu.CompilerParams(dimension_semantics=("arbitrary", "arbitrary"),
                                             vmem_limit_bytes=vmem_limit),
        name="mixer_in",
    )(decay_logit, x1, norm_w, w_in, b_in, ln_g, ln_b, w_s, b_s, w_a, cos, sin, cos_t, sin_t)


def _mixer_out_body(*refs, final_norm):
    dl_ref, pk_ref, kt_ref, x1_ref, wb_ref, wo_ref, nw_ref, wg_ref, wu_ref, wd_ref = refs[:10]
    fn_ref = refs[10] if final_norm else None
    q_ref, v_ref, sg_ref, sgb_ref, mixa_ref, cb_ref = _packed_views(pk_ref)
    o_ref, st_ref, p_ref, r_ref, mix_ref, act_ref = refs[-6:]
    _zero_state_at_sequence_start(st_ref)
    row_idx = lax.broadcasted_iota(jnp.int32, (RET_BLOCK, RET_DIM), 0).astype(F32)
    lane_idx = lax.broadcasted_iota(jnp.int32, (RET_DIM, RET_BLOCK), 1).astype(F32)
    diff = (lax.broadcasted_iota(jnp.int32, (RET_BLOCK, RET_BLOCK), 0)
            - lax.broadcasted_iota(jnp.int32, (RET_BLOCK, RET_BLOCK), 1)).astype(F32)
    n_blocks = RET_ROWS // RET_BLOCK
    head_cols = lambda hd: slice(hd * RET_DIM, (hd + 1) * RET_DIM)
    block_rows = lambda blk: slice(blk * RET_BLOCK, (blk + 1) * RET_BLOCK)

    for hd in range(RET_HEADS):
        lg_f = _log_gamma(dl_ref[FORWARD, hd], (RET_BLOCK, RET_BLOCK))
        lg_b = _log_gamma(dl_ref[BACKWARD, hd], (RET_BLOCK, RET_BLOCK))
        decay = jnp.exp(jnp.where(diff >= 0, diff * lg_f, -diff * lg_b))
        for blk in range(n_blocks):
            scores = _dot(q_ref[block_rows(blk), head_cols(hd)], kt_ref[head_cols(hd), block_rows(blk)])
            p_ref[hd * n_blocks + blk] = (scores * decay).astype(BF16)

    def merge(rows, cols):
        mix_ref[rows, cols] = (sgb_ref[rows, cols].astype(F32) * _dot(r_ref[rows, :], wb_ref[:, cols])
                               + mixa_ref[rows, cols].astype(F32)).astype(BF16)

    for blk in range(n_blocks):
        rows = block_rows(blk)
        for hd in range(RET_HEADS):
            cols = head_cols(hd)
            logit = dl_ref[FORWARD, hd]
            q_scale = jnp.exp((row_idx + 1.0) * _log_gamma(logit, (RET_BLOCK, RET_DIM)))
            k_scale = jnp.exp((RET_BLOCK - 1.0 - lane_idx) * _log_gamma(logit, (RET_DIM, RET_BLOCK)))
            block_decay = jnp.exp(RET_BLOCK * _log_gamma(logit, (1, RET_DIM)))
            v = v_ref[rows, cols]
            r = (_dot(p_ref[hd * n_blocks + blk], v)
                 + _dot(q_ref[rows, cols], st_ref[hd].astype(BF16)) * q_scale
                 + cb_ref[rows, cols].astype(F32))
            r = r * lax.rsqrt(jnp.mean(r * r, axis=-1, keepdims=True) + NORM_EPS)
            r_ref[rows, cols] = (r * sg_ref[rows, cols].astype(F32)).astype(BF16)
            _state_update(st_ref, hd, kt_ref[cols, rows], v, k_scale, block_decay)
            if blk > 0:
                merge(block_rows(blk - 1), cols)
    for cols in _col_chunks():
        merge(block_rows(n_blocks - 1), cols)
    x2 = x1_ref[...] + _dot(mix_ref[...], wo_ref[...])
    y = _swiglu_half_step(x2, nw_ref, wg_ref, wu_ref, wd_ref, act_ref)
    if final_norm:
        y = _rms(y, fn_ref[...])
    o_ref[...] = y


def _mixer_out(decay_logit, packed, kt, x1, w_b, w_o, norm_w, w_gate, w_up, w_down, final_w, batch, seq):
    steps = seq // RET_ROWS
    rows = pl.BlockSpec((RET_ROWS, D_MODEL), lambda b, j: (b * steps + j, 0))
    packed_rows = pl.BlockSpec((RET_ROWS, N_PACK * D_MODEL), lambda b, j: (b * steps + j, 0))
    lanes = pl.BlockSpec((D_MODEL, RET_ROWS), lambda b, j: (0, b * steps + j))
    in_specs = ([pl.BlockSpec(memory_space=pltpu.SMEM), packed_rows, lanes, rows]
                + [_resident((D_MODEL, D_MODEL)), _resident((D_MODEL, D_MODEL))] + _ffn_weight_specs())
    args = [decay_logit, packed, kt, x1, w_b, w_o, norm_w, w_gate, w_up, w_down]
    if final_w is not None:
        in_specs.append(_resident((1, D_MODEL)))
        args.append(final_w)
    scratch = [((RET_HEADS, RET_DIM, RET_DIM), F32),
               ((RET_HEADS * (RET_ROWS // RET_BLOCK), RET_BLOCK, RET_BLOCK), BF16),
               ((RET_ROWS, D_MODEL), BF16), ((RET_ROWS, D_MODEL), BF16), ((RET_ROWS, D_FF), BF16)]
    vmem_limit = _vmem_limit(
        resident=[(a.shape, a.dtype) for a in args[4:]],
        streamed=[((RET_ROWS, N_PACK * D_MODEL), BF16), ((D_MODEL, RET_ROWS), BF16)]
                 + [((RET_ROWS, D_MODEL), F32)] * 2,
        scratch=scratch)
    return pl.pallas_call(
        functools.partial(_mixer_out_body, final_norm=final_w is not None),
        grid=(batch, steps),
        in_specs=in_specs,
        out_specs=rows,
        out_shape=jax.ShapeDtypeStruct(x1.shape, F32),
        scratch_shapes=[pltpu.VMEM(shape, dtype) for shape, dtype in scratch],
        compiler_params=pltpu.CompilerParams(dimension_semantics=("arbitrary", "arbitrary"),
                                             vmem_limit_bytes=vmem_limit),
        name="mixer_out",
    )(*args)


def kernel(x, ffn1_norm, ffn1_w_gate, ffn1_w_up, ffn1_w_down, mix_norm, w_in, b_in, sgu_norm_g, sgu_norm_b, sgu_w_s, sgu_b_s, ret_decay_logit, w_branch_a, w_branch_b, w_out, ffn2_norm, ffn2_w_gate, ffn2_w_up, ffn2_w_down, final_norm):
    batch, seq, _ = x.shape
    depth = ffn1_norm.shape[0]
    assert depth >= 1 and seq % RET_ROWS == 0 and seq % MIX_ROWS == 0 and (batch * seq) % FFN_ROWS == 0
    tokens = batch * seq
    row = lambda p: p.reshape(1, -1)
    theta = ROPE_BASE ** (-jnp.arange(0, RET_DIM, 2, dtype=F32) / RET_DIM)

    xt = x.reshape(tokens, D_MODEL)
    for l in range(depth):
        last = l == depth - 1
        x1, (w_in_bf, w_a, w_b, w_o, w_gate2, w_up2, w_down2), (cos, sin, cos_t, sin_t) = _ffn(
            xt, row(ffn1_norm[l]), ffn1_w_gate[l], ffn1_w_up[l], ffn1_w_down[l], theta,
            [w_in[l], w_branch_a[l], w_branch_b[l], w_out[l], ffn2_w_gate[l], ffn2_w_up[l], ffn2_w_down[l]], seq)
        packed, kt = _mixer_in(
            ret_decay_logit[l], x1, row(mix_norm[l]), w_in_bf, row(b_in[l]),
            row(sgu_norm_g[l]), row(sgu_norm_b[l]), _paired_spatial_weight(sgu_w_s[l]),
            jnp.tile(sgu_b_s[l], (1, SPATIAL_PAIR))[..., None], w_a, cos, sin, cos_t, sin_t, batch, seq)
        xt = _mixer_out(ret_decay_logit[l], packed, kt, x1, w_b, w_o,
                        row(ffn2_norm[l]), w_gate2, w_up2, w_down2,
                        row(final_norm) if last else None, batch, seq)
    return xt.reshape(batch, seq, D_MODEL)
```

```python
import functools
import math

import jax
import jax.numpy as jnp
import numpy as np
from jax import lax
from jax.experimental import pallas as pl
from jax.experimental.pallas import tpu as pltpu

D_MODEL = 1024
D_FF = 2816
CHUNK = 128
SPATIAL_PAIR = 2
SGU_GROUPS = 4
SGU_GROUP_DIM = D_MODEL // SGU_GROUPS
RET_HEADS = 4
RET_DIM = 256
SEG_U, SEG_V, SEG_Q, SEG_K, SEG_VR, SEG_G, SEG_GATE_A, SEG_GATE_B = range(8)
FORWARD, BACKWARD = 0, 1
PACK_Q, PACK_V, PACK_SG, PACK_SGB, PACK_MIXA, PACK_CB = range(6)
N_PACK = 6
ROPE_BASE = 10000.0
NORM_EPS = 1e-6

MXU_COLS = 256
BF16_SUBLANES = 16
LANES = 128
FFN_ROWS = 512
MIX_ROWS = 512
RET_ROWS = 512
RET_BLOCK = 256
HEADS_BEFORE_LAYER_NORM = 1
MIB = 1024 * 1024
V7X_VMEM_BYTES = 64 * MIB
VMEM_LEFT_TO_COMPILER = 4 * MIB
SPILL_ALLOWANCE = 8 * MIB

BF16 = jnp.bfloat16
F32 = jnp.float32

assert RET_DIM == MXU_COLS and SGU_GROUP_DIM == MXU_COLS
assert 0 < HEADS_BEFORE_LAYER_NORM < RET_HEADS


def _rms(x, g):
    return x * lax.rsqrt(jnp.mean(x * x, axis=-1, keepdims=True) + NORM_EPS) * g


def _gelu(x):
    return 0.5 * x * (1.0 + lax.erf(x * np.float32(np.sqrt(0.5))))


def _silu(x):
    return x * jax.nn.sigmoid(x)


def _dot(a, b):
    return jnp.dot(a, b, preferred_element_type=F32)


def _col_chunks(width=D_MODEL):
    return [slice(c, c + MXU_COLS) for c in range(0, width, MXU_COLS)]


def _packed_views(pk_ref):
    return [pk_ref.at[:, i * D_MODEL:(i + 1) * D_MODEL] for i in range(N_PACK)]


def _vmem_limit(resident, streamed, scratch):
    nbytes = lambda items: sum(math.prod(shape) * jnp.dtype(dtype).itemsize for shape, dtype in items)
    need = nbytes(resident) + 2 * nbytes(streamed) + nbytes(scratch) + SPILL_ALLOWANCE
    return min(need, V7X_VMEM_BYTES - VMEM_LEFT_TO_COMPILER)


def _resident(shape):
    return pl.BlockSpec(shape, lambda *_: (0,) * len(shape), pipeline_mode=pl.Buffered(1))


def _swiglu_half_step(x, nw_ref, wg_ref, wu_ref, wd_ref, act_ref):
    h = (x * nw_ref[...]).astype(BF16)
    inv_rms = lax.rsqrt(jnp.mean(x * x, axis=-1, keepdims=True) + NORM_EPS)
    for cols in _col_chunks(D_FF):
        g = _dot(h, wg_ref[:, cols].astype(BF16)) * inv_rms
        u = _dot(h, wu_ref[:, cols].astype(BF16)) * inv_rms
        act_ref[:, cols] = (_silu(g) * u).astype(BF16)
    act = act_ref[...]
    down = _dot(act, wd_ref[...].astype(BF16))
    return x + 0.5 * down


def _ffn_body(*refs, n_cast, pos_rows):
    n = n_cast
    x_ref, nw_ref, wg_ref, wu_ref, wd_ref, theta_ref = refs[:6]
    srcs = refs[6:6 + n]
    o_ref = refs[6 + n]
    dsts = refs[7 + n:7 + 2 * n]
    cos_ref, sin_ref, cost_ref, sint_ref, act_ref, cos0_ref, sin0_ref = refs[7 + 2 * n:]
    step = pl.program_id(0)
    theta = theta_ref[...]

    @pl.when(step == 0)
    def _():
        ang0 = lax.broadcasted_iota(jnp.int32, (pos_rows, RET_DIM // 2), 0).astype(F32) * theta
        cos0_ref[...] = jnp.cos(ang0)
        sin0_ref[...] = jnp.sin(ang0)

    for src, dst in zip(srcs, dsts):
        dst[...] = src[...].astype(BF16)

    ang1 = (step * pos_rows).astype(F32) * theta
    cos1, sin1 = jnp.cos(ang1), jnp.sin(ang1)
    cos0, sin0 = cos0_ref[...], sin0_ref[...]
    cos = cos1 * cos0 - sin1 * sin0
    sin = sin1 * cos0 + cos1 * sin0
    cos_ref[...] = cos
    sin_ref[...] = sin
    cost_ref[...] = cos.T
    sint_ref[...] = sin.T

    o_ref[...] = _swiglu_half_step(x_ref[...], nw_ref, wg_ref, wu_ref, wd_ref, act_ref)


def _ffn_weight_specs():
    return [_resident((1, D_MODEL)), _resident((D_MODEL, D_FF)), _resident((D_MODEL, D_FF)),
            _resident((D_FF, D_MODEL))]


def _cast_rows(n_rows, n_steps):
    for rows in range(BF16_SUBLANES, n_rows + 1, BF16_SUBLANES):
        if n_rows % rows == 0 and n_rows // rows <= n_steps:
            return rows
    raise ValueError(f"cannot split {n_rows} rows over {n_steps} steps")


def _ffn(x, norm_w, w_gate, w_up, w_down, theta, later_weights, seq):
    t = x.shape[0]
    n_steps = t // FFN_ROWS
    pos_rows = seq // n_steps
    half = RET_DIM // 2
    assert seq % n_steps == 0 and pos_rows % LANES == 0
    rows = pl.BlockSpec((FFN_ROWS, D_MODEL), lambda i: (i, 0))

    cast_specs, cast_shapes = [], []
    for w in later_weights:
        r = _cast_rows(w.shape[0], n_steps)
        steps = w.shape[0] // r
        cast_specs.append(pl.BlockSpec((r, w.shape[1]), lambda i, last=steps - 1: (jnp.minimum(i, last), 0)))
        cast_shapes.append(jax.ShapeDtypeStruct(w.shape, BF16))
    table = pl.BlockSpec((pos_rows, half), lambda i: (i, 0))
    table_t = pl.BlockSpec((half, pos_rows), lambda i: (0, i))

    scratch = [((FFN_ROWS, D_FF), BF16), ((pos_rows, half), F32), ((pos_rows, half), F32)]
    vmem_limit = _vmem_limit(
        resident=[(w.shape, w.dtype) for w in (norm_w, w_gate, w_up, w_down)] + [((1, half), F32)],
        streamed=[((FFN_ROWS, D_MODEL), F32)] * 2 + [((pos_rows, half), F32)] * 4
                 + [(spec.block_shape, dt) for spec in cast_specs for dt in (F32, BF16)],
        scratch=scratch)
    outs = pl.pallas_call(
        functools.partial(_ffn_body, n_cast=len(later_weights), pos_rows=pos_rows),
        grid=(n_steps,),
        in_specs=[rows] + _ffn_weight_specs() + [_resident((1, half))] + cast_specs,
        out_specs=[rows] + cast_specs + [table, table, table_t, table_t],
        out_shape=[jax.ShapeDtypeStruct((t, D_MODEL), F32)] + cast_shapes
                  + [jax.ShapeDtypeStruct((seq, half), F32)] * 2 + [jax.ShapeDtypeStruct((half, seq), F32)] * 2,
        scratch_shapes=[pltpu.VMEM(shape, dtype) for shape, dtype in scratch],
        compiler_params=pltpu.CompilerParams(dimension_semantics=("arbitrary",), vmem_limit_bytes=vmem_limit),
        name="ffn",
    )(x, norm_w, w_gate, w_up, w_down, theta.reshape(1, half), *later_weights)
    n = len(later_weights)
    return outs[0], outs[1:1 + n], outs[1 + n:]


def _log_gamma(logit, shape):
    return jax.nn.log_sigmoid(jnp.full(shape, logit, F32))


def _zero_state_at_sequence_start(st_ref):
    @pl.when(pl.program_id(1) == 0)
    def _():
        st_ref[...] = jnp.zeros_like(st_ref)


def _state_update(st_ref, hd, kt, v, k_scale, block_decay):
    kd = (kt.astype(F32) * k_scale).astype(BF16)
    st_ref[hd] = st_ref[hd] * block_decay + _dot(kd, v)


def _retention_backward_head(dl_ref, hd, q_ref, kt_ref, v_ref, o_ref, st_ref, between=None):
    n_blocks = q_ref.shape[0] // RET_BLOCK
    cols = slice(hd * RET_DIM, (hd + 1) * RET_DIM)
    row_idx = lax.broadcasted_iota(jnp.int32, (RET_BLOCK, RET_DIM), 0).astype(F32)
    lane_idx = lax.broadcasted_iota(jnp.int32, (RET_DIM, RET_BLOCK), 1).astype(F32)
    logit = dl_ref[BACKWARD, hd]
    q_scale = jnp.exp((RET_BLOCK - row_idx) * _log_gamma(logit, (RET_BLOCK, RET_DIM)))
    k_scale = jnp.exp(lane_idx * _log_gamma(logit, (RET_DIM, RET_BLOCK)))
    block_decay = jnp.exp(RET_BLOCK * _log_gamma(logit, (1, RET_DIM)))
    for blk in reversed(range(n_blocks)):
        rows = slice(blk * RET_BLOCK, (blk + 1) * RET_BLOCK)
        o_ref[rows, cols] = (_dot(q_ref[rows, cols], st_ref[hd].astype(BF16)) * q_scale).astype(BF16)
        _state_update(st_ref, hd, kt_ref[cols, rows], v_ref[rows, cols], k_scale, block_decay)
        if blk == n_blocks - 1 and between is not None:
            between()


def _mixer_in_body(dl_ref, x_ref, nw_ref, w_ref, b_ref, lng_ref, lnb_ref, ws_ref, bs_ref, wa_ref,
                   cos_ref, sin_ref, cost_ref, sint_ref,
                   pk_ref, kt_ref, va_ref, vn_ref, a_ref, st_ref):
    q_ref, v_ref, sg_ref, sgb_ref, mixa_ref, cb_ref = _packed_views(pk_ref)
    _zero_state_at_sequence_start(st_ref)
    x = x_ref[...]
    inv_rms = lax.rsqrt(jnp.mean(x * x, axis=-1, keepdims=True) + NORM_EPS)
    h = (x * inv_rms * nw_ref[...]).astype(BF16)
    h_unscaled = (x * nw_ref[...]).astype(BF16)
    half = RET_DIM // 2

    def proj(seg, cols, scale_result=False):
        wcols = slice(seg * D_MODEL + cols.start, seg * D_MODEL + cols.stop)
        if scale_result:
            return _dot(h_unscaled, w_ref[:, wcols]) * inv_rms + b_ref[:, wcols]
        return _dot(h, w_ref[:, wcols]) + b_ref[:, wcols]

    cos, sin = cos_ref[...], sin_ref[...]
    cost, sint = cost_ref[...], sint_ref[...]
    k_scale = RET_DIM ** -0.5

    def value_projection(hd):
        cols = _col_chunks()[hd]
        v_ref[:, cols] = proj(SEG_VR, cols).astype(BF16)

    def retention_head(hd, between=None, own_value=True):
        cols = _col_chunks()[hd]
        t = proj(SEG_Q, cols)
        t1, t2 = t[:, :half], t[:, half:]
        q_ref[:, cols.start:cols.start + half] = (t1 * cos - t2 * sin).astype(BF16)
        q_ref[:, cols.start + half:cols.stop] = (t2 * cos + t1 * sin).astype(BF16)
        tt = proj(SEG_K, cols).T
        t1, t2 = tt[:half], tt[half:]
        kt_ref[cols.start:cols.start + half, :] = ((t1 * cost - t2 * sint) * k_scale).astype(BF16)
        kt_ref[cols.start + half:cols.stop, :] = ((t2 * cost + t1 * sint) * k_scale).astype(BF16)
        if own_value:
            value_projection(hd)
        sg_ref[:, cols] = _silu(proj(SEG_G, cols)).astype(BF16)
        sgb_ref[:, cols] = jax.nn.sigmoid(proj(SEG_GATE_B, cols)).astype(BF16)
        _retention_backward_head(dl_ref, hd, q_ref, kt_ref, v_ref, cb_ref, st_ref, between)

    for cols in _col_chunks():
        va_ref[:, cols] = _gelu(proj(SEG_V, cols, scale_result=True))
    for hd in range(HEADS_BEFORE_LAYER_NORM):
        retention_head(hd, between=functools.partial(value_projection, hd + 1), own_value=hd == 0)
    va = va_ref[...]
    mu = jnp.mean(va, axis=-1, keepdims=True)
    vc = va - mu
    var = jnp.mean(vc * vc, axis=-1, keepdims=True)
    vn_ref[...] = (vc * lax.rsqrt(var + NORM_EPS) * lng_ref[...] + lnb_ref[...]).astype(BF16)
    for g, cols in enumerate(_col_chunks()):
        u = _gelu(proj(SEG_U, cols))
        for j in range(MIX_ROWS // (SPATIAL_PAIR * CHUNK)):
            rows = slice(j * SPATIAL_PAIR * CHUNK, (j + 1) * SPATIAL_PAIR * CHUNK)
            s = _dot(ws_ref[g], vn_ref[rows, cols]) + bs_ref[g]
            a_ref[rows, cols] = (u[rows] * s).astype(BF16)

    def gated_branch_a(cols):
        mixa_ref[:, cols] = (jax.nn.sigmoid(proj(SEG_GATE_A, cols))
                             * _dot(a_ref[...], wa_ref[:, cols])).astype(BF16)

    chunks = _col_chunks()
    for hd in range(HEADS_BEFORE_LAYER_NORM, RET_HEADS):
        retention_head(hd, between=functools.partial(gated_branch_a, chunks.pop(0)),
                       own_value=hd != HEADS_BEFORE_LAYER_NORM)
    for cols in chunks:
        gated_branch_a(cols)


def _paired_spatial_weight(w_s):
    eye = jnp.eye(SPATIAL_PAIR, dtype=w_s.dtype)
    paired = jnp.einsum("pq,gcm->gpcqm", eye, w_s)
    return paired.reshape(w_s.shape[0], SPATIAL_PAIR * CHUNK, SPATIAL_PAIR * CHUNK).astype(BF16)


def _mixer_in(decay_logit, x1, norm_w, w_in, b_in, ln_g, ln_b, w_s, b_s, w_a, cos, sin, cos_t, sin_t,
              batch, seq):
    t = x1.shape[0]
    half = RET_DIM // 2
    steps = seq // MIX_ROWS
    tile = lambda b, j: b * steps + steps - 1 - j
    rows = pl.BlockSpec((MIX_ROWS, D_MODEL), lambda b, j: (tile(b, j), 0))
    lanes = pl.BlockSpec((D_MODEL, MIX_ROWS), lambda b, j: (0, tile(b, j)))
    pos = pl.BlockSpec((MIX_ROWS, half), lambda b, j: (steps - 1 - j, 0))
    pos_t = pl.BlockSpec((half, MIX_ROWS), lambda b, j: (0, steps - 1 - j))
    packed = pl.BlockSpec((MIX_ROWS, N_PACK * D_MODEL), lambda b, j: (tile(b, j), 0))
    width = w_in.shape[1]
    out_t = jax.ShapeDtypeStruct((D_MODEL, t), BF16)
    scratch = [((MIX_ROWS, D_MODEL), F32), ((MIX_ROWS, D_MODEL), BF16), ((MIX_ROWS, D_MODEL), BF16),
               ((RET_HEADS, RET_DIM, RET_DIM), F32)]
    vmem_limit = _vmem_limit(
        resident=[(a.shape, a.dtype) for a in (norm_w, w_in, b_in, ln_g, ln_b, w_s, w_a)]
                 + [(b_s.shape[:-1] + (LANES,), b_s.dtype)],
        streamed=[((MIX_ROWS, D_MODEL), F32), ((MIX_ROWS, N_PACK * D_MODEL), BF16), ((D_MODEL, MIX_ROWS), BF16)]
                 + [((MIX_ROWS, half), F32)] * 4,
        scratch=scratch)
    return pl.pallas_call(
        _mixer_in_body,
        grid=(batch, steps),
        in_specs=[pl.BlockSpec(memory_space=pltpu.SMEM),
                  rows, _resident((1, D_MODEL)), _resident((D_MODEL, width)), _resident((1, width)),
                  _resident((1, D_MODEL)), _resident((1, D_MODEL)), _resident(w_s.shape), _resident(b_s.shape),
                  _resident((D_MODEL, D_MODEL)), pos, pos, pos_t, pos_t],
        out_specs=[packed, lanes],
        out_shape=[jax.ShapeDtypeStruct((t, N_PACK * D_MODEL), BF16), out_t],
        scratch_shapes=[pltpu.VMEM(shape, dtype) for shape, dtype in scratch],
        compiler_params=pltpu.CompilerParams(dimension_semantics=("arbitrary", "arbitrary"),
                                             vmem_limit_bytes=vmem_limit),
        name="mixer_in",
    )(decay_logit, x1, norm_w, w_in, b_in, ln_g, ln_b, w_s, b_s, w_a, cos, sin, cos_t, sin_t)


def _mixer_out_body(*refs, final_norm):
    dl_ref, pk_ref, kt_ref, x1_ref, wb_ref, wo_ref, nw_ref, wg_ref, wu_ref, wd_ref = refs[:10]
    fn_ref = refs[10] if final_norm else None
    q_ref, v_ref, sg_ref, sgb_ref, mixa_ref, cb_ref = _packed_views(pk_ref)
    o_ref, st_ref, p_ref, r_ref, mix_ref, act_ref = refs[-6:]
    _zero_state_at_sequence_start(st_ref)
    row_idx = lax.broadcasted_iota(jnp.int32, (RET_BLOCK, RET_DIM), 0).astype(F32)
    lane_idx = lax.broadcasted_iota(jnp.int32, (RET_DIM, RET_BLOCK), 1).astype(F32)
    diff = (lax.broadcasted_iota(jnp.int32, (RET_BLOCK, RET_BLOCK), 0)
            - lax.broadcasted_iota(jnp.int32, (RET_BLOCK, RET_BLOCK), 1)).astype(F32)
    n_blocks = RET_ROWS // RET_BLOCK
    head_cols = lambda hd: slice(hd * RET_DIM, (hd + 1) * RET_DIM)
    block_rows = lambda blk: slice(blk * RET_BLOCK, (blk + 1) * RET_BLOCK)

    for hd in range(RET_HEADS):
        lg_f = _log_gamma(dl_ref[FORWARD, hd], (RET_BLOCK, RET_BLOCK))
        lg_b = _log_gamma(dl_ref[BACKWARD, hd], (RET_BLOCK, RET_BLOCK))
        decay = jnp.exp(jnp.where(diff >= 0, diff * lg_f, -diff * lg_b))
        for blk in range(n_blocks):
            scores = _dot(q_ref[block_rows(blk), head_cols(hd)], kt_ref[head_cols(hd), block_rows(blk)])
            p_ref[hd * n_blocks + blk] = (scores * decay).astype(BF16)

    def merge(rows, cols):
        mix_ref[rows, cols] = (sgb_ref[rows, cols].astype(F32) * _dot(r_ref[rows, :], wb_ref[:, cols])
                               + mixa_ref[rows, cols].astype(F32)).astype(BF16)

    for blk in range(n_blocks):
        rows = block_rows(blk)
        for hd in range(RET_HEADS):
            cols = head_cols(hd)
            logit = dl_ref[FORWARD, hd]
            q_scale = jnp.exp((row_idx + 1.0) * _log_gamma(logit, (RET_BLOCK, RET_DIM)))
            k_scale = jnp.exp((RET_BLOCK - 1.0 - lane_idx) * _log_gamma(logit, (RET_DIM, RET_BLOCK)))
            block_decay = jnp.exp(RET_BLOCK * _log_gamma(logit, (1, RET_DIM)))
            v = v_ref[rows, cols]
            r = (_dot(p_ref[hd * n_blocks + blk], v)
                 + _dot(q_ref[rows, cols], st_ref[hd].astype(BF16)) * q_scale
                 + cb_ref[rows, cols].astype(F32))
            r = r * lax.rsqrt(jnp.mean(r * r, axis=-1, keepdims=True) + NORM_EPS)
            r_ref[rows, cols] = (r * sg_ref[rows, cols].astype(F32)).astype(BF16)
            _state_update(st_ref, hd, kt_ref[cols, rows], v, k_scale, block_decay)
            if blk > 0:
                merge(block_rows(blk - 1), cols)
    for cols in _col_chunks():
        merge(block_rows(n_blocks - 1), cols)
    x2 = x1_ref[...] + _dot(mix_ref[...], wo_ref[...])
    y = _swiglu_half_step(x2, nw_ref, wg_ref, wu_ref, wd_ref, act_ref)
    if final_norm:
        y = _rms(y, fn_ref[...])
    o_ref[...] = y


def _mixer_out(decay_logit, packed, kt, x1, w_b, w_o, norm_w, w_gate, w_up, w_down, final_w, batch, seq):
    steps = seq // RET_ROWS
    rows = pl.BlockSpec((RET_ROWS, D_MODEL), lambda b, j: (b * steps + j, 0))
    packed_rows = pl.BlockSpec((RET_ROWS, N_PACK * D_MODEL), lambda b, j: (b * steps + j, 0))
    lanes = pl.BlockSpec((D_MODEL, RET_ROWS), lambda b, j: (0, b * steps + j))
    in_specs = ([pl.BlockSpec(memory_space=pltpu.SMEM), packed_rows, lanes, rows]
                + [_resident((D_MODEL, D_MODEL)), _resident((D_MODEL, D_MODEL))] + _ffn_weight_specs())
    args = [decay_logit, packed, kt, x1, w_b, w_o, norm_w, w_gate, w_up, w_down]
    if final_w is not None:
        in_specs.append(_resident((1, D_MODEL)))
        args.append(final_w)
    scratch = [((RET_HEADS, RET_DIM, RET_DIM), F32),
               ((RET_HEADS * (RET_ROWS // RET_BLOCK), RET_BLOCK, RET_BLOCK), BF16),
               ((RET_ROWS, D_MODEL), BF16), ((RET_ROWS, D_MODEL), BF16), ((RET_ROWS, D_FF), BF16)]
    vmem_limit = _vmem_limit(
        resident=[(a.shape, a.dtype) for a in args[4:]],
        streamed=[((RET_ROWS, N_PACK * D_MODEL), BF16), ((D_MODEL, RET_ROWS), BF16)]
                 + [((RET_ROWS, D_MODEL), F32)] * 2,
        scratch=scratch)
    return pl.pallas_call(
        functools.partial(_mixer_out_body, final_norm=final_w is not None),
        grid=(batch, steps),
        in_specs=in_specs,
        out_specs=rows,
        out_shape=jax.ShapeDtypeStruct(x1.shape, F32),
        scratch_shapes=[pltpu.VMEM(shape, dtype) for shape, dtype in scratch],
        compiler_params=pltpu.CompilerParams(dimension_semantics=("arbitrary", "arbitrary"),
                                             vmem_limit_bytes=vmem_limit),
        name="mixer_out",
    )(*args)


def kernel(x, ffn1_norm, ffn1_w_gate, ffn1_w_up, ffn1_w_down, mix_norm, w_in, b_in, sgu_norm_g, sgu_norm_b, sgu_w_s, sgu_b_s, ret_decay_logit, w_branch_a, w_branch_b, w_out, ffn2_norm, ffn2_w_gate, ffn2_w_up, ffn2_w_down, final_norm):
    batch, seq, _ = x.shape
    depth = ffn1_norm.shape[0]
    assert depth >= 1 and seq % RET_ROWS == 0 and seq % MIX_ROWS == 0 and (batch * seq) % FFN_ROWS == 0
    tokens = batch * seq
    row = lambda p: p.reshape(1, -1)
    theta = ROPE_BASE ** (-jnp.arange(0, RET_DIM, 2, dtype=F32) / RET_DIM)

    xt = x.reshape(tokens, D_MODEL)
    for l in range(depth):
        last = l == depth - 1
        x1, (w_in_bf, w_a, w_b, w_o, w_gate2, w_up2, w_down2), (cos, sin, cos_t, sin_t) = _ffn(
            xt, row(ffn1_norm[l]), ffn1_w_gate[l], ffn1_w_up[l], ffn1_w_down[l], theta,
            [w_in[l], w_branch_a[l], w_branch_b[l], w_out[l], ffn2_w_gate[l], ffn2_w_up[l], ffn2_w_down[l]], seq)
        packed, kt = _mixer_in(
            ret_decay_logit[l], x1, row(mix_norm[l]), w_in_bf, row(b_in[l]),
            row(sgu_norm_g[l]), row(sgu_norm_b[l]), _paired_spatial_weight(sgu_w_s[l]),
            jnp.tile(sgu_b_s[l], (1, SPATIAL_PAIR))[..., None], w_a, cos, sin, cos_t, sin_t, batch, seq)
        xt = _mixer_out(ret_decay_logit[l], packed, kt, x1, w_b, w_o,
                        row(ffn2_norm[l]), w_gate2, w_up2, w_down2,
                        row(final_norm) if last else None, batch, seq)
    return xt.reshape(batch, seq, D_MODEL)
```

```python
import functools
import math

import jax
import jax.numpy as jnp
import numpy as np
from jax import lax
from jax.experimental import pallas as pl
from jax.experimental.pallas import tpu as pltpu

D_MODEL = 1024
D_FF = 2816
CHUNK = 128
SPATIAL_PAIR = 2
SGU_GROUPS = 4
SGU_GROUP_DIM = D_MODEL // SGU_GROUPS
RET_HEADS = 4
RET_DIM = 256
SEG_U, SEG_V, SEG_Q, SEG_K, SEG_VR, SEG_G, SEG_GATE_A, SEG_GATE_B = range(8)
FORWARD, BACKWARD = 0, 1
PACK_Q, PACK_V, PACK_SG, PACK_SGB, PACK_MIXA, PACK_CB = range(6)
N_PACK = 6
ROPE_BASE = 10000.0
NORM_EPS = 1e-6

MXU_COLS = 256
BF16_SUBLANES = 16
LANES = 128
FFN_ROWS = 512
MIX_ROWS = 512
RET_ROWS = 512
RET_BLOCK = 256
HEADS_BEFORE_LAYER_NORM = 1
MIB = 1024 * 1024
V7X_VMEM_BYTES = 64 * MIB
VMEM_LEFT_TO_COMPILER = 4 * MIB
SPILL_ALLOWANCE = 8 * MIB

BF16 = jnp.bfloat16
F32 = jnp.float32

assert RET_DIM == MXU_COLS and SGU_GROUP_DIM == MXU_COLS
assert 0 < HEADS_BEFORE_LAYER_NORM < RET_HEADS


def _rms(x, g):
    return x * lax.rsqrt(jnp.mean(x * x, axis=-1, keepdims=True) + NORM_EPS) * g


def _gelu(x):
    return 0.5 * x * (1.0 + lax.erf(x * np.float32(np.sqrt(0.5))))


def _silu(x):
    return x * jax.nn.sigmoid(x)


def _dot(a, b):
    return jnp.dot(a, b, preferred_element_type=F32)


def _col_chunks(width=D_MODEL):
    return [slice(c, c + MXU_COLS) for c in range(0, width, MXU_COLS)]


def _packed_views(pk_ref):
    return [pk_ref.at[:, i * D_MODEL:(i + 1) * D_MODEL] for i in range(N_PACK)]


def _vmem_limit(resident, streamed, scratch):
    nbytes = lambda items: sum(math.prod(shape) * jnp.dtype(dtype).itemsize for shape, dtype in items)
    need = nbytes(resident) + 2 * nbytes(streamed) + nbytes(scratch) + SPILL_ALLOWANCE
    return min(need, V7X_VMEM_BYTES - VMEM_LEFT_TO_COMPILER)


def _resident(shape):
    return pl.BlockSpec(shape, lambda *_: (0,) * len(shape), pipeline_mode=pl.Buffered(1))


def _swiglu_half_step(x, nw_ref, wg_ref, wu_ref, wd_ref, act_ref):
    h = (x * nw_ref[...]).astype(BF16)
    inv_rms = lax.rsqrt(jnp.mean(x * x, axis=-1, keepdims=True) + NORM_EPS)
    for cols in _col_chunks(D_FF):
        g = _dot(h, wg_ref[:, cols].astype(BF16)) * inv_rms
        u = _dot(h, wu_ref[:, cols].astype(BF16)) * inv_rms
        act_ref[:, cols] = (_silu(g) * u).astype(BF16)
    act = act_ref[...]
    down = _dot(act, wd_ref[...].astype(BF16))
    return x + 0.5 * down


def _ffn_body(*refs, n_cast, pos_rows):
    n = n_cast
    x_ref, nw_ref, wg_ref, wu_ref, wd_ref, theta_ref = refs[:6]
    srcs = refs[6:6 + n]
    o_ref = refs[6 + n]
    dsts = refs[7 + n:7 + 2 * n]
    cos_ref, sin_ref, cost_ref, sint_ref, act_ref, cos0_ref, sin0_ref = refs[7 + 2 * n:]
    step = pl.program_id(0)
    theta = theta_ref[...]

    @pl.when(step == 0)
    def _():
        ang0 = lax.broadcasted_iota(jnp.int32, (pos_rows, RET_DIM // 2), 0).astype(F32) * theta
        cos0_ref[...] = jnp.cos(ang0)
        sin0_ref[...] = jnp.sin(ang0)

    for src, dst in zip(srcs, dsts):
        dst[...] = src[...].astype(BF16)

    ang1 = (step * pos_rows).astype(F32) * theta
    cos1, sin1 = jnp.cos(ang1), jnp.sin(ang1)
    cos0, sin0 = cos0_ref[...], sin0_ref[...]
    cos = cos1 * cos0 - sin1 * sin0
    sin = sin1 * cos0 + cos1 * sin0
    cos_ref[...] = cos
    sin_ref[...] = sin
    cost_ref[...] = cos.T
    sint_ref[...] = sin.T

    o_ref[...] = _swiglu_half_step(x_ref[...], nw_ref, wg_ref, wu_ref, wd_ref, act_ref)


def _ffn_weight_specs():
    return [_resident((1, D_MODEL)), _resident((D_MODEL, D_FF)), _resident((D_MODEL, D_FF)),
            _resident((D_FF, D_MODEL))]


def _cast_rows(n_rows, n_steps):
    for rows in range(BF16_SUBLANES, n_rows + 1, BF16_SUBLANES):
        if n_rows % rows == 0 and n_rows // rows <= n_steps:
            return rows
    raise ValueError(f"cannot split {n_rows} rows over {n_steps} steps")


def _ffn(x, norm_w, w_gate, w_up, w_down, theta, later_weights, seq):
    t = x.shape[0]
    n_steps = t // FFN_ROWS
    pos_rows = seq // n_steps
    half = RET_DIM // 2
    assert seq % n_steps == 0 and pos_rows % LANES == 0
    rows = pl.BlockSpec((FFN_ROWS, D_MODEL), lambda i: (i, 0))

    cast_specs, cast_shapes = [], []
    for w in later_weights:
        r = _cast_rows(w.shape[0], n_steps)
        steps = w.shape[0] // r
        cast_specs.append(pl.BlockSpec((r, w.shape[1]), lambda i, last=steps - 1: (jnp.minimum(i, last), 0)))
        cast_shapes.append(jax.ShapeDtypeStruct(w.shape, BF16))
    table = pl.BlockSpec((pos_rows, half), lambda i: (i, 0))
    table_t = pl.BlockSpec((half, pos_rows), lambda i: (0, i))

    scratch = [((FFN_ROWS, D_FF), BF16), ((pos_rows, half), F32), ((pos_rows, half), F32)]
    vmem_limit = _vmem_limit(
        resident=[(w.shape, w.dtype) for w in (norm_w, w_gate, w_up, w_down)] + [((1, half), F32)],
        streamed=[((FFN_ROWS, D_MODEL), F32)] * 2 + [((pos_rows, half), F32)] * 4
                 + [(spec.block_shape, dt) for spec in cast_specs for dt in (F32, BF16)],
        scratch=scratch)
    outs = pl.pallas_call(
        functools.partial(_ffn_body, n_cast=len(later_weights), pos_rows=pos_rows),
        grid=(n_steps,),
        in_specs=[rows] + _ffn_weight_specs() + [_resident((1, half))] + cast_specs,
        out_specs=[rows] + cast_specs + [table, table, table_t, table_t],
        out_shape=[jax.ShapeDtypeStruct((t, D_MODEL), F32)] + cast_shapes
                  + [jax.ShapeDtypeStruct((seq, half), F32)] * 2 + [jax.ShapeDtypeStruct((half, seq), F32)] * 2,
        scratch_shapes=[pltpu.VMEM(shape, dtype) for shape, dtype in scratch],
        compiler_params=pltpu.CompilerParams(dimension_semantics=("arbitrary",), vmem_limit_bytes=vmem_limit),
        name="ffn",
    )(x, norm_w, w_gate, w_up, w_down, theta.reshape(1, half), *later_weights)
    n = len(later_weights)
    return outs[0], outs[1:1 + n], outs[1 + n:]


def _log_gamma(logit, shape):
    return jax.nn.log_sigmoid(jnp.full(shape, logit, F32))


def _zero_state_at_sequence_start(st_ref):
    @pl.when(pl.program_id(1) == 0)
    def _():
        st_ref[...] = jnp.zeros_like(st_ref)


def _state_update(st_ref, hd, kt, v, k_scale, block_decay):
    kd = (kt.astype(F32) * k_scale).astype(BF16)
    st_ref[hd] = st_ref[hd] * block_decay + _dot(kd, v)


def _retention_backward_head(dl_ref, hd, q_ref, kt_ref, v_ref, o_ref, st_ref, between=None):
    n_blocks = q_ref.shape[0] // RET_BLOCK
    cols = slice(hd * RET_DIM, (hd + 1) * RET_DIM)
    row_idx = lax.broadcasted_iota(jnp.int32, (RET_BLOCK, RET_DIM), 0).astype(F32)
    lane_idx = lax.broadcasted_iota(jnp.int32, (RET_DIM, RET_BLOCK), 1).astype(F32)
    logit = dl_ref[BACKWARD, hd]
    q_scale = jnp.exp((RET_BLOCK - row_idx) * _log_gamma(logit, (RET_BLOCK, RET_DIM)))
    k_scale = jnp.exp(lane_idx * _log_gamma(logit, (RET_DIM, RET_BLOCK)))
    block_decay = jnp.exp(RET_BLOCK * _log_gamma(logit, (1, RET_DIM)))
    for blk in reversed(range(n_blocks)):
        rows = slice(blk * RET_BLOCK, (blk + 1) * RET_BLOCK)
        o_ref[rows, cols] = (_dot(q_ref[rows, cols], st_ref[hd].astype(BF16)) * q_scale).astype(BF16)
        _state_update(st_ref, hd, kt_ref[cols, rows], v_ref[rows, cols], k_scale, block_decay)
        if blk == n_blocks - 1 and between is not None:
            between()


def _mixer_in_body(dl_ref, x_ref, nw_ref, w_ref, b_ref, lng_ref, lnb_ref, ws_ref, bs_ref, wa_ref,
                   cos_ref, sin_ref, cost_ref, sint_ref,
                   pk_ref, kt_ref, va_ref, vn_ref, a_ref, st_ref):
    q_ref, v_ref, sg_ref, sgb_ref, mixa_ref, cb_ref = _packed_views(pk_ref)
    _zero_state_at_sequence_start(st_ref)
    x = x_ref[...]
    inv_rms = lax.rsqrt(jnp.mean(x * x, axis=-1, keepdims=True) + NORM_EPS)
    h = (x * inv_rms * nw_ref[...]).astype(BF16)
    h_unscaled = (x * nw_ref[...]).astype(BF16)
    half = RET_DIM // 2

    def proj(seg, cols, scale_result=False):
        wcols = slice(seg * D_MODEL + cols.start, seg * D_MODEL + cols.stop)
        if scale_result:
            return _dot(h_unscaled, w_ref[:, wcols]) * inv_rms + b_ref[:, wcols]
        return _dot(h, w_ref[:, wcols]) + b_ref[:, wcols]

    cos, sin = cos_ref[...], sin_ref[...]
    cost, sint = cost_ref[...], sint_ref[...]
    k_scale = RET_DIM ** -0.5

    def value_projection(hd):
        cols = _col_chunks()[hd]
        v_ref[:, cols] = proj(SEG_VR, cols).astype(BF16)

    def retention_head(hd, between=None, own_value=True):
        cols = _col_chunks()[hd]
        t = proj(SEG_Q, cols)
        t1, t2 = t[:, :half], t[:, half:]
        q_ref[:, cols.start:cols.start + half] = (t1 * cos - t2 * sin).astype(BF16)
        q_ref[:, cols.start + half:cols.stop] = (t2 * cos + t1 * sin).astype(BF16)
        tt = proj(SEG_K, cols).T
        t1, t2 = tt[:half], tt[half:]
        kt_ref[cols.start:cols.start + half, :] = ((t1 * cost - t2 * sint) * k_scale).astype(BF16)
        kt_ref[cols.start + half:cols.stop, :] = ((t2 * cost + t1 * sint) * k_scale).astype(BF16)
        if own_value:
            value_projection(hd)
        sg_ref[:, cols] = _silu(proj(SEG_G, cols)).astype(BF16)
        sgb_ref[:, cols] = jax.nn.sigmoid(proj(SEG_GATE_B, cols)).astype(BF16)
        _retention_backward_head(dl_ref, hd, q_ref, kt_ref, v_ref, cb_ref, st_ref, between)

    for cols in _col_chunks():
        va_ref[:, cols] = _gelu(proj(SEG_V, cols, scale_result=True))
    for hd in range(HEADS_BEFORE_LAYER_NORM):
        retention_head(hd, between=functools.partial(value_projection, hd + 1), own_value=hd == 0)
    va = va_ref[...]
    mu = jnp.mean(va, axis=-1, keepdims=True)
    vc = va - mu
    var = jnp.mean(vc * vc, axis=-1, keepdims=True)
    vn_ref[...] = (vc * lax.rsqrt(var + NORM_EPS) * lng_ref[...] + lnb_ref[...]).astype(BF16)
    for g, cols in enumerate(_col_chunks()):
        u = _gelu(proj(SEG_U, cols))
        for j in range(MIX_ROWS // (SPATIAL_PAIR * CHUNK)):
            rows = slice(j * SPATIAL_PAIR * CHUNK, (j + 1) * SPATIAL_PAIR * CHUNK)
            s = _dot(ws_ref[g], vn_ref[rows, cols]) + bs_ref[g]
            a_ref[rows, cols] = (u[rows] * s).astype(BF16)

    def gated_branch_a(cols):
        mixa_ref[:, cols] = (jax.nn.sigmoid(proj(SEG_GATE_A, cols))
                             * _dot(a_ref[...], wa_ref[:, cols])).astype(BF16)

    chunks = _col_chunks()
    for hd in range(HEADS_BEFORE_LAYER_NORM, RET_HEADS):
        retention_head(hd, between=functools.partial(gated_branch_a, chunks.pop(0)),
                       own_value=hd != HEADS_BEFORE_LAYER_NORM)
    for cols in chunks:
        gated_branch_a(cols)


def _paired_spatial_weight(w_s):
    eye = jnp.eye(SPATIAL_PAIR, dtype=w_s.dtype)
    paired = jnp.einsum("pq,gcm->gpcqm", eye, w_s)
    return paired.reshape(w_s.shape[0], SPATIAL_PAIR * CHUNK, SPATIAL_PAIR * CHUNK).astype(BF16)


def _mixer_in(decay_logit, x1, norm_w, w_in, b_in, ln_g, ln_b, w_s, b_s, w_a, cos, sin, cos_t, sin_t,
              batch, seq):
    t = x1.shape[0]
    half = RET_DIM // 2
    steps = seq // MIX_ROWS
    tile = lambda b, j: b * steps + steps - 1 - j
    rows = pl.BlockSpec((MIX_ROWS, D_MODEL), lambda b, j: (tile(b, j), 0))
    lanes = pl.BlockSpec((D_MODEL, MIX_ROWS), lambda b, j: (0, tile(b, j)))
    pos = pl.BlockSpec((MIX_ROWS, half), lambda b, j: (steps - 1 - j, 0))
    pos_t = pl.BlockSpec((half, MIX_ROWS), lambda b, j: (0, steps - 1 - j))
    packed = pl.BlockSpec((MIX_ROWS, N_PACK * D_MODEL), lambda b, j: (tile(b, j), 0))
    width = w_in.shape[1]
    out_t = jax.ShapeDtypeStruct((D_MODEL, t), BF16)
    scratch = [((MIX_ROWS, D_MODEL), F32), ((MIX_ROWS, D_MODEL), BF16), ((MIX_ROWS, D_MODEL), BF16),
               ((RET_HEADS, RET_DIM, RET_DIM), F32)]
    vmem_limit = _vmem_limit(
        resident=[(a.shape, a.dtype) for a in (norm_w, w_in, b_in, ln_g, ln_b, w_s, w_a)]
                 + [(b_s.shape[:-1] + (LANES,), b_s.dtype)],
        streamed=[((MIX_ROWS, D_MODEL), F32), ((MIX_ROWS, N_PACK * D_MODEL), BF16), ((D_MODEL, MIX_ROWS), BF16)]
                 + [((MIX_ROWS, half), F32)] * 4,
        scratch=scratch)
    return pl.pallas_call(
        _mixer_in_body,
        grid=(batch, steps),
        in_specs=[pl.BlockSpec(memory_space=pltpu.SMEM),
                  rows, _resident((1, D_MODEL)), _resident((D_MODEL, width)), _resident((1, width)),
                  _resident((1, D_MODEL)), _resident((1, D_MODEL)), _resident(w_s.shape), _resident(b_s.shape),
                  _resident((D_MODEL, D_MODEL)), pos, pos, pos_t, pos_t],
        out_specs=[packed, lanes],
        out_shape=[jax.ShapeDtypeStruct((t, N_PACK * D_MODEL), BF16), out_t],
        scratch_shapes=[pltpu.VMEM(shape, dtype) for shape, dtype in scratch],
        compiler_params=pltpu.CompilerParams(dimension_semantics=("arbitrary", "arbitrary"),
                                             vmem_limit_bytes=vmem_limit),
        name="mixer_in",
    )(decay_logit, x1, norm_w, w_in, b_in, ln_g, ln_b, w_s, b_s, w_a, cos, sin, cos_t, sin_t)


def _mixer_out_body(*refs, final_norm):
    dl_ref, pk_ref, kt_ref, x1_ref, wb_ref, wo_ref, nw_ref, wg_ref, wu_ref, wd_ref = refs[:10]
    fn_ref = refs[10] if final_norm else None
    q_ref, v_ref, sg_ref, sgb_ref, mixa_ref, cb_ref = _packed_views(pk_ref)
    o_ref, st_ref, p_ref, r_ref, mix_ref, act_ref = refs[-6:]
    _zero_state_at_sequence_start(st_ref)
    row_idx = lax.broadcasted_iota(jnp.int32, (RET_BLOCK, RET_DIM), 0).astype(F32)
    lane_idx = lax.broadcasted_iota(jnp.int32, (RET_DIM, RET_BLOCK), 1).astype(F32)
    diff = (lax.broadcasted_iota(jnp.int32, (RET_BLOCK, RET_BLOCK), 0)
            - lax.broadcasted_iota(jnp.int32, (RET_BLOCK, RET_BLOCK), 1)).astype(F32)
    n_blocks = RET_ROWS // RET_BLOCK
    head_cols = lambda hd: slice(hd * RET_DIM, (hd + 1) * RET_DIM)
    block_rows = lambda blk: slice(blk * RET_BLOCK, (blk + 1) * RET_BLOCK)

    for hd in range(RET_HEADS):
        lg_f = _log_gamma(dl_ref[FORWARD, hd], (RET_BLOCK, RET_BLOCK))
        lg_b = _log_gamma(dl_ref[BACKWARD, hd], (RET_BLOCK, RET_BLOCK))
        decay = jnp.exp(jnp.where(diff >= 0, diff * lg_f, -diff * lg_b))
        for blk in range(n_blocks):
            scores = _dot(q_ref[block_rows(blk), head_cols(hd)], kt_ref[head_cols(hd), block_rows(blk)])
            p_ref[hd * n_blocks + blk] = (scores * decay).astype(BF16)

    def merge(rows, cols):
        mix_ref[rows, cols] = (sgb_ref[rows, cols].astype(F32) * _dot(r_ref[rows, :], wb_ref[:, cols])
                               + mixa_ref[rows, cols].astype(F32)).astype(BF16)

    for blk in range(n_blocks):
        rows = block_rows(blk)
        for hd in range(RET_HEADS):
            cols = head_cols(hd)
            logit = dl_ref[FORWARD, hd]
            q_scale = jnp.exp((row_idx + 1.0) * _log_gamma(logit, (RET_BLOCK, RET_DIM)))
            k_scale = jnp.exp((RET_BLOCK - 1.0 - lane_idx) * _log_gamma(logit, (RET_DIM, RET_BLOCK)))
            block_decay = jnp.exp(RET_BLOCK * _log_gamma(logit, (1, RET_DIM)))
            v = v_ref[rows, cols]
            r = (_dot(p_ref[hd * n_blocks + blk], v)
                 + _dot(q_ref[rows, cols], st_ref[hd].astype(BF16)) * q_scale
                 + cb_ref[rows, cols].astype(F32))
            r = r * lax.rsqrt(jnp.mean(r * r, axis=-1, keepdims=True) + NORM_EPS)
            r_ref[rows, cols] = (r * sg_ref[rows, cols].astype(F32)).astype(BF16)
            if blk > 0:
                merge(block_rows(blk - 1), cols)
            _state_update(st_ref, hd, kt_ref[cols, rows], v, k_scale, block_decay)
    for cols in _col_chunks():
        merge(block_rows(n_blocks - 1), cols)
    x2 = x1_ref[...] + _dot(mix_ref[...], wo_ref[...])
    y = _swiglu_half_step(x2, nw_ref, wg_ref, wu_ref, wd_ref, act_ref)
    if final_norm:
        y = _rms(y, fn_ref[...])
    o_ref[...] = y


def _mixer_out(decay_logit, packed, kt, x1, w_b, w_o, norm_w, w_gate, w_up, w_down, final_w, batch, seq):
    steps = seq // RET_ROWS
    rows = pl.BlockSpec((RET_ROWS, D_MODEL), lambda b, j: (b * steps + j, 0))
    packed_rows = pl.BlockSpec((RET_ROWS, N_PACK * D_MODEL), lambda b, j: (b * steps + j, 0))
    lanes = pl.BlockSpec((D_MODEL, RET_ROWS), lambda b, j: (0, b * steps + j))
    in_specs = ([pl.BlockSpec(memory_space=pltpu.SMEM), packed_rows, lanes, rows]
                + [_resident((D_MODEL, D_MODEL)), _resident((D_MODEL, D_MODEL))] + _ffn_weight_specs())
    args = [decay_logit, packed, kt, x1, w_b, w_o, norm_w, w_gate, w_up, w_down]
    if final_w is not None:
        in_specs.append(_resident((1, D_MODEL)))
        args.append(final_w)
    scratch = [((RET_HEADS, RET_DIM, RET_DIM), F32),
               ((RET_HEADS * (RET_ROWS // RET_BLOCK), RET_BLOCK, RET_BLOCK), BF16),
               ((RET_ROWS, D_MODEL), BF16), ((RET_ROWS, D_MODEL), BF16), ((RET_ROWS, D_FF), BF16)]
    vmem_limit = _vmem_limit(
        resident=[(a.shape, a.dtype) for a in args[4:]],
        streamed=[((RET_ROWS, N_PACK * D_MODEL), BF16), ((D_MODEL, RET_ROWS), BF16)]
                 + [((RET_ROWS, D_MODEL), F32)] * 2,
        scratch=scratch)
    return pl.pallas_call(
        functools.partial(_mixer_out_body, final_norm=final_w is not None),
        grid=(batch, steps),
        in_specs=in_specs,
        out_specs=rows,
        out_shape=jax.ShapeDtypeStruct(x1.shape, F32),
        scratch_shapes=[pltpu.VMEM(shape, dtype) for shape, dtype in scratch],
        compiler_params=pltpu.CompilerParams(dimension_semantics=("arbitrary", "arbitrary"),
                                             vmem_limit_bytes=vmem_limit),
        name="mixer_out",
    )(*args)


def kernel(x, ffn1_norm, ffn1_w_gate, ffn1_w_up, ffn1_w_down, mix_norm, w_in, b_in, sgu_norm_g, sgu_norm_b, sgu_w_s, sgu_b_s, ret_decay_logit, w_branch_a, w_branch_b, w_out, ffn2_norm, ffn2_w_gate, ffn2_w_up, ffn2_w_down, final_norm):
    batch, seq, _ = x.shape
    depth = ffn1_norm.shape[0]
    assert depth >= 1 and seq % RET_ROWS == 0 and seq % MIX_ROWS == 0 and (batch * seq) % FFN_ROWS == 0
    tokens = batch * seq
    row = lambda p: p.reshape(1, -1)
    theta = ROPE_BASE ** (-jnp.arange(0, RET_DIM, 2, dtype=F32) / RET_DIM)

    xt = x.reshape(tokens, D_MODEL)
    for l in range(depth):
        last = l == depth - 1
        x1, (w_in_bf, w_a, w_b, w_o, w_gate2, w_up2, w_down2), (cos, sin, cos_t, sin_t) = _ffn(
            xt, row(ffn1_norm[l]), ffn1_w_gate[l], ffn1_w_up[l], ffn1_w_down[l], theta,
            [w_in[l], w_branch_a[l], w_branch_b[l], w_out[l], ffn2_w_gate[l], ffn2_w_up[l], ffn2_w_down[l]], seq)
        packed, kt = _mixer_in(
            ret_decay_logit[l], x1, row(mix_norm[l]), w_in_bf, row(b_in[l]),
            row(sgu_norm_g[l]), row(sgu_norm_b[l]), _paired_spatial_weight(sgu_w_s[l]),
            jnp.tile(sgu_b_s[l], (1, SPATIAL_PAIR))[..., None], w_a, cos, sin, cos_t, sin_t, batch, seq)
        xt = _mixer_out(ret_decay_logit[l], packed, kt, x1, w_b, w_o,
                        row(ffn2_norm[l]), w_gate2, w_up2, w_down2,
                        row(final_norm) if last else None, batch, seq)
    return xt.reshape(batch, seq, D_MODEL)
```
